```python
import math
import jax
import jax.numpy as jnp
from jax import lax
import numpy as np

D_MODEL = 4096
BATCH = 4
SEQ = 2048
DEPTH = 2
DEC_BATCH = 8
DEC_SEQ = 4
PAST_LEN = 16384
PAGE_SIZE = 128

W_MIX = D_MODEL // 4
N_BRANCH = 4
RET_HEADS = 4
RET_DK = W_MIX // RET_HEADS
RET_DV = W_MIX // RET_HEADS
RET_CHUNK = 128
ROPE_BASE = 10000.0
GM_WIDTH = W_MIX
GM_CHUNK = 128
GM_GROUPS = 4
NSA_HEADS = 8
NSA_KV = 2
NSA_HD = W_MIX // NSA_HEADS
NSA_GROUP = NSA_HEADS // NSA_KV
CMP_LEN = 32
CMP_STRIDE = 16
CMP_HID = NSA_HD
SEL_LEN = 64
SEL_TOPK = 16
SEL_QBLK = 64
WINDOW = 512
WIN_QBLK = 128
CONV_W = 31
CONV_C = W_MIX
D_FF = 4 * D_MODEL
EPS = 1e-6
LN_EPS = 1e-5

SPLIT_SIZES = (W_MIX, W_MIX, W_MIX, W_MIX, GM_WIDTH, GM_WIDTH, NSA_HEADS * NSA_HD, 3 * 2 * NSA_KV * NSA_HD, 3 * NSA_HEADS, CONV_C, CONV_C, N_BRANCH * D_MODEL)
N_COLS = sum(SPLIT_SIZES)

kernel_name = 'hybrid_ret_gmlp_nsa_conv_decoder_step'


def split_cols(z):
    offs = np.cumsum(np.array(SPLIT_SIZES))[:-1].tolist()
    return jnp.split(z, offs, axis=-1)


def rmsnorm(x, w):
    xf = x.astype(jnp.float32)
    y = xf * lax.rsqrt(jnp.mean(xf * xf, axis=-1, keepdims=True) + EPS)
    return (y * w.astype(jnp.float32)).astype(x.dtype)


def layernorm(x, w, b):
    xf = x.astype(jnp.float32)
    mu = jnp.mean(xf, axis=-1, keepdims=True)
    var = jnp.mean(jnp.square(xf - mu), axis=-1, keepdims=True)
    y = (xf - mu) * lax.rsqrt(var + LN_EPS)
    return (y * w.astype(jnp.float32) + b.astype(jnp.float32)).astype(x.dtype)


def head_groupnorm(o, w, b):
    B, T, H, dv = o.shape
    mu = jnp.mean(o, axis=-1, keepdims=True)
    var = jnp.mean(jnp.square(o - mu), axis=-1, keepdims=True)
    y = ((o - mu) * lax.rsqrt(var + LN_EPS)).reshape(B, T, H * dv)
    return y * w.astype(jnp.float32) + b.astype(jnp.float32)


def masked_softmax(s, mask):
    s = jnp.where(mask, s.astype(jnp.float32), -jnp.inf)
    m = jnp.max(s, axis=-1, keepdims=True)
    m = jnp.where(jnp.isfinite(m), m, 0.0)
    e = jnp.where(mask, jnp.exp(s - m), 0.0)
    den = jnp.sum(e, axis=-1, keepdims=True)
    return e / jnp.where(den > 0, den, 1.0)


def rotary(x, pos):
    half = x.shape[-1] // 2
    inv = ROPE_BASE ** (-jnp.arange(half, dtype=jnp.float32) / half)
    ang = pos[:, None] * inv[None, :]
    cos = jnp.cos(ang)[:, None, :].astype(x.dtype)
    sin = jnp.sin(ang)[:, None, :].astype(x.dtype)
    x1, x2 = x[..., :half], x[..., half:]
    return jnp.concatenate([x1 * cos - x2 * sin, x2 * cos + x1 * sin], axis=-1)


def retention(q, k, v, r0):
    B, T, H, dk = q.shape
    dv = v.shape[-1]
    c = math.gcd(T, RET_CHUNK)
    n = T // c
    log_g = jnp.log1p(-jnp.exp2(-5.0 - jnp.arange(H, dtype=jnp.float32)))
    i = jnp.arange(c, dtype=jnp.float32)
    diff = i[:, None] - i[None, :]
    dmask = jnp.where(diff >= 0, jnp.exp(jnp.maximum(diff, 0.0)[None] * log_g[:, None, None]), 0.0)
    q_dec = jnp.exp((i[:, None] + 1.0) * log_g[None, :])
    k_dec = jnp.exp((c - 1.0 - i)[:, None] * log_g[None, :])
    c_dec = jnp.exp(c * log_g)

    def to_chunks(a):
        return a.astype(jnp.float32).reshape(B, n, c, H, a.shape[-1]).transpose(1, 0, 2, 3, 4)

    def step(r, inp):
        qc, kc, vc = inp
        s = jnp.einsum('bihd,bjhd->bhij', qc, kc) * dmask[None]
        o = jnp.einsum('bhij,bjhe->bihe', s, vc) + jnp.einsum('bihd,bhde->bihe', qc * q_dec[None, :, :, None], r)
        r = c_dec[None, :, None, None] * r + jnp.einsum('bjhd,bjhe->bhde', kc * k_dec[None, :, :, None], vc)
        return r, o

    r, o = lax.scan(step, r0.astype(jnp.float32), (to_chunks(q), to_chunks(k), to_chunks(v)))
    return o.transpose(1, 0, 2, 3, 4).reshape(B, T, H, dv), r


def gmlp_spatial_gate(u, v, ws, bs):
    B, T, C = v.shape
    n = -(-T // GM_CHUNK)
    tp = n * GM_CHUNK
    vp = jnp.pad(v, ((0, 0), (0, tp - T), (0, 0))).reshape(B, n, GM_CHUNK, GM_GROUPS, C // GM_GROUPS)
    wm = ws * jnp.tril(jnp.ones((GM_CHUNK, GM_CHUNK), ws.dtype))
    s = jnp.einsum('gij,bnjgd->bnigd', wm, vp) + bs.T[None, None, :, :, None]
    return u * s.reshape(B, tp, C)[:, :T]


def compress_blocks(k, pe, w1, w2):
    B, S, KV, hd = k.shape
    n_c = (S - CMP_LEN) // CMP_STRIDE + 1
    idx = jnp.arange(n_c)[:, None] * CMP_STRIDE + jnp.arange(CMP_LEN)[None, :]
    blk = k[:, idx] + pe[None, None, :, None, :]
    flat = blk.transpose(0, 1, 3, 2, 4).reshape(B, n_c, KV, CMP_LEN * hd)
    return jax.nn.gelu(flat @ w1) @ w2


def selected_block_attention(qg, ks, vs, top_i, top_ok, qpos):
    B, T, KV, G, hd = qg.shape
    S = ks.shape[1]
    n_s = -(-S // SEL_LEN)
    n_take = top_i.shape[-1]
    scale = hd ** -0.5

    def to_blocks(a):
        a = jnp.pad(a, ((0, 0), (0, n_s * SEL_LEN - S), (0, 0), (0, 0)))
        return a.reshape(B, n_s, SEL_LEN, KV, hd).transpose(0, 3, 1, 2, 4)

    kb, vb = to_blocks(ks), to_blocks(vs)
    gather = jax.vmap(jax.vmap(lambda a, i: a[i]))
    qb_sz = math.gcd(T, SEL_QBLK)
    nq = T // qb_sz
    qs = qg.reshape(B, nq, qb_sz, KV, G, hd).transpose(1, 0, 2, 3, 4, 5)
    is_ = top_i.reshape(B, KV, nq, qb_sz, n_take).transpose(2, 0, 1, 3, 4)
    oks = top_ok.reshape(B, KV, nq, qb_sz, n_take).transpose(2, 0, 1, 3, 4)
    ps = qpos.reshape(nq, qb_sz)

    def one(args):
        qb, ib, okb, pb = args
        kg = gather(kb, ib)
        vg = gather(vb, ib)
        s = jnp.einsum('btkgd,bktnld->bkgtnl', qb, kg) * scale
        tok = ib[..., None] * SEL_LEN + jnp.arange(SEL_LEN)
        mask = (tok <= pb[None, None, :, None, None]) & okb[..., None]
        p = masked_softmax(s.reshape(B, KV, G, qb_sz, n_take * SEL_LEN), mask.reshape(B, KV, 1, qb_sz, n_take * SEL_LEN))
        p = p.reshape(B, KV, G, qb_sz, n_take, SEL_LEN).astype(qb.dtype)
        return jnp.einsum('bkgtnl,bktnld->btkgd', p, vg)

    o = lax.map(one, (qs, is_, oks, ps))
    return o.transpose(1, 0, 2, 3, 4, 5).reshape(B, T, KV, G, hd)


def window_attention(qg, k_ext, v_ext, q0, n_before):
    B, T, KV, G, hd = qg.shape
    qb_sz = math.gcd(T, WIN_QBLK)
    nq = T // qb_sz
    span = n_before + qb_sz
    scale = hd ** -0.5

    def one(nb):
        s0 = nb * qb_sz
        qb = lax.dynamic_slice_in_dim(qg, s0, qb_sz, axis=1)
        kb = lax.dynamic_slice_in_dim(k_ext, s0, span, axis=1)
        vb = lax.dynamic_slice_in_dim(v_ext, s0, span, axis=1)
        qp = q0 + s0 + jnp.arange(qb_sz)
        kp = q0 - n_before + s0 + jnp.arange(span)
        d = qp[:, None] - kp[None, :]
        mask = (kp[None, :] >= 0) & (d >= 0) & (d <= WINDOW)
        s = jnp.einsum('btkgd,bskd->bkgts', qb, kb) * scale
        p = masked_softmax(s, mask).astype(qb.dtype)
        return jnp.einsum('bkgts,bskd->btkgd', p, vb)

    o = lax.map(one, jnp.arange(nq))
    return o.transpose(1, 0, 2, 3, 4, 5).reshape(B, T, KV, G, hd)


def nsa_attention(q, kv_full, win_ext, gates, q0, n_before, pe, w1, w2):
    B, T, H, hd = q.shape
    S = kv_full.shape[1]
    qg = q.reshape(B, T, NSA_KV, NSA_GROUP, hd)
    qpos = q0 + jnp.arange(T)
    scale = hd ** -0.5
    k_cmp = compress_blocks(kv_full[:, :, 0], pe[0], w1[0], w2[0])
    v_cmp = compress_blocks(kv_full[:, :, 1], pe[1], w1[1], w2[1])
    n_c = k_cmp.shape[1]
    cmp_start = jnp.arange(n_c) * CMP_STRIDE
    s = jnp.einsum('btkgd,bckd->bkgtc', qg, k_cmp) * scale
    p_cmp = masked_softmax(s, (cmp_start + CMP_LEN - 1)[None, :] <= qpos[:, None])
    o_cmp = jnp.einsum('bkgtc,bckd->btkgd', p_cmp.astype(q.dtype), v_cmp)
    n_s = -(-S // SEL_LEN)
    sel_start = jnp.arange(n_s) * SEL_LEN
    overlap = jnp.clip(jnp.minimum(sel_start[:, None] + SEL_LEN, cmp_start[None, :] + CMP_LEN) - jnp.maximum(sel_start[:, None], cmp_start[None, :]), 0, None).astype(jnp.float32)
    imp = jnp.einsum('bkgtc,sc->bkts', p_cmp, overlap)
    blk = jnp.arange(n_s)[None, :]
    cur = (qpos // SEL_LEN)[:, None]
    forced = (blk == 0) | (blk == cur) | (blk == cur - 1)
    valid = sel_start[None, :] <= qpos[:, None]
    score = jnp.where(valid, jnp.where(forced, jnp.inf, imp), -jnp.inf)
    top_v, top_i = lax.top_k(score, min(SEL_TOPK, n_s))
    o_sel = selected_block_attention(qg, kv_full[:, :, 2], kv_full[:, :, 3], top_i, top_v > -jnp.inf, qpos)
    o_win = window_attention(qg, win_ext[:, :, 0], win_ext[:, :, 1], q0, n_before)
    g = gates.reshape(B, T, NSA_KV, NSA_GROUP, 3).astype(q.dtype)
    o = g[..., 0:1] * o_cmp + g[..., 1:2] * o_sel + g[..., 2:3] * o_win
    return o.reshape(B, T, H * hd)


def causal_depthwise_conv(ext, w, b):
    y = lax.conv_general_dilated(ext, w[:, None, :], window_strides=(1,), padding='VALID', dimension_numbers=('NWC', 'WIO', 'NWC'), feature_group_count=ext.shape[-1])
    return y + b


def trunk(x, q0, ret_state, kv_pool, page_table, win_buf, conv_buf, keep_chunk_rows, params):
    (norm1, w_in, ret_gn_w, ret_gn_b, gm_ln_w, gm_ln_b, gm_ws, gm_bs, nsa_pe, nsa_w1, nsa_w2,
     conv_w, conv_b, conv_ln_w, conv_ln_b, w_branch, w_out, norm2, w_up, w_down, final_norm) = params
    B, T, _ = x.shape
    pos = q0 + jnp.arange(T, dtype=jnp.float32)
    n_before = win_buf.shape[2]
    keep = min(WINDOW, q0 + T)
    r_out, kv_out, win_out, conv_out, gm_out = [], [], [], [], []
    for l in range(DEPTH):
        h = rmsnorm(x, norm1[l])
        (rq, rk, rv, rg, gu, gv, nq, nkv, ng, ca, cb, gz) = split_cols(h @ w_in[l])
        rq = rotary(rq.reshape(B, T, RET_HEADS, RET_DK), pos)
        rk = rotary(rk.reshape(B, T, RET_HEADS, RET_DK), pos) * (RET_DK ** -0.5)
        o, r = retention(rq, rk, rv.reshape(B, T, RET_HEADS, RET_DV), ret_state[l])
        o_ret = jax.nn.silu(rg) * head_groupnorm(o, ret_gn_w[l], ret_gn_b[l]).astype(x.dtype)
        gu = jax.nn.gelu(gu)
        gv = layernorm(jax.nn.gelu(gv), gm_ln_w[l], gm_ln_b[l])
        o_gm = gmlp_spatial_gate(gu, gv, gm_ws[l], gm_bs[l])
        nkv = nkv.reshape(B, T, 3, 2, NSA_KV, NSA_HD)
        kv_rows = nkv[:, :, :2].reshape(B, T, 4, NSA_KV, NSA_HD)
        if kv_pool is None:
            kv_full = kv_rows
        else:
            past = kv_pool[l, page_table].reshape(B, -1, 4, NSA_KV, NSA_HD)
            kv_full = jnp.concatenate([past, kv_rows], axis=1)
        win_ext = jnp.concatenate([win_buf[l], nkv[:, :, 2]], axis=1)
        o_nsa = nsa_attention(nq.reshape(B, T, NSA_HEADS, NSA_HD), kv_full, win_ext, jax.nn.sigmoid(ng), q0, n_before, nsa_pe[l], nsa_w1[l], nsa_w2[l])
        conv_ext = jnp.concatenate([conv_buf[l], ca * jax.nn.sigmoid(cb)], axis=1)
        o_conv = jax.nn.silu(layernorm(causal_depthwise_conv(conv_ext, conv_w[l], conv_b[l]), conv_ln_w[l], conv_ln_b[l]))
        br = jnp.stack([o_ret, o_gm, o_nsa, o_conv], axis=2)
        proj = jnp.einsum('btnc,ncd->btnd', br, w_branch[l])
        merged = jnp.sum(jax.nn.sigmoid(gz.reshape(B, T, N_BRANCH, D_MODEL)) * proj, axis=2)
        x = x + merged @ w_out[l]
        h2 = rmsnorm(x, norm2[l])
        x = x + jnp.square(jax.nn.relu(h2 @ w_up[l])) @ w_down[l]
        r_out.append(r.astype(ret_state.dtype))
        kv_out.append(kv_rows)
        win_out.append(win_ext[:, -keep:])
        conv_out.append(conv_ext[:, -(CONV_W - 1):])
        if keep_chunk_rows:
            gm_out.append(gv)
    y = rmsnorm(x, final_norm)
    gm_state = jnp.stack(gm_out) if keep_chunk_rows else None
    return y, jnp.stack(r_out), jnp.stack(kv_out), jnp.stack(win_out), jnp.stack(conv_out), gm_state


def setup_inputs(seed: int = 0) -> dict:
    key = jax.random.key(seed)
    ks = jax.random.split(key, 32)
    f32 = jnp.float32

    def nrm(k, shape, scale):
        return jax.random.normal(k, shape, f32) * scale

    n_pages = PAST_LEN // PAGE_SIZE
    n_used = DEC_BATCH * n_pages
    n_pool = n_used + max(1, n_used // 4)
    win_rows = min(WINDOW, PAST_LEN)
    page_table = jax.random.permutation(ks[0], n_pool)[:n_used].reshape(DEC_BATCH, n_pages).astype(jnp.int32)
    return {
        'x_prompt': nrm(ks[1], (BATCH, SEQ, D_MODEL), 1.0),
        'x_sample': nrm(ks[2], (DEC_BATCH, DEC_SEQ, D_MODEL), 1.0),
        'cache_nsa_kv': nrm(ks[3], (DEPTH, n_pool, PAGE_SIZE, 4, NSA_KV, NSA_HD), 1.0),
        'state_ret': nrm(ks[4], (DEPTH, DEC_BATCH, RET_HEADS, RET_DK, RET_DV), 0.3),
        'state_win_kv': nrm(ks[5], (DEPTH, DEC_BATCH, win_rows, 2, NSA_KV, NSA_HD), 1.0),
        'state_conv': nrm(ks[6], (DEPTH, DEC_BATCH, CONV_W - 1, CONV_C), 0.5),
        'page_table': page_table,
        'norm1': 1.0 + nrm(ks[7], (DEPTH, D_MODEL), 0.02),
        'w_in': nrm(ks[8], (DEPTH, D_MODEL, N_COLS), D_MODEL ** -0.5),
        'ret_gn_w': 1.0 + nrm(ks[9], (DEPTH, W_MIX), 0.02),
        'ret_gn_b': nrm(ks[10], (DEPTH, W_MIX), 0.02),
        'gm_ln_w': 1.0 + nrm(ks[11], (DEPTH, GM_WIDTH), 0.02),
        'gm_ln_b': nrm(ks[12], (DEPTH, GM_WIDTH), 0.02),
        'gm_ws': nrm(ks[13], (DEPTH, GM_GROUPS, GM_CHUNK, GM_CHUNK), GM_CHUNK ** -0.5),
        'gm_bs': 1.0 + nrm(ks[14], (DEPTH, GM_GROUPS, GM_CHUNK), 0.02),
        'nsa_pe': nrm(ks[15], (DEPTH, 2, CMP_LEN, NSA_HD), 0.1),
        'nsa_w1': nrm(ks[16], (DEPTH, 2, CMP_LEN * NSA_HD, CMP_HID), (CMP_LEN * NSA_HD) ** -0.5),
        'nsa_w2': nrm(ks[17], (DEPTH, 2, CMP_HID, NSA_HD), CMP_HID ** -0.5),
        'conv_w': nrm(ks[18], (DEPTH, CONV_W, CONV_C), CONV_W ** -0.5),
        'conv_b': nrm(ks[19], (DEPTH, CONV_C), 0.02),
        'conv_ln_w': 1.0 + nrm(ks[20], (DEPTH, CONV_C), 0.02),
        'conv_ln_b': nrm(ks[21], (DEPTH, CONV_C), 0.02),
        'w_branch': nrm(ks[22], (DEPTH, N_BRANCH, W_MIX, D_MODEL), W_MIX ** -0.5),
        'w_out': nrm(ks[23], (DEPTH, D_MODEL, D_MODEL), D_MODEL ** -0.5),
        'norm2': 1.0 + nrm(ks[24], (DEPTH, D_MODEL), 0.02),
        'w_up': nrm(ks[25], (DEPTH, D_MODEL, D_FF), D_MODEL ** -0.5),
        'w_down': nrm(ks[26], (DEPTH, D_FF, D_MODEL), D_FF ** -0.5),
        'final_norm': 1.0 + nrm(ks[27], (D_MODEL,), 0.02),
    }


def reference(x_prompt, x_sample, cache_nsa_kv, state_ret, state_win_kv, state_conv, page_table,
              norm1, w_in, ret_gn_w, ret_gn_b, gm_ln_w, gm_ln_b, gm_ws, gm_bs, nsa_pe, nsa_w1, nsa_w2,
              conv_w, conv_b, conv_ln_w, conv_ln_b, w_branch, w_out, norm2, w_up, w_down, final_norm):
    params = (norm1, w_in, ret_gn_w, ret_gn_b, gm_ln_w, gm_ln_b, gm_ws, gm_bs, nsa_pe, nsa_w1, nsa_w2,
              conv_w, conv_b, conv_ln_w, conv_ln_b, w_branch, w_out, norm2, w_up, w_down, final_norm)
    dt = x_prompt.dtype
    B = x_prompt.shape[0]
    ret0 = jnp.zeros((DEPTH, B, RET_HEADS, RET_DK, RET_DV), dt)
    win0 = jnp.zeros((DEPTH, B, WINDOW, 2, NSA_KV, NSA_HD), dt)
    conv0 = jnp.zeros((DEPTH, B, CONV_W - 1, CONV_C), dt)
    y_prompt, ret_p, kv_p, win_p, conv_p, _ = trunk(x_prompt, 0, ret0, None, None, win0, conv0, False, params)
    past_len = page_table.shape[1] * PAGE_SIZE
    y_sample, ret_s, kv_s, win_s, conv_s, gm_s = trunk(x_sample, past_len, state_ret, cache_nsa_kv, page_table, state_win_kv, state_conv, True, params)
    return (y_prompt, y_sample, ret_p, ret_s, kv_p, kv_s, win_p, win_s, conv_p, conv_s, gm_s)
```

```python
import functools
import math

import numpy as np
import jax
import jax.numpy as jnp
from jax import lax
from jax.experimental import pallas as pl
from jax.experimental.pallas import tpu as pltpu

F32 = jnp.float32
BF16 = jnp.bfloat16

D_MODEL = 4096
DEPTH = 2
PAGE_SIZE = 128
W_MIX = D_MODEL // 4
N_BRANCH = 4
RET_HEADS = 4
RET_DK = W_MIX // RET_HEADS
RET_DV = W_MIX // RET_HEADS
RET_CHUNK = 128
ROPE_BASE = 10000.0
GM_CHUNK = 128
GM_GROUPS = 4
NSA_HEADS = 8
NSA_KV = 2
NSA_HD = W_MIX // NSA_HEADS
NSA_GROUP = NSA_HEADS // NSA_KV
CMP_LEN = 32
CMP_STRIDE = 16
SEL_LEN = 64
SEL_SHIFT = 6
SEL_TOPK = 16
WINDOW = 512
CONV_W = 31
D_FF = 4 * D_MODEL
EPS = 1e-6
LN_EPS = 1e-5

LANE = 128
SUBLANE = 8
SLAB = 8
VMEM_BIG = 56 * 1024 * 1024

C_RQ, C_RK, C_RV, C_RG = 0, 1024, 2048, 3072
C_GU, C_GV = 4096, 5120
C_NQ = 6144
C_CA, C_CB = 7168, 8192
C_GZ = 9216
C_NKV = 25600
C_NG = 27136
N_PAD = 27648
_O_NKV, _O_NG, _O_CA, _O_GZ, _O_END = 7168, 8704, 8728, 10776, 27160

NEG = -1e30


def _round_up(a, b):
    return -(-a // b) * b


def _tile(m, pref):
    best = None
    for t in range(SUBLANE, min(m, pref) + 1, SUBLANE):
        if m % t == 0:
            best = t
    assert best is not None, (m, pref)
    return best


def _act_dtype(rows):
    return BF16 if rows % (2 * SUBLANE) == 0 else F32


def _params(sem, vmem=None):
    return pltpu.CompilerParams(dimension_semantics=sem, vmem_limit_bytes=vmem)


def _gelu(x):
    return 0.5 * x * (1.0 + jnp.tanh(0.7978845608028654 * (x + 0.044715 * (x * x * x))))


def _sigmoid(x):
    return 1.0 / (1.0 + jnp.exp(-x))


def _layernorm(x, w, b):
    mu = jnp.mean(x, axis=-1, keepdims=True)
    xc = x - mu
    var = jnp.mean(xc * xc, axis=-1, keepdims=True)
    return xc * lax.rsqrt(var + LN_EPS) * w + b


def _softmax_lanes(s, mask):
    sm = jnp.where(mask, s, NEG)
    m = jnp.max(sm, axis=-1, keepdims=True)
    m = jnp.where(m > 0.5 * NEG, m, 0.0)
    e = jnp.where(mask, jnp.exp(sm - m), 0.0)
    den = jnp.sum(e, axis=-1, keepdims=True)
    return e / jnp.where(den > 0.0, den, 1.0)


def _dot(a, b):
    return jnp.dot(a, b, preferred_element_type=F32)


def _dot_nt(a, b):
    return lax.dot_general(a, b, (((1,), (1,)), ((), ())), preferred_element_type=F32)


def _dot_tn(a, b):
    return lax.dot_general(a, b, (((0,), (0,)), ((), ())), preferred_element_type=F32)


def _dot_split3(a, b_bf16):
    a1 = a.astype(BF16)
    r1 = a - a1.astype(F32)
    a2 = r1.astype(BF16)
    a3 = (r1 - a2.astype(F32)).astype(BF16)
    return _dot(a1, b_bf16) + _dot(a2, b_bf16) + _dot(a3, b_bf16)


def _rmsnorm_kernel(x_ref, w_ref, o_ref, *, slab, valid):
    x = x_ref[...]
    y = x * lax.rsqrt(jnp.mean(x * x, axis=-1, keepdims=True) + EPS) * w_ref[...]
    if valid < slab:
        row = lax.broadcasted_iota(jnp.int32, y.shape, 0)
        y = jnp.where((row & (slab - 1)) < valid, y, 0.0)
    o_ref[...] = y.astype(o_ref.dtype)


def _rmsnorm(x, w, *, slab, valid, out_dtype):
    m, d = x.shape
    tr = _tile(m, 256)
    assert tr % slab == 0 or valid == slab
    return pl.pallas_call(
        functools.partial(_rmsnorm_kernel, slab=slab, valid=valid),
        grid=(m // tr,),
        in_specs=[pl.BlockSpec((tr, d), lambda i: (i, 0)), pl.BlockSpec((1, d), lambda i: (0, 0))],
        out_specs=pl.BlockSpec((tr, d), lambda i: (i, 0)),
        out_shape=jax.ShapeDtypeStruct((m, d), out_dtype),
        compiler_params=_params(("parallel",)),
        name="rmsnorm",
    )(x, w.reshape(1, d))


def _mm_kernel(x_ref, w_ref, o_ref, *, act):
    acc = _dot(x_ref[...], w_ref[...])
    if act == "relu2":
        acc = jnp.square(jnp.maximum(acc, 0.0))
    o_ref[...] = acc.astype(o_ref.dtype)


def _mm_res_kernel(x_ref, w_ref, r_ref, o_ref):
    o_ref[...] = r_ref[...] + _dot(x_ref[...], w_ref[...])


def _mm_kres_kernel(x_ref, w_ref, r_ref, o_ref):
    @pl.when(pl.program_id(2) == 0)
    def _():
        o_ref[...] = r_ref[...]

    o_ref[...] += _dot(x_ref[...], w_ref[...])


def _matmul(x, w, *, tn, act=None, out_dtype=F32, res=None, name="matmul"):
    m, k = x.shape
    n = w.shape[1]
    tm = _tile(m, 1024)
    grid = (m // tm, n // tn)
    in_specs = [pl.BlockSpec((tm, k), lambda i, j: (i, 0)), pl.BlockSpec((k, tn), lambda i, j: (0, j))]
    args = [x, w]
    if res is None:
        body = functools.partial(_mm_kernel, act=act)
    else:
        body = _mm_res_kernel
        in_specs.append(pl.BlockSpec((tm, tn), lambda i, j: (i, j)))
        args.append(res)
    return pl.pallas_call(
        body,
        grid=grid,
        in_specs=in_specs,
        out_specs=pl.BlockSpec((tm, tn), lambda i, j: (i, j)),
        out_shape=jax.ShapeDtypeStruct((m, n), out_dtype),
        compiler_params=_params(("parallel", "parallel"), VMEM_BIG),
        name=name,
    )(*args)


def _matmul_kres(x, w, res, *, tn, tk, name="matmul_k"):
    m, k = x.shape
    n = w.shape[1]
    tm = _tile(m, 1024)
    return pl.pallas_call(
        _mm_kres_kernel,
        grid=(m // tm, n // tn, k // tk),
        in_specs=[
            pl.BlockSpec((tm, tk), lambda i, j, kk: (i, kk)),
            pl.BlockSpec((tk, tn), lambda i, j, kk: (kk, j)),
            pl.BlockSpec((tm, tn), lambda i, j, kk: (i, j)),
        ],
        out_specs=pl.BlockSpec((tm, tn), lambda i, j, kk: (i, j)),
        out_shape=jax.ShapeDtypeStruct((m, n), F32),
        compiler_params=_params(("parallel", "parallel", "arbitrary"), VMEM_BIG),
        name=name,
    )(x, w, res)


def _ret_kernel(q_ref, k_ref, v_ref, g_ref, cos_ref, sin_ref, dm_ref, qd_ref, kd_ref, cd_ref, gw_ref, gb_ref,
                r0_ref, o_ref, ro_ref, r_sc, *, rows):
    ci = pl.program_id(2)

    @pl.when(ci == 0)
    def _():
        r_sc[...] = r0_ref[0, 0]

    cos = cos_ref[...]
    sin = sin_ref[...]
    half = RET_DK // 2

    def rot(x):
        x1, x2 = x[:, :half], x[:, half:]
        return jnp.concatenate([x1 * cos - x2 * sin, x2 * cos + x1 * sin], axis=-1)

    def pad(x):
        if rows == RET_CHUNK:
            return x
        return jnp.concatenate([x, jnp.zeros((RET_CHUNK - rows, x.shape[1]), x.dtype)], axis=0)

    q = pad(rot(q_ref[...]))
    k = pad(rot(k_ref[...]) * (RET_DK ** -0.5))
    v = pad(v_ref[...])
    vb = v.astype(BF16)
    r = r_sc[...]
    s = _dot_nt(q.astype(BF16), k.astype(BF16)) * dm_ref[0]
    o = _dot(s.astype(BF16), vb) + _dot((q * qd_ref[0]).astype(BF16), r.astype(BF16))
    r_new = cd_ref[0] * r + _dot_tn((k * kd_ref[0]).astype(BF16), vb)
    r_sc[...] = r_new
    ro_ref[0, 0] = r_new
    o = o[:rows]
    mu = jnp.mean(o, axis=-1, keepdims=True)
    oc = o - mu
    var = jnp.mean(oc * oc, axis=-1, keepdims=True)
    y = oc * lax.rsqrt(var + LN_EPS) * gw_ref[...] + gb_ref[...]
    g = g_ref[...]
    o_ref[...] = (g * _sigmoid(g) * y).astype(o_ref.dtype)


def _ret_tables(c_eff):
    log_g = np.log1p(-np.exp2(-5.0 - np.arange(RET_HEADS, dtype=np.float64)))
    i = np.arange(RET_CHUNK, dtype=np.float64)
    live = i < c_eff
    diff = i[:, None] - i[None, :]
    dmask = np.where(diff >= 0, np.exp(np.maximum(diff, 0.0)[None] * log_g[:, None, None]), 0.0)
    dmask = dmask * (live[:, None] & live[None, :])[None]
    q_dec = np.exp((i[None, :] + 1.0) * log_g[:, None]) * live[None]
    k_dec = np.exp((c_eff - 1.0 - i)[None, :] * log_g[:, None]) * live[None]
    c_dec = np.exp(c_eff * log_g)
    bc = lambda a: np.broadcast_to(a[:, :, None], (RET_HEADS, RET_CHUNK, RET_DK))
    return (jnp.asarray(dmask, F32), jnp.asarray(bc(q_dec), F32), jnp.asarray(bc(k_dec), F32),
            jnp.asarray(np.broadcast_to(c_dec[:, None, None], (RET_HEADS, 1, RET_DK)), F32))


def _rope_tables(positions, rows):
    half = RET_DK // 2
    inv = ROPE_BASE ** (-np.arange(half, dtype=np.float64) / half)
    ang = np.asarray(positions, np.float64)[:, None] * inv[None, :]
    cos = np.zeros((rows, half)); sin = np.zeros((rows, half))
    cos[:len(positions)] = np.cos(ang); sin[:len(positions)] = np.sin(ang)
    return jnp.asarray(cos, F32), jnp.asarray(sin, F32)


def _retention(z, r0, gn_w, gn_b, *, nb, rows, n_chunks, c_eff, cos, sin):
    m = z.shape[0]
    dmask, q_dec, k_dec, c_dec = _ret_tables(c_eff)
    zspec = lambda off: pl.BlockSpec((rows, RET_DK), lambda b, h, c: (b * n_chunks + c, off // RET_DK + h))
    tab = pl.BlockSpec((rows, RET_DK // 2), lambda b, h, c: (c, 0))
    per_h = lambda r: pl.BlockSpec((1, r, RET_DK), lambda b, h, c: (h, 0, 0))
    return pl.pallas_call(
        functools.partial(_ret_kernel, rows=rows),
        grid=(nb, RET_HEADS, n_chunks),
        in_specs=[zspec(C_RQ), zspec(C_RK), zspec(C_RV), zspec(C_RG), tab, tab,
                  pl.BlockSpec((1, RET_CHUNK, RET_CHUNK), lambda b, h, c: (h, 0, 0)),
                  per_h(RET_CHUNK), per_h(RET_CHUNK), per_h(1),
                  pl.BlockSpec((1, RET_DV), lambda b, h, c: (0, h)),
                  pl.BlockSpec((1, RET_DV), lambda b, h, c: (0, h)),
                  pl.BlockSpec((1, 1, RET_DK, RET_DV), lambda b, h, c: (b, h, 0, 0))],
        out_specs=[pl.BlockSpec((rows, RET_DV), lambda b, h, c: (b * n_chunks + c, h)),
                   pl.BlockSpec((1, 1, RET_DK, RET_DV), lambda b, h, c: (b, h, 0, 0))],
        out_shape=[jax.ShapeDtypeStruct((m, W_MIX), _act_dtype(rows)),
                   jax.ShapeDtypeStruct((nb, RET_HEADS, RET_DK, RET_DV), F32)],
        scratch_shapes=[pltpu.VMEM((RET_DK, RET_DV), F32)],
        compiler_params=_params(("parallel", "parallel", "arbitrary")),
        name="retention",
    )(z, z, z, z, cos, sin, dmask, q_dec, k_dec, c_dec, gn_w.reshape(1, W_MIX), gn_b.reshape(1, W_MIX), r0)


def _gm_kernel(u_ref, v_ref, lw_ref, lb_ref, ws_ref, bst_ref, o_ref, gv_ref, *, rows):
    u = _gelu(u_ref[...])
    v = _layernorm(_gelu(v_ref[...]), lw_ref[...], lb_ref[...])
    gv_ref[...] = v
    if rows < GM_CHUNK:
        v = jnp.concatenate([v, jnp.zeros((GM_CHUNK - rows, v.shape[1]), F32)], axis=0)
    ri = lax.broadcasted_iota(jnp.int32, (GM_CHUNK, GM_CHUNK), 0)
    cj = lax.broadcasted_iota(jnp.int32, (GM_CHUNK, GM_CHUNK), 1)
    gw = W_MIX // GM_GROUPS
    for g in range(GM_GROUPS):
        wm = jnp.where(cj <= ri, ws_ref[g], 0.0).astype(BF16)
        s = _dot(wm, v[:, g * gw:(g + 1) * gw].astype(BF16)) + bst_ref[:, g:g + 1]
        o_ref[:, g * gw:(g + 1) * gw] = (u[:, g * gw:(g + 1) * gw] * s[:rows]).astype(o_ref.dtype)


def _gmlp(z, ln_w, ln_b, ws, bs, *, rows):
    m = z.shape[0]
    row = lambda: pl.BlockSpec((1, W_MIX), lambda i: (0, 0))
    return pl.pallas_call(
        functools.partial(_gm_kernel, rows=rows),
        grid=(m // rows,),
        in_specs=[pl.BlockSpec((rows, W_MIX), lambda i: (i, C_GU // W_MIX)),
                  pl.BlockSpec((rows, W_MIX), lambda i: (i, C_GV // W_MIX)),
                  row(), row(),
                  pl.BlockSpec((GM_GROUPS, GM_CHUNK, GM_CHUNK), lambda i: (0, 0, 0)),
                  pl.BlockSpec((GM_CHUNK, GM_GROUPS), lambda i: (0, 0))],
        out_specs=[pl.BlockSpec((rows, W_MIX), lambda i: (i, 0)), pl.BlockSpec((rows, W_MIX), lambda i: (i, 0))],
        out_shape=[jax.ShapeDtypeStruct((m, W_MIX), _act_dtype(rows)), jax.ShapeDtypeStruct((m, W_MIX), F32)],
        compiler_params=_params(("parallel",)),
        name="gmlp",
    )(z, z, ln_w.reshape(1, W_MIX), ln_b.reshape(1, W_MIX), ws, bs.T)


HALO = 32
CONV_RB = 64


def _conv_kernel(a_ref, b_ref, buf_ref, cw_ref, cb_ref, lw_ref, lb_ref, o_ref, ext_ref, ext_sc, y_sc, *, rows):
    ti = pl.program_id(1)

    @pl.when(ti == 0)
    def _():
        ext_sc[0:HALO, :] = buf_ref[0]

    @pl.when(ti > 0)
    def _():
        ext_sc[0:HALO, :] = ext_sc[rows:rows + HALO, :]

    ext_sc[HALO:HALO + rows, :] = a_ref[...] * _sigmoid(b_ref[...])
    rb = min(CONV_RB, rows)
    first = HALO - (CONV_W - 1)
    for cc in range(W_MIX // LANE):
        lanes = slice(cc * LANE, (cc + 1) * LANE)
        for r0 in range(0, rows, rb):
            acc = jnp.broadcast_to(cb_ref[:, lanes], (rb, LANE))
            for w in range(CONV_W):
                acc = acc + ext_sc[first + w + r0:first + w + r0 + rb, lanes] * cw_ref[w:w + 1, lanes]
            y_sc[r0:r0 + rb, lanes] = acc
    y = _layernorm(y_sc[...], lw_ref[...], lb_ref[...])
    o_ref[...] = (y * _sigmoid(y)).astype(o_ref.dtype)
    ext_ref[0] = ext_sc[...]


def _conv_module(z, buf, cw, cb, ln_w, ln_b, *, nb, rows, n_tiles):
    m = z.shape[0]
    row = lambda: pl.BlockSpec((1, W_MIX), lambda b, t: (0, 0))
    cwp = jnp.concatenate([cw, jnp.zeros((HALO - CONV_W, W_MIX), F32)], axis=0)
    return pl.pallas_call(
        functools.partial(_conv_kernel, rows=rows),
        grid=(nb, n_tiles),
        in_specs=[pl.BlockSpec((rows, W_MIX), lambda b, t: (b * n_tiles + t, C_CA // W_MIX)),
                  pl.BlockSpec((rows, W_MIX), lambda b, t: (b * n_tiles + t, C_CB // W_MIX)),
                  pl.BlockSpec((1, HALO, W_MIX), lambda b, t: (b, 0, 0)),
                  pl.BlockSpec((HALO, W_MIX), lambda b, t: (0, 0)),
                  row(), row(), row()],
        out_specs=[pl.BlockSpec((rows, W_MIX), lambda b, t: (b * n_tiles + t, 0)),
                   pl.BlockSpec((1, HALO + rows, W_MIX), lambda b, t: (b, 0, 0))],
        out_shape=[jax.ShapeDtypeStruct((m, W_MIX), _act_dtype(rows)),
                   jax.ShapeDtypeStruct((nb, HALO + rows, W_MIX), F32)],
        scratch_shapes=[pltpu.VMEM((HALO + rows, W_MIX), F32), pltpu.VMEM((rows, W_MIX), F32)],
        compiler_params=_params(("parallel", "arbitrary")),
        name="conv_module",
    )(z, z, buf, cwp, cb.reshape(1, W_MIX), ln_w.reshape(1, W_MIX), ln_b.reshape(1, W_MIX))


def _overlap_t(n_c_pad, n_s_pad, n_c, n_s):
    cs = np.arange(n_c_pad)[:, None] * CMP_STRIDE
    ss = np.arange(n_s_pad)[None, :] * SEL_LEN
    ov = np.clip(np.minimum(ss + SEL_LEN, cs + CMP_LEN) - np.maximum(ss, cs), 0, None).astype(np.float64)
    ov = ov * (np.arange(n_c_pad)[:, None] < n_c) * (np.arange(n_s_pad)[None, :] < n_s)
    return jnp.asarray(ov, BF16)


def _cmp_p_kernel(x_ref, pe_ref, w1_ref, w2_ref, o_ref, xs_sc, *, t, ncp):
    xs_sc[0:t, :] = x_ref[...]
    xs_sc[t:, :] = jnp.zeros((xs_sc.shape[0] - t, NSA_HD), F32)
    acc = jnp.zeros((ncp, NSA_HD), F32)
    for l in range(CMP_LEN):
        rows = xs_sc[pl.ds(l, ncp, stride=CMP_STRIDE), :] + pe_ref[0, l:l + 1, :]
        acc = acc + _dot(rows.astype(BF16), w1_ref[0, l * NSA_HD:(l + 1) * NSA_HD, :])
    o_ref[0, 0] = _dot(_gelu(acc).astype(BF16), w2_ref[0])


def _compress_prompt(z, pe, w1, w2, *, nb, t):
    ncp = _round_up(t // CMP_STRIDE, LANE)
    pad_rows = _round_up(CMP_STRIDE * (ncp - 1) + CMP_LEN, SUBLANE)
    return pl.pallas_call(
        functools.partial(_cmp_p_kernel, t=t, ncp=ncp),
        grid=(nb, 4),
        in_specs=[pl.BlockSpec((t, NSA_HD), lambda b, j: (b, C_NKV // NSA_HD + j)),
                  pl.BlockSpec((1, CMP_LEN, NSA_HD), lambda b, j: (j // 2, 0, 0)),
                  pl.BlockSpec((1, CMP_LEN * NSA_HD, NSA_HD), lambda b, j: (j // 2, 0, 0)),
                  pl.BlockSpec((1, NSA_HD, NSA_HD), lambda b, j: (j // 2, 0, 0))],
        out_specs=pl.BlockSpec((1, 1, ncp, NSA_HD), lambda b, j: (b, j, 0, 0)),
        out_shape=jax.ShapeDtypeStruct((nb, 4, ncp, NSA_HD), F32),
        scratch_shapes=[pltpu.VMEM((max(pad_rows, t + SUBLANE), NSA_HD), F32)],
        compiler_params=_params(("parallel", "parallel")),
        name="nsa_compress_prompt",
    )(z, pe, w1, w2)


def _nsa_p_kernel(q_ref, kc_ref, vc_ref, ks_ref, vs_ref, kw_ref, vw_ref, ng_ref, ov_ref, ex_ref, o_ref,
                  *, t, tq, n_c, n_s, span):
    qi = pl.program_id(2)
    kv = pl.program_id(1)
    scale = NSA_HD ** -0.5
    g_n = NSA_GROUP
    q = q_ref[...]
    q4 = jnp.concatenate([q[:, g * NSA_HD:(g + 1) * NSA_HD] for g in range(g_n)], axis=0).astype(BF16)
    qpos = qi * tq + lax.broadcasted_iota(jnp.int32, (tq, 1), 0)
    qpos4 = jnp.concatenate([qpos] * g_n, axis=0)

    kc = kc_ref[0, 0].astype(BF16)
    ncp = kc.shape[0]
    s_c = _dot_nt(q4, kc) * scale
    cidx = lax.broadcasted_iota(jnp.int32, (1, ncp), 1)
    mask_c = (cidx * CMP_STRIDE + CMP_LEN - 1 <= qpos4) & (cidx < n_c)
    p_c = _softmax_lanes(s_c, mask_c)
    o_c = _dot(p_c.astype(BF16), vc_ref[0, 0].astype(BF16))

    p_sum = p_c[0:tq]
    for g in range(1, g_n):
        p_sum = p_sum + p_c[g * tq:(g + 1) * tq]
    imp = _dot_split3(p_sum, ov_ref[...])
    nsp = imp.shape[1]
    blk = lax.broadcasted_iota(jnp.int32, (tq, nsp), 1)
    cur = jnp.right_shift(qpos, SEL_SHIFT)
    forced = (blk == 0) | (blk == cur) | (blk == cur - 1)
    valid = (blk * SEL_LEN <= qpos) & (blk < n_s)
    score = jnp.where(valid, jnp.where(forced, jnp.inf, imp), -jnp.inf)
    rank = jnp.zeros((tq, nsp), F32)
    for j in range(n_s):
        sj = score[:, j:j + 1]
        ahead = (sj > score) | ((sj == score) & (blk > j))
        rank = rank + jnp.where(ahead, 1.0, 0.0)
    sel = valid & (rank < float(min(SEL_TOPK, n_s)))

    sel_f = jnp.where(sel, 1.0, 0.0)
    sel4 = jnp.concatenate([sel_f] * g_n, axis=0).astype(BF16)
    kidx = lax.broadcasted_iota(jnp.int32, (1, t), 1)
    mask_s4 = (_dot(sel4, ex_ref[...]) > 0.5) & (kidx <= qpos4)
    s_s = _dot_nt(q4, ks_ref[...].astype(BF16)) * scale
    p_s = _softmax_lanes(s_s, mask_s4)
    o_s = _dot(p_s.astype(BF16), vs_ref[...].astype(BF16))

    w0 = pl.multiple_of(jnp.clip(qi * tq - WINDOW, 0, t - span), LANE)
    kw = kw_ref[pl.ds(w0, span), :].astype(BF16)
    vw = vw_ref[pl.ds(w0, span), :].astype(BF16)
    kpos = w0 + lax.broadcasted_iota(jnp.int32, (1, span), 1)
    dist = qpos4 - kpos
    mask_w = (dist >= 0) & (dist <= WINDOW)
    s_w = _dot_nt(q4, kw) * scale
    p_w = _softmax_lanes(s_w, mask_w)
    o_w = _dot(p_w.astype(BF16), vw)

    gates = _sigmoid(ng_ref[...])
    for g in range(g_n):
        rows = slice(g * tq, (g + 1) * tq)
        col = (kv * g_n + g) * 3
        lane = lax.broadcasted_iota(jnp.int32, gates.shape, 1)
        pick = lambda j: jnp.sum(jnp.where(lane == col + j, gates, 0.0), axis=-1, keepdims=True)
        o = pick(0) * o_c[rows] + pick(1) * o_s[rows] + pick(2) * o_w[rows]
        o_ref[:, g * NSA_HD:(g + 1) * NSA_HD] = o.astype(o_ref.dtype)


def _nsa_prompt(z, cmp_kv, *, nb, t):
    m = z.shape[0]
    tq = 128
    nq = t // tq
    n_c = (t - CMP_LEN) // CMP_STRIDE + 1
    n_s = -(-t // SEL_LEN)
    ncp = cmp_kv.shape[2]
    nsp = LANE
    assert n_s <= nsp
    span = min(WINDOW + tq, t)
    ov = _overlap_t(ncp, nsp, n_c, n_s)
    ex = jnp.asarray((np.arange(t)[None, :] // SEL_LEN) == np.arange(nsp)[:, None], BF16)
    kvw = NSA_GROUP * NSA_HD
    nk = C_NKV // NSA_HD
    full = lambda j0: pl.BlockSpec((t, NSA_HD), lambda b, kv, qi: (b, nk + j0 + kv))
    return pl.pallas_call(
        functools.partial(_nsa_p_kernel, t=t, tq=tq, n_c=n_c, n_s=n_s, span=span),
        grid=(nb, NSA_KV, nq),
        in_specs=[pl.BlockSpec((tq, kvw), lambda b, kv, qi: (b * nq + qi, C_NQ // kvw + kv)),
                  pl.BlockSpec((1, 1, ncp, NSA_HD), lambda b, kv, qi: (b, kv, 0, 0)),
                  pl.BlockSpec((1, 1, ncp, NSA_HD), lambda b, kv, qi: (b, 2 + kv, 0, 0)),
                  full(4), full(6), full(8), full(10),
                  pl.BlockSpec((tq, LANE), lambda b, kv, qi: (b * nq + qi, C_NG // LANE)),
                  pl.BlockSpec((ncp, nsp), lambda b, kv, qi: (0, 0)),
                  pl.BlockSpec((nsp, t), lambda b, kv, qi: (0, 0))],
        out_specs=pl.BlockSpec((tq, kvw), lambda b, kv, qi: (b * nq + qi, kv)),
        out_shape=jax.ShapeDtypeStruct((m, W_MIX), BF16),
        compiler_params=_params(("parallel", "parallel", "arbitrary"), VMEM_BIG),
        name="nsa_prompt",
    )(z, cmp_kv, cmp_kv, z, z, z, z, z, ov, ex)


PAGES_PER_STEP = 8
PAGE_ROWS = PAGE_SIZE * 4 * NSA_KV
GROUPS_PER_PAGE = PAGE_SIZE // CMP_STRIDE


def _s_cmp_kernel(pt_ref, *refs):
    del pt_ref
    pages = refs[:PAGES_PER_STEP]
    w_ref, o_ref = refs[PAGES_PER_STEP], refs[PAGES_PER_STEP + 1]
    gp = PAGES_PER_STEP * GROUPS_PER_PAGE
    for which in range(2):
        acc = jnp.zeros((NSA_KV * gp, 2 * NSA_HD), F32)
        for l in range(CMP_STRIDE):
            pieces = [pages[p][pl.ds(l * 4 * NSA_KV + which * NSA_KV + kv, GROUPS_PER_PAGE, stride=CMP_STRIDE * 4 * NSA_KV), :]
                      for kv in range(NSA_KV) for p in range(PAGES_PER_STEP)]
            acc = acc + _dot(jnp.concatenate(pieces, axis=0).astype(BF16), w_ref[which, l])
        for kv in range(NSA_KV):
            o_ref[0, which, kv] = acc[kv * gp:(kv + 1) * gp]


def _page_specs(layer, n):
    def spec(p):
        return pl.BlockSpec((None, None, PAGE_ROWS, NSA_HD),
                            lambda b, s, pt: (layer, pt[b, s * PAGES_PER_STEP + p], 0, 0))
    return [spec(p) for p in range(n)]


def _compress_sample_partial(cache4, page_table, w1ab, *, layer, nb, n_pages):
    steps = n_pages // PAGES_PER_STEP
    gp = PAGES_PER_STEP * GROUPS_PER_PAGE
    ng = n_pages * GROUPS_PER_PAGE
    return pl.pallas_call(
        _s_cmp_kernel,
        grid_spec=pltpu.PrefetchScalarGridSpec(
            num_scalar_prefetch=1,
            grid=(nb, steps),
            in_specs=_page_specs(layer, PAGES_PER_STEP)
            + [pl.BlockSpec((2, CMP_STRIDE, NSA_HD, 2 * NSA_HD), lambda b, s, pt: (0, 0, 0, 0))],
            out_specs=pl.BlockSpec((1, 2, NSA_KV, gp, 2 * NSA_HD), lambda b, s, pt: (b, 0, 0, s, 0)),
        ),
        out_shape=jax.ShapeDtypeStruct((nb, 2, NSA_KV, ng, 2 * NSA_HD), F32),
        compiler_params=_params(("parallel", "arbitrary"), VMEM_BIG),
        name="nsa_compress_sample",
    )(page_table, *([cache4] * PAGES_PER_STEP), w1ab)


def _s_sel_kernel(uv_ref, pef_ref, w1_ref, w2_ref, q_ref, ng_ref, kwn_ref, vwn_ref, kwp_ref, vwp_ref, ov_ref,
                  selt_ref, op_ref, *, q0, valid_rows, n_c, n_s):
    kv = pl.program_id(1)
    scale = NSA_HD ** -0.5
    g_n = NSA_GROUP
    rows = SLAB
    q = q_ref[...]
    q4 = jnp.concatenate([q[:, g * NSA_HD:(g + 1) * NSA_HD] for g in range(g_n)], axis=0).astype(BF16)
    tpos = lax.broadcasted_iota(jnp.int32, (rows, 1), 0)
    qpos = q0 + tpos
    qpos4 = jnp.concatenate([qpos] * g_n, axis=0)

    def compressed(which):
        uv = uv_ref[0, which, 0]
        ng = uv.shape[0]
        u = uv[:, :NSA_HD]
        v_next = pltpu.roll(uv[:, NSA_HD:], ng - 1, 0)
        const = _dot(pef_ref[which], w1_ref[which])[0:1]
        return _dot(_gelu(u + v_next + const).astype(BF16), w2_ref[which])

    kc = compressed(0)
    vc = compressed(1)
    ncp = kc.shape[0]
    s_c = _dot_nt(q4, kc.astype(BF16)) * scale
    cidx = lax.broadcasted_iota(jnp.int32, (1, ncp), 1)
    mask_c = (cidx * CMP_STRIDE + CMP_LEN - 1 <= qpos4) & (cidx < n_c)
    p_c = _softmax_lanes(s_c, mask_c)
    o_c = _dot(p_c.astype(BF16), vc.astype(BF16))

    p_sum = p_c[0:rows]
    for g in range(1, g_n):
        p_sum = p_sum + p_c[g * rows:(g + 1) * rows]
    imp = _dot_split3(p_sum, ov_ref[...])
    nsp = imp.shape[1]
    blk = lax.broadcasted_iota(jnp.int32, (rows, nsp), 1)
    cur = jnp.right_shift(qpos, SEL_SHIFT)
    forced = (blk == 0) | (blk == cur) | (blk == cur - 1)
    valid = (blk * SEL_LEN <= qpos) & (blk < n_s)
    score = jnp.where(valid, jnp.where(forced, jnp.inf, imp), -jnp.inf)
    score_pad = jnp.concatenate([score, jnp.zeros((LANE - rows, nsp), F32)], axis=0)
    score_t = jnp.transpose(score_pad)
    bi = lax.broadcasted_iota(jnp.int32, (nsp, nsp), 0)
    bj = lax.broadcasted_iota(jnp.int32, (nsp, nsp), 1)
    lane = lax.broadcasted_iota(jnp.int32, (nsp, LANE), 1)
    sel_t = jnp.zeros((nsp, LANE), F32)
    k_take = float(min(SEL_TOPK, n_s))
    for tt in range(valid_rows):
        s_i = score_t[:, tt:tt + 1]
        s_j = score[tt:tt + 1, :]
        ahead = (s_j > s_i) | ((s_j == s_i) & (bj < bi))
        rank_i = jnp.sum(jnp.where(ahead, 1.0, 0.0), axis=-1, keepdims=True)
        ok_i = (rank_i < k_take) & (s_i > -jnp.inf)
        sel_t = jnp.where(((lane & (rows - 1)) == tt) & (lane < g_n * rows) & ok_i, 1.0, sel_t)
    selt_ref[0, 0] = sel_t

    kw = jnp.concatenate([kwp_ref[...], kwn_ref[...]], axis=0).astype(BF16)
    vw = jnp.concatenate([vwp_ref[...], vwn_ref[...]], axis=0).astype(BF16)
    n_before = kwp_ref.shape[0]
    widx = lax.broadcasted_iota(jnp.int32, (1, n_before + rows), 1)
    kpos = jnp.where(widx < n_before, q0 - n_before + widx, q0 + widx - n_before)
    dist = qpos4 - kpos
    mask_w = (dist >= 0) & (dist <= WINDOW) & (kpos >= 0)
    s_w = _dot_nt(q4, kw) * scale
    p_w = _softmax_lanes(s_w, mask_w)
    o_w = _dot(p_w.astype(BF16), vw)

    gates = _sigmoid(ng_ref[...])
    glane = lax.broadcasted_iota(jnp.int32, gates.shape, 1)
    for g in range(g_n):
        r = slice(g * rows, (g + 1) * rows)
        col = (kv * g_n + g) * 3
        pick = lambda j: jnp.sum(jnp.where(glane == col + j, gates, 0.0), axis=-1, keepdims=True)
        op_ref[0, 0, r, :] = pick(0) * o_c[r] + pick(2) * o_w[r]


def _s_attn_kernel(pt_ref, *refs, n_steps, valid_rows):
    del pt_ref
    pages = refs[:PAGES_PER_STEP]
    q_ref, kvn_ref, ng_ref, selt_ref, op_ref, o_ref, m_sc, l_sc, acc_sc = refs[PAGES_PER_STEP:]
    s_id = pl.program_id(1)
    scale = NSA_HD ** -0.5
    g_n = NSA_GROUP
    rows = SLAB
    blocks_per_step = PAGES_PER_STEP * PAGE_SIZE // SEL_LEN

    @pl.when(s_id == 0)
    def _():
        m_sc[...] = jnp.full(m_sc.shape, NEG, F32)
        l_sc[...] = jnp.zeros(l_sc.shape, F32)
        acc_sc[...] = jnp.zeros(acc_sc.shape, F32)

    q = q_ref[...]

    def q_rows(kv):
        q4 = jnp.concatenate([q[:, (kv * g_n + g) * NSA_HD:(kv * g_n + g + 1) * NSA_HD] for g in range(g_n)], axis=0)
        return jnp.concatenate([q4, jnp.zeros((LANE - g_n * rows, NSA_HD), F32)], axis=0).astype(BF16)

    def update(kv, s_t, mask_t, v):
        m_old = m_sc[kv]
        m_new = jnp.maximum(m_old, jnp.max(jnp.where(mask_t, s_t, NEG), axis=0, keepdims=True))
        alpha = jnp.exp(m_old - m_new)
        p_t = jnp.where(mask_t, jnp.exp(jnp.minimum(s_t - m_new, 0.0)), 0.0)
        l_sc[kv] = alpha * l_sc[kv] + jnp.sum(p_t, axis=0, keepdims=True)
        acc_sc[kv] = alpha * acc_sc[kv] + _dot_tn(v, p_t.astype(BF16))
        m_sc[kv] = m_new

    stride = 4 * NSA_KV
    for kv in range(NSA_KV):
        qk = q_rows(kv)
        k = jnp.concatenate([pg[pl.ds(2 * NSA_KV + kv, PAGE_SIZE, stride=stride), :] for pg in pages], axis=0)
        v = jnp.concatenate([pg[pl.ds(3 * NSA_KV + kv, PAGE_SIZE, stride=stride), :] for pg in pages], axis=0)
        s_t = _dot_nt(k.astype(BF16), qk) * scale
        start = pl.multiple_of(s_id * blocks_per_step, blocks_per_step)
        chunk = selt_ref[0, kv, pl.ds(start, blocks_per_step), :]
        mask_t = jnp.concatenate([jnp.broadcast_to(chunk[c:c + 1, :], (SEL_LEN, LANE))
                                  for c in range(blocks_per_step)], axis=0) > 0.5
        update(kv, s_t, mask_t, v.astype(BF16))

    @pl.when(s_id == n_steps - 1)
    def _():
        n_past_blocks = n_steps * blocks_per_step
        kvn = jnp.concatenate([kvn_ref[...], jnp.zeros((LANE - rows, 4 * NSA_HD), F32)], axis=0)
        gates = _sigmoid(ng_ref[...])
        glane = lax.broadcasted_iota(jnp.int32, gates.shape, 1)
        for kv in range(NSA_KV):
            k_new = kvn[:, kv * NSA_HD:(kv + 1) * NSA_HD].astype(BF16)
            v_new = kvn[:, (NSA_KV + kv) * NSA_HD:(NSA_KV + kv + 1) * NSA_HD].astype(BF16)
            s_t = _dot_nt(k_new, q_rows(kv)) * scale
            key = lax.broadcasted_iota(jnp.int32, (LANE, LANE), 0)
            tok = lax.broadcasted_iota(jnp.int32, (LANE, LANE), 1) & (rows - 1)
            blk_ok = selt_ref[0, kv, n_past_blocks:n_past_blocks + 1, :] > 0.5
            mask_t = (key <= tok) & (key < valid_rows) & blk_ok
            update(kv, s_t, mask_t, v_new)
            l = l_sc[kv]
            o_t = acc_sc[kv] / jnp.where(l > 0.0, l, 1.0)
            o_sel = jnp.transpose(o_t)
            for g in range(g_n):
                r = slice(g * rows, (g + 1) * rows)
                col = (kv * g_n + g) * 3 + 1
                gate = jnp.sum(jnp.where(glane == col, gates, 0.0), axis=-1, keepdims=True)
                o = op_ref[0, kv, r, :] + gate * o_sel[r]
                o_ref[:, (kv * g_n + g) * NSA_HD:(kv * g_n + g + 1) * NSA_HD] = o.astype(o_ref.dtype)


def _nsa_sample(zs, cache4, page_table, win_past, pe, w1, w2, *, layer, nb, valid_rows, q0):
    n_pages = page_table.shape[1]
    assert n_pages % PAGES_PER_STEP == 0 and q0 == n_pages * PAGE_SIZE
    s_len = q0 + valid_rows
    n_c = (s_len - CMP_LEN) // CMP_STRIDE + 1
    n_s = -(-s_len // SEL_LEN)
    ng = n_pages * GROUPS_PER_PAGE
    assert n_c <= ng - 1
    nsp = _round_up(n_s, LANE)
    steps = n_pages // PAGES_PER_STEP
    w1b = w1.astype(BF16)
    w1r = w1b.reshape(2, CMP_LEN, NSA_HD, NSA_HD)
    w1ab = jnp.concatenate([w1r[:, :CMP_STRIDE], w1r[:, CMP_STRIDE:]], axis=-1)
    uv = _compress_sample_partial(cache4, page_table, w1ab, layer=layer, nb=nb, n_pages=n_pages)
    pef = jnp.broadcast_to(pe.reshape(2, 1, CMP_LEN * NSA_HD), (2, SUBLANE, CMP_LEN * NSA_HD)).astype(BF16)
    ov = _overlap_t(ng, nsp, n_c, n_s)
    kvw = NSA_GROUP * NSA_HD
    nk = C_NKV // NSA_HD
    n_before = win_past.shape[4]
    selt, o_part = pl.pallas_call(
        functools.partial(_s_sel_kernel, q0=q0, valid_rows=valid_rows, n_c=n_c, n_s=n_s),
        grid=(nb, NSA_KV),
        in_specs=[pl.BlockSpec((1, 2, 1, ng, 2 * NSA_HD), lambda b, kv: (b, 0, kv, 0, 0)),
                  pl.BlockSpec((2, SUBLANE, CMP_LEN * NSA_HD), lambda b, kv: (0, 0, 0)),
                  pl.BlockSpec((2, CMP_LEN * NSA_HD, NSA_HD), lambda b, kv: (0, 0, 0)),
                  pl.BlockSpec((2, NSA_HD, NSA_HD), lambda b, kv: (0, 0, 0)),
                  pl.BlockSpec((SLAB, kvw), lambda b, kv: (b, C_NQ // kvw + kv)),
                  pl.BlockSpec((SLAB, LANE), lambda b, kv: (b, C_NG // LANE)),
                  pl.BlockSpec((SLAB, NSA_HD), lambda b, kv: (b, nk + 8 + kv)),
                  pl.BlockSpec((SLAB, NSA_HD), lambda b, kv: (b, nk + 10 + kv)),
                  pl.BlockSpec((None, None, None, None, n_before, NSA_HD), lambda b, kv: (layer, b, 0, kv, 0, 0)),
                  pl.BlockSpec((None, None, None, None, n_before, NSA_HD), lambda b, kv: (layer, b, 1, kv, 0, 0)),
                  pl.BlockSpec((ng, nsp), lambda b, kv: (0, 0))],
        out_specs=[pl.BlockSpec((1, 1, nsp, LANE), lambda b, kv: (b, kv, 0, 0)),
                   pl.BlockSpec((1, 1, NSA_GROUP * SLAB, NSA_HD), lambda b, kv: (b, kv, 0, 0))],
        out_shape=[jax.ShapeDtypeStruct((nb, NSA_KV, nsp, LANE), F32),
                   jax.ShapeDtypeStruct((nb, NSA_KV, NSA_GROUP * SLAB, NSA_HD), F32)],
        compiler_params=_params(("parallel", "parallel"), VMEM_BIG),
        name="nsa_select_sample",
    )(uv, pef, w1b, w2.astype(BF16), zs, zs, zs, zs, win_past, win_past, ov)
    return pl.pallas_call(
        functools.partial(_s_attn_kernel, n_steps=steps, valid_rows=valid_rows),
        grid_spec=pltpu.PrefetchScalarGridSpec(
            num_scalar_prefetch=1,
            grid=(nb, steps),
            in_specs=_page_specs(layer, PAGES_PER_STEP)
            + [pl.BlockSpec((SLAB, NSA_HEADS * NSA_HD), lambda b, s, pt: (b, C_NQ // (NSA_HEADS * NSA_HD))),
               pl.BlockSpec((SLAB, 4 * NSA_HD), lambda b, s, pt: (b, (C_NKV + 4 * NSA_HD) // (4 * NSA_HD))),
               pl.BlockSpec((SLAB, LANE), lambda b, s, pt: (b, C_NG // LANE)),
               pl.BlockSpec((1, NSA_KV, nsp, LANE), lambda b, s, pt: (b, 0, 0, 0)),
               pl.BlockSpec((1, NSA_KV, NSA_GROUP * SLAB, NSA_HD), lambda b, s, pt: (b, 0, 0, 0))],
            out_specs=pl.BlockSpec((SLAB, W_MIX), lambda b, s, pt: (b, 0)),
            scratch_shapes=[pltpu.VMEM((NSA_KV, 1, LANE), F32), pltpu.VMEM((NSA_KV, 1, LANE), F32),
                            pltpu.VMEM((NSA_KV, NSA_HD, LANE), F32)],
        ),
        out_shape=jax.ShapeDtypeStruct((nb * SLAB, W_MIX), _act_dtype(SLAB)),
        compiler_params=_params(("parallel", "arbitrary"), VMEM_BIG),
        name="nsa_attend_sample",
    )(page_table, *([cache4] * PAGES_PER_STEP), zs, zs, zs, selt, o_part)


def _merge_kernel(a_ref, b_ref, c_ref, d_ref, w_ref, g0_ref, g1_ref, g2_ref, g3_ref, o_ref):
    acc = None
    for n, (x_ref, g_ref) in enumerate(((a_ref, g0_ref), (b_ref, g1_ref), (c_ref, g2_ref), (d_ref, g3_ref))):
        term = _sigmoid(g_ref[...]) * _dot(x_ref[...].astype(BF16), w_ref[n])
        acc = term if acc is None else acc + term
    o_ref[...] = acc.astype(o_ref.dtype)


def _merge(branches, w_branch, z):
    m = z.shape[0]
    tm = _tile(m, 512)
    tn = 1024
    br = pl.BlockSpec((tm, W_MIX), lambda i, j: (i, 0))
    gz = lambda n: pl.BlockSpec((tm, tn), lambda i, j: (i, (C_GZ + n * D_MODEL) // tn + j))
    return pl.pallas_call(
        _merge_kernel,
        grid=(m // tm, D_MODEL // tn),
        in_specs=[br, br, br, br, pl.BlockSpec((N_BRANCH, W_MIX, tn), lambda i, j: (0, 0, j)),
                  gz(0), gz(1), gz(2), gz(3)],
        out_specs=pl.BlockSpec((tm, tn), lambda i, j: (i, j)),
        out_shape=jax.ShapeDtypeStruct((m, D_MODEL), BF16),
        compiler_params=_params(("parallel", "parallel"), VMEM_BIG),
        name="branch_merge",
    )(*branches, w_branch, z, z, z, z)


def _reorder_w_in(w):
    parts = [w[:, :_O_NKV], w[:, _O_CA:_O_GZ], w[:, _O_GZ:_O_END], w[:, _O_NKV:_O_NG], w[:, _O_NG:_O_CA],
             jnp.zeros((w.shape[0], N_PAD - C_NG - (_O_CA - _O_NG)), w.dtype)]
    return jnp.concatenate(parts, axis=1).astype(BF16)


def _trunk(x, weights, *, nb, rows, valid_rows, q0, ret_state, conv_state, win_state=None, cache4=None,
           page_table=None):
    (norm1, w_in, ret_gn_w, ret_gn_b, gm_ln_w, gm_ln_b, gm_ws, gm_bs, nsa_pe, nsa_w1, nsa_w2,
     conv_w, conv_b, conv_ln_w, conv_ln_b, w_branch, w_out, norm2, w_up, w_down, final_norm) = weights
    prompt = cache4 is None
    c_eff = math.gcd(valid_rows, RET_CHUNK)
    if prompt:
        ret_rows, n_chunks = RET_CHUNK, rows // RET_CHUNK
        conv_rows = 128
    else:
        ret_rows, n_chunks = rows, 1
        conv_rows = rows
    assert c_eff == min(valid_rows, RET_CHUNK) and rows % ret_rows == 0
    pos = q0 + np.arange(rows)
    cos, sin = _rope_tables(pos[:valid_rows] if not prompt else pos, rows)
    outs = dict(ret=[], z=[], conv=[], gm=[])
    for l in range(DEPTH):
        h = _rmsnorm(x, norm1[l], slab=rows if not prompt else 1, valid=valid_rows if not prompt else 1, out_dtype=BF16)
        z = _matmul(h, w_in[l], tn=1024, name="in_proj")
        o_ret, r_new = _retention(z, ret_state[l], ret_gn_w[l], ret_gn_b[l], nb=nb, rows=ret_rows,
                                  n_chunks=n_chunks, c_eff=c_eff, cos=cos, sin=sin)
        o_gm, gv = _gmlp(z, gm_ln_w[l], gm_ln_b[l], gm_ws[l], gm_bs[l], rows=GM_CHUNK if prompt else rows)
        if prompt:
            cmp_kv = _compress_prompt(z, nsa_pe[l], nsa_w1[l].astype(BF16), nsa_w2[l].astype(BF16), nb=nb, t=rows)
            o_nsa = _nsa_prompt(z, cmp_kv, nb=nb, t=rows)
        else:
            o_nsa = _nsa_sample(z, cache4, page_table, win_state, nsa_pe[l], nsa_w1[l], nsa_w2[l], layer=l, nb=nb,
                                valid_rows=valid_rows, q0=q0)
        o_conv, ext = _conv_module(z, conv_state[l], conv_w[l], conv_b[l], conv_ln_w[l], conv_ln_b[l], nb=nb,
                                   rows=conv_rows, n_tiles=rows // conv_rows)
        merged = _merge((o_ret, o_gm, o_nsa, o_conv), w_branch[l], z)
        x = _matmul(merged, w_out[l], tn=512, res=x, name="out_proj")
        h2 = _rmsnorm(x, norm2[l], slab=1, valid=1, out_dtype=BF16)
        u = _matmul(h2, w_up[l], tn=1024, act="relu2", out_dtype=BF16, name="mlp_up")
        x = _matmul_kres(u, w_down[l], x, tn=1024, tk=2048, name="mlp_down")
        outs["ret"].append(r_new)
        outs["z"].append(z)
        tail = HALO + (valid_rows if not prompt else conv_rows)
        outs["conv"].append(ext[:, tail - (CONV_W - 1):tail])
        outs["gm"].append(gv)
    y = _rmsnorm(x, final_norm, slab=1, valid=1, out_dtype=F32)
    return y, outs


def kernel(x_prompt, x_sample, cache_nsa_kv, state_ret, state_win_kv, state_conv, page_table, norm1, w_in, ret_gn_w,
           ret_gn_b, gm_ln_w, gm_ln_b, gm_ws, gm_bs, nsa_pe, nsa_w1, nsa_w2, conv_w, conv_b, conv_ln_w, conv_ln_b,
           w_branch, w_out, norm2, w_up, w_down, final_norm):
    bp, t, d = x_prompt.shape
    bs, ts, _ = x_sample.shape
    assert d == D_MODEL and t % RET_CHUNK == 0 and ts <= SLAB
    depth = norm1.shape[0]
    assert depth == DEPTH
    n_pages = page_table.shape[1]
    past_len = n_pages * PAGE_SIZE
    weights = (norm1, [_reorder_w_in(w_in[l]) for l in range(DEPTH)], ret_gn_w, ret_gn_b, gm_ln_w, gm_ln_b, gm_ws,
               gm_bs, nsa_pe, nsa_w1, nsa_w2, conv_w, conv_b, conv_ln_w, conv_ln_b, w_branch.astype(BF16),
               w_out.astype(BF16), norm2, w_up.astype(BF16), w_down.astype(BF16), final_norm)

    y_p, o_p = _trunk(x_prompt.reshape(bp * t, d), weights, nb=bp, rows=t, valid_rows=t, q0=0,
                      ret_state=jnp.zeros((DEPTH, bp, RET_HEADS, RET_DK, RET_DV), F32),
                      conv_state=jnp.zeros((DEPTH, bp, HALO, W_MIX), F32))
    keep_p = min(WINDOW, t)
    kv_p, win_p = [], []
    for z in o_p["z"]:
        z3 = z.reshape(bp, t, N_PAD)
        kv_p.append(z3[:, :, C_NKV:C_NKV + 8 * NSA_HD].reshape(bp, t, 4, NSA_KV, NSA_HD))
        win_p.append(z3[:, t - keep_p:, C_NKV + 8 * NSA_HD:C_NKV + 12 * NSA_HD].reshape(bp, keep_p, 2, NSA_KV, NSA_HD))

    xs = jnp.pad(x_sample, ((0, 0), (0, SLAB - ts), (0, 0))).reshape(bs * SLAB, d)
    cache4 = cache_nsa_kv.reshape(DEPTH, cache_nsa_kv.shape[1], PAGE_ROWS, NSA_HD)
    win_t = jnp.transpose(state_win_kv, (0, 1, 3, 4, 2, 5))
    conv_pad = jnp.pad(state_conv, ((0, 0), (0, 0), (HALO - (CONV_W - 1), 0), (0, 0)))
    y_s, o_s = _trunk(xs, weights, nb=bs, rows=SLAB, valid_rows=ts, q0=past_len, ret_state=state_ret,
                      conv_state=conv_pad, win_state=win_t, cache4=cache4, page_table=page_table)
    keep_s = min(WINDOW, past_len + ts)
    kv_s, win_s, gm_s = [], [], []
    for l, z in enumerate(o_s["z"]):
        z3 = z.reshape(bs, SLAB, N_PAD)[:, :ts]
        kv_s.append(z3[:, :, C_NKV:C_NKV + 8 * NSA_HD].reshape(bs, ts, 4, NSA_KV, NSA_HD))
        new_win = z3[:, :, C_NKV + 8 * NSA_HD:C_NKV + 12 * NSA_HD].reshape(bs, ts, 2, NSA_KV, NSA_HD)
        win_s.append(jnp.concatenate([state_win_kv[l], new_win], axis=1)[:, -keep_s:])
        gm_s.append(o_s["gm"][l].reshape(bs, SLAB, W_MIX)[:, :ts])
    return (y_p.reshape(bp, t, d), y_s.reshape(bs, SLAB, d)[:, :ts],
            jnp.stack(o_p["ret"]), jnp.stack(o_s["ret"]),
            jnp.stack(kv_p), jnp.stack(kv_s), jnp.stack(win_p), jnp.stack(win_s),
            jnp.stack(o_p["conv"]), jnp.stack(o_s["conv"]), jnp.stack(gm_s))
```

```python
import functools
import math

import numpy as np
import jax
import jax.numpy as jnp
from jax import lax
from jax.experimental import pallas as pl
from jax.experimental.pallas import tpu as pltpu

F32 = jnp.float32
BF16 = jnp.bfloat16

D_MODEL = 4096
DEPTH = 2
PAGE_SIZE = 128
W_MIX = D_MODEL // 4
N_BRANCH = 4
RET_HEADS = 4
RET_DK = W_MIX // RET_HEADS
RET_DV = W_MIX // RET_HEADS
RET_CHUNK = 128
ROPE_BASE = 10000.0
GM_CHUNK = 128
GM_GROUPS = 4
NSA_HEADS = 8
NSA_KV = 2
NSA_HD = W_MIX // NSA_HEADS
NSA_GROUP = NSA_HEADS // NSA_KV
CMP_LEN = 32
CMP_STRIDE = 16
SEL_LEN = 64
SEL_SHIFT = 6
SEL_TOPK = 16
WINDOW = 512
CONV_W = 31
D_FF = 4 * D_MODEL
EPS = 1e-6
LN_EPS = 1e-5

LANE = 128
SUBLANE = 8
SLAB = 8
VMEM_BIG = 56 * 1024 * 1024

A_RQ, A_RK, A_RV, A_RG = 0, 1024, 2048, 3072
A_GU, A_GV = 4096, 5120
A_NQ = 6144
A_NKV = 7168
N_A = 8704
B_CA, B_CB, B_GZ = 0, 1024, 2048
N_B = 18432
N_C = 128
_O_NG, _O_CA, _O_END = 8704, 8728, 27160

NEG = -1e30


def _round_up(a, b):
    return -(-a // b) * b


def _tile(m, pref):
    best = None
    for t in range(SUBLANE, min(m, pref) + 1, SUBLANE):
        if m % t == 0:
            best = t
    assert best is not None, (m, pref)
    return best


def _act_dtype(rows):
    return BF16 if rows % (2 * SUBLANE) == 0 else F32


def _params(sem, vmem=None):
    return pltpu.CompilerParams(dimension_semantics=sem, vmem_limit_bytes=vmem)


def _gelu(x):
    return 0.5 * x * (1.0 + jnp.tanh(0.7978845608028654 * (x + 0.044715 * (x * x * x))))


def _sigmoid(x):
    return 1.0 / (1.0 + jnp.exp(-x))


def _layernorm(x, w, b):
    mu = jnp.mean(x, axis=-1, keepdims=True)
    xc = x - mu
    var = jnp.mean(xc * xc, axis=-1, keepdims=True)
    return xc * lax.rsqrt(var + LN_EPS) * w + b


def _softmax_lanes(s, mask):
    sm = jnp.where(mask, s, NEG)
    m = jnp.max(sm, axis=-1, keepdims=True)
    m = jnp.where(m > 0.5 * NEG, m, 0.0)
    e = jnp.where(mask, jnp.exp(sm - m), 0.0)
    den = jnp.sum(e, axis=-1, keepdims=True)
    return e / jnp.where(den > 0.0, den, 1.0)


def _softmax_parts(s, mask):
    sm = jnp.where(mask, s, NEG)
    m = jnp.max(sm, axis=-1, keepdims=True)
    m = jnp.where(m > 0.5 * NEG, m, 0.0)
    e = jnp.exp(sm - m)
    den = jnp.sum(e, axis=-1, keepdims=True)
    return e, 1.0 / jnp.where(den > 0.0, den, 1.0)


def _dot(a, b):
    return jnp.dot(a, b, preferred_element_type=F32)


def _dot_nt(a, b):
    return lax.dot_general(a, b, (((1,), (1,)), ((), ())), preferred_element_type=F32)


def _dot_tn(a, b):
    return lax.dot_general(a, b, (((0,), (0,)), ((), ())), preferred_element_type=F32)


def _dot_split3(a, b_bf16):
    a1 = a.astype(BF16)
    r1 = a - a1.astype(F32)
    a2 = r1.astype(BF16)
    a3 = (r1 - a2.astype(F32)).astype(BF16)
    return _dot(a1, b_bf16) + _dot(a2, b_bf16) + _dot(a3, b_bf16)


def _rmsnorm_kernel(x_ref, w_ref, o_ref, *, slab, valid):
    x = x_ref[...]
    y = x * lax.rsqrt(jnp.mean(x * x, axis=-1, keepdims=True) + EPS) * w_ref[...]
    if valid < slab:
        row = lax.broadcasted_iota(jnp.int32, y.shape, 0)
        y = jnp.where((row & (slab - 1)) < valid, y, 0.0)
    o_ref[...] = y.astype(o_ref.dtype)


def _rmsnorm(x, w, *, slab, valid, out_dtype):
    m, d = x.shape
    tr = _tile(m, 256)
    assert tr % slab == 0 or valid == slab
    return pl.pallas_call(
        functools.partial(_rmsnorm_kernel, slab=slab, valid=valid),
        grid=(m // tr,),
        in_specs=[pl.BlockSpec((tr, d), lambda i: (i, 0)), pl.BlockSpec((1, d), lambda i: (0, 0))],
        out_specs=pl.BlockSpec((tr, d), lambda i: (i, 0)),
        out_shape=jax.ShapeDtypeStruct((m, d), out_dtype),
        compiler_params=_params(("parallel",)),
        name="rmsnorm",
    )(x, w.reshape(1, d))


def _dense_kernel(*refs, act, has_res, kgrid):
    if has_res:
        xp_ref, xs_ref, w_ref, rp_ref, rs_ref, op_ref, os_ref = refs
    else:
        xp_ref, xs_ref, w_ref, op_ref, os_ref = refs
        rp_ref = rs_ref = None
    first_tile = pl.program_id(0) == 0
    wb = w_ref[...].astype(BF16)

    def finish(acc, r_ref):
        if act == "relu2":
            acc = jnp.square(jnp.maximum(acc, 0.0))
        if r_ref is not None:
            acc = r_ref[...] + acc
        return acc

    if kgrid:
        @pl.when(pl.program_id(2) == 0)
        def _():
            op_ref[...] = rp_ref[...]
            os_ref[...] = rs_ref[...]

        op_ref[...] += _dot(xp_ref[...], wb)

        @pl.when(first_tile)
        def _():
            os_ref[...] += _dot(xs_ref[...], wb)
    else:
        op_ref[...] = finish(_dot(xp_ref[...], wb), rp_ref).astype(op_ref.dtype)

        @pl.when(first_tile)
        def _():
            os_ref[...] = finish(_dot(xs_ref[...], wb), rs_ref).astype(os_ref.dtype)

        @pl.when(jnp.logical_not(first_tile))
        def _():
            os_ref[...] = jnp.zeros(os_ref.shape, os_ref.dtype)


def _dense(xp, xs, w, *, n, tn, tm=2048, layer=None, tk=None, act=None, out_dtype=F32, resp=None, ress=None,
           name="dense"):
    mp, k = xp.shape
    ms = xs.shape[0]
    tm = _tile(mp, tm)
    kgrid = tk is not None
    has_res = resp is not None
    assert n % tn == 0 and (not kgrid or (has_res and act is None and k % tk == 0))
    if kgrid:
        grid = (mp // tm, n // tn, k // tk)
        row = lambda i, j, kk: (i, kk)
        srow = lambda i, j, kk: (0, kk)
        wix = (lambda i, j, kk: (layer, kk, j)) if w.ndim == 3 else (lambda i, j, kk: (kk, j))
        out = lambda i, j, kk: (i, j)
        sres = lambda i, j, kk: (0, j)
        sout = lambda i, j, kk: (i, j)
        kb = tk
        sem = ("arbitrary", "arbitrary", "arbitrary")
    else:
        grid = (mp // tm, n // tn)
        row = lambda i, j: (i, 0)
        srow = lambda i, j: (0, 0)
        wix = (lambda i, j: (layer, 0, j)) if w.ndim == 3 else (lambda i, j: (0, j))
        out = lambda i, j: (i, j)
        sres = lambda i, j: (0, j)
        sout = lambda i, j: (i, j)
        kb = k
        sem = ("arbitrary", "arbitrary")
    wblock = (None, kb, tn) if w.ndim == 3 else (kb, tn)
    x_mode = {} if kgrid else dict(pipeline_mode=pl.Buffered(1))
    in_specs = [pl.BlockSpec((tm, kb), row, **x_mode), pl.BlockSpec((ms, kb), srow, **x_mode),
                pl.BlockSpec(wblock, wix)]
    args = [xp, xs, w]
    if has_res:
        r_mode = dict(pipeline_mode=pl.Buffered(1)) if kgrid else {}
        in_specs += [pl.BlockSpec((tm, tn), out, **r_mode), pl.BlockSpec((ms, tn), sres, **r_mode)]
        args += [resp, ress]
    op, os2 = pl.pallas_call(
        functools.partial(_dense_kernel, act=act, has_res=has_res, kgrid=kgrid),
        grid=grid,
        in_specs=in_specs,
        out_specs=[pl.BlockSpec((tm, tn), out), pl.BlockSpec((ms, tn), sout)],
        out_shape=[jax.ShapeDtypeStruct((mp, n), out_dtype),
                   jax.ShapeDtypeStruct((mp // tm * ms, n), out_dtype)],
        compiler_params=_params(sem, VMEM_BIG),
        name=name,
    )(*args)
    return op, os2[:ms]


def _ret_kernel(q_ref, k_ref, v_ref, g_ref, cos_ref, sin_ref, dm_ref, qd_ref, kd_ref, cd_ref, gw_ref, gb_ref,
                r0_ref, o_ref, ro_ref, r_sc, *, rows):
    ci = pl.program_id(2)

    @pl.when(ci == 0)
    def _():
        r_sc[...] = r0_ref[0, 0]

    cos = cos_ref[...]
    sin = sin_ref[...]
    half = RET_DK // 2

    def rot(x):
        x1, x2 = x[:, :half], x[:, half:]
        return jnp.concatenate([x1 * cos - x2 * sin, x2 * cos + x1 * sin], axis=-1)

    def pad(x):
        if rows == RET_CHUNK:
            return x
        return jnp.concatenate([x, jnp.zeros((RET_CHUNK - rows, x.shape[1]), x.dtype)], axis=0)

    q = pad(rot(q_ref[...]))
    k = pad(rot(k_ref[...]) * (RET_DK ** -0.5))
    v = pad(v_ref[...])
    vb = v.astype(BF16)
    r = r_sc[...]
    s = _dot_nt(q.astype(BF16), k.astype(BF16)) * dm_ref[0]
    o = _dot(s.astype(BF16), vb) + _dot((q * qd_ref[0]).astype(BF16), r.astype(BF16))
    r_new = cd_ref[0] * r + _dot_tn((k * kd_ref[0]).astype(BF16), vb)
    r_sc[...] = r_new
    ro_ref[0, 0] = r_new
    o = o[:rows]
    mu = jnp.mean(o, axis=-1, keepdims=True)
    oc = o - mu
    var = jnp.mean(oc * oc, axis=-1, keepdims=True)
    y = oc * lax.rsqrt(var + LN_EPS) * gw_ref[...] + gb_ref[...]
    g = g_ref[...]
    o_ref[...] = (g * _sigmoid(g) * y).astype(o_ref.dtype)


def _ret_tables(c_eff):
    log_g = np.log1p(-np.exp2(-5.0 - np.arange(RET_HEADS, dtype=np.float64)))
    i = np.arange(RET_CHUNK, dtype=np.float64)
    live = i < c_eff
    diff = i[:, None] - i[None, :]
    dmask = np.where(diff >= 0, np.exp(np.maximum(diff, 0.0)[None] * log_g[:, None, None]), 0.0)
    dmask = dmask * (live[:, None] & live[None, :])[None]
    q_dec = np.exp((i[None, :] + 1.0) * log_g[:, None]) * live[None]
    k_dec = np.exp((c_eff - 1.0 - i)[None, :] * log_g[:, None]) * live[None]
    c_dec = np.exp(c_eff * log_g)
    bc = lambda a: np.broadcast_to(a[:, :, None], (RET_HEADS, RET_CHUNK, RET_DK))
    return (jnp.asarray(dmask, F32), jnp.asarray(bc(q_dec), F32), jnp.asarray(bc(k_dec), F32),
            jnp.asarray(np.broadcast_to(c_dec[:, None, None], (RET_HEADS, 1, RET_DK)), F32))


def _rope_tables(positions, rows):
    half = RET_DK // 2
    inv = ROPE_BASE ** (-np.arange(half, dtype=np.float64) / half)
    ang = np.asarray(positions, np.float64)[:, None] * inv[None, :]
    cos = np.zeros((rows, half)); sin = np.zeros((rows, half))
    cos[:len(positions)] = np.cos(ang); sin[:len(positions)] = np.sin(ang)
    return jnp.asarray(cos, F32), jnp.asarray(sin, F32)


def _retention(z, r0, gn_w, gn_b, *, nb, rows, n_chunks, c_eff, cos, sin):
    m = z.shape[0]
    dmask, q_dec, k_dec, c_dec = _ret_tables(c_eff)
    zspec = lambda off: pl.BlockSpec((rows, RET_DK), lambda b, h, c: (b * n_chunks + c, off // RET_DK + h))
    tab = pl.BlockSpec((rows, RET_DK // 2), lambda b, h, c: (c, 0))
    per_h = lambda r: pl.BlockSpec((1, r, RET_DK), lambda b, h, c: (h, 0, 0))
    return pl.pallas_call(
        functools.partial(_ret_kernel, rows=rows),
        grid=(nb, RET_HEADS, n_chunks),
        in_specs=[zspec(A_RQ), zspec(A_RK), zspec(A_RV), zspec(A_RG), tab, tab,
                  pl.BlockSpec((1, RET_CHUNK, RET_CHUNK), lambda b, h, c: (h, 0, 0)),
                  per_h(RET_CHUNK), per_h(RET_CHUNK), per_h(1),
                  pl.BlockSpec((1, RET_DV), lambda b, h, c: (0, h)),
                  pl.BlockSpec((1, RET_DV), lambda b, h, c: (0, h)),
                  pl.BlockSpec((1, 1, RET_DK, RET_DV), lambda b, h, c: (b, h, 0, 0))],
        out_specs=[pl.BlockSpec((rows, RET_DV), lambda b, h, c: (b * n_chunks + c, h)),
                   pl.BlockSpec((1, 1, RET_DK, RET_DV), lambda b, h, c: (b, h, 0, 0))],
        out_shape=[jax.ShapeDtypeStruct((m, W_MIX), _act_dtype(rows)),
                   jax.ShapeDtypeStruct((nb, RET_HEADS, RET_DK, RET_DV), F32)],
        scratch_shapes=[pltpu.VMEM((RET_DK, RET_DV), F32)],
        compiler_params=_params(("parallel", "parallel", "arbitrary")),
        name="retention",
    )(z, z, z, z, cos, sin, dmask, q_dec, k_dec, c_dec, gn_w.reshape(1, W_MIX), gn_b.reshape(1, W_MIX), r0)


def _gm_kernel(u_ref, v_ref, lw_ref, lb_ref, ws_ref, bst_ref, o_ref, *maybe_gv_ref, rows):
    u = _gelu(u_ref[...])
    v = _layernorm(_gelu(v_ref[...]), lw_ref[...], lb_ref[...])
    for gv_ref in maybe_gv_ref:
        gv_ref[...] = v
    if rows < GM_CHUNK:
        v = jnp.concatenate([v, jnp.zeros((GM_CHUNK - rows, v.shape[1]), F32)], axis=0)
    ri = lax.broadcasted_iota(jnp.int32, (GM_CHUNK, GM_CHUNK), 0)
    cj = lax.broadcasted_iota(jnp.int32, (GM_CHUNK, GM_CHUNK), 1)
    gw = W_MIX // GM_GROUPS
    for g in range(GM_GROUPS):
        wm = jnp.where(cj <= ri, ws_ref[g], 0.0).astype(BF16)
        s = _dot(wm, v[:, g * gw:(g + 1) * gw].astype(BF16)) + bst_ref[:, g:g + 1]
        o_ref[:, g * gw:(g + 1) * gw] = (u[:, g * gw:(g + 1) * gw] * s[:rows]).astype(o_ref.dtype)


def _gmlp(z, ln_w, ln_b, ws, bs, *, rows, keep_v):
    m = z.shape[0]
    row = lambda: pl.BlockSpec((1, W_MIX), lambda i: (0, 0))
    n_out = 2 if keep_v else 1
    return pl.pallas_call(
        functools.partial(_gm_kernel, rows=rows),
        grid=(m // rows,),
        in_specs=[pl.BlockSpec((rows, W_MIX), lambda i: (i, A_GU // W_MIX)),
                  pl.BlockSpec((rows, W_MIX), lambda i: (i, A_GV // W_MIX)),
                  row(), row(),
                  pl.BlockSpec((GM_GROUPS, GM_CHUNK, GM_CHUNK), lambda i: (0, 0, 0)),
                  pl.BlockSpec((GM_CHUNK, GM_GROUPS), lambda i: (0, 0))],
        out_specs=[pl.BlockSpec((rows, W_MIX), lambda i: (i, 0)), pl.BlockSpec((rows, W_MIX), lambda i: (i, 0))][:n_out],
        out_shape=[jax.ShapeDtypeStruct((m, W_MIX), _act_dtype(rows)), jax.ShapeDtypeStruct((m, W_MIX), F32)][:n_out],
        compiler_params=_params(("parallel",)),
        name="gmlp",
    )(z, z, ln_w.reshape(1, W_MIX), ln_b.reshape(1, W_MIX), ws, bs.T)


HALO = 32
CONV_RB = 64


def _conv_kernel(a_ref, b_ref, buf_ref, cw_ref, cb_ref, lw_ref, lb_ref, o_ref, ext_ref, ext_sc, y_sc, *, rows):
    ti = pl.program_id(1)

    @pl.when(ti == 0)
    def _():
        ext_sc[0:HALO, :] = buf_ref[0]

    @pl.when(ti > 0)
    def _():
        ext_sc[0:HALO, :] = ext_sc[rows:rows + HALO, :]

    ext_sc[HALO:HALO + rows, :] = a_ref[...] * _sigmoid(b_ref[...])
    rb = min(CONV_RB, rows)
    first = HALO - (CONV_W - 1)
    for cc in range(W_MIX // LANE):
        lanes = slice(cc * LANE, (cc + 1) * LANE)
        for r0 in range(0, rows, rb):
            acc = jnp.broadcast_to(cb_ref[:, lanes], (rb, LANE))
            for w in range(CONV_W):
                acc = acc + ext_sc[first + w + r0:first + w + r0 + rb, lanes] * cw_ref[w:w + 1, lanes]
            y_sc[r0:r0 + rb, lanes] = acc
    y = _layernorm(y_sc[...], lw_ref[...], lb_ref[...])
    o_ref[...] = (y * _sigmoid(y)).astype(o_ref.dtype)
    ext_ref[0] = ext_sc[...]


def _conv_module(z, buf, cw, cb, ln_w, ln_b, *, nb, rows, n_tiles):
    m = z.shape[0]
    row = lambda: pl.BlockSpec((1, W_MIX), lambda b, t: (0, 0))
    cwp = jnp.concatenate([cw, jnp.zeros((HALO - CONV_W, W_MIX), F32)], axis=0)
    return pl.pallas_call(
        functools.partial(_conv_kernel, rows=rows),
        grid=(nb, n_tiles),
        in_specs=[pl.BlockSpec((rows, W_MIX), lambda b, t: (b * n_tiles + t, B_CA // W_MIX)),
                  pl.BlockSpec((rows, W_MIX), lambda b, t: (b * n_tiles + t, B_CB // W_MIX)),
                  pl.BlockSpec((1, HALO, W_MIX), lambda b, t: (b, 0, 0)),
                  pl.BlockSpec((HALO, W_MIX), lambda b, t: (0, 0)),
                  row(), row(), row()],
        out_specs=[pl.BlockSpec((rows, W_MIX), lambda b, t: (b * n_tiles + t, 0)),
                   pl.BlockSpec((1, HALO + rows, W_MIX), lambda b, t: (b, 0, 0))],
        out_shape=[jax.ShapeDtypeStruct((m, W_MIX), _act_dtype(rows)),
                   jax.ShapeDtypeStruct((nb, HALO + rows, W_MIX), F32)],
        scratch_shapes=[pltpu.VMEM((HALO + rows, W_MIX), F32), pltpu.VMEM((rows, W_MIX), F32)],
        compiler_params=_params(("parallel", "arbitrary")),
        name="conv_module",
    )(z, z, buf, cwp, cb.reshape(1, W_MIX), ln_w.reshape(1, W_MIX), ln_b.reshape(1, W_MIX))


def _overlap_t(n_c_pad, n_s_pad, n_c, n_s):
    cs = np.arange(n_c_pad)[:, None] * CMP_STRIDE
    ss = np.arange(n_s_pad)[None, :] * SEL_LEN
    ov = np.clip(np.minimum(ss + SEL_LEN, cs + CMP_LEN) - np.maximum(ss, cs), 0, None).astype(np.float64)
    ov = ov * (np.arange(n_c_pad)[:, None] < n_c) * (np.arange(n_s_pad)[None, :] < n_s)
    return jnp.asarray(ov, BF16)


def _cmp_p_kernel(x_ref, pe_ref, w1_ref, w2_ref, o_ref, xs_sc, *, t, ncp):
    xs_sc[0:t, :] = x_ref[...]
    xs_sc[t:, :] = jnp.zeros((xs_sc.shape[0] - t, NSA_HD), F32)
    acc = jnp.zeros((ncp, NSA_HD), F32)
    for l in range(CMP_LEN):
        rows = xs_sc[pl.ds(l, ncp, stride=CMP_STRIDE), :] + pe_ref[0, l:l + 1, :]
        acc = acc + _dot(rows.astype(BF16), w1_ref[0, l * NSA_HD:(l + 1) * NSA_HD, :])
    o_ref[0, 0] = _dot(_gelu(acc).astype(BF16), w2_ref[0])


def _compress_prompt(z, pe, w1, w2, *, nb, t):
    ncp = _round_up(t // CMP_STRIDE, LANE)
    pad_rows = _round_up(CMP_STRIDE * (ncp - 1) + CMP_LEN, SUBLANE)
    return pl.pallas_call(
        functools.partial(_cmp_p_kernel, t=t, ncp=ncp),
        grid=(nb, 4),
        in_specs=[pl.BlockSpec((t, NSA_HD), lambda b, j: (b, A_NKV // NSA_HD + j)),
                  pl.BlockSpec((1, CMP_LEN, NSA_HD), lambda b, j: (j // 2, 0, 0)),
                  pl.BlockSpec((1, CMP_LEN * NSA_HD, NSA_HD), lambda b, j: (j // 2, 0, 0)),
                  pl.BlockSpec((1, NSA_HD, NSA_HD), lambda b, j: (j // 2, 0, 0))],
        out_specs=pl.BlockSpec((1, 1, ncp, NSA_HD), lambda b, j: (b, j, 0, 0)),
        out_shape=jax.ShapeDtypeStruct((nb, 4, ncp, NSA_HD), F32),
        scratch_shapes=[pltpu.VMEM((max(pad_rows, t + SUBLANE), NSA_HD), F32)],
        compiler_params=_params(("parallel", "parallel")),
        name="nsa_compress_prompt",
    )(z, pe, w1, w2)


def _nsa_p_kernel(q_ref, kc_ref, vc_ref, ks_ref, vs_ref, kw_ref, vw_ref, ng_ref, ov_ref, ex_ref, o_ref, os_sc,
                  *, t, tq, n_c, n_s, span, key_limits):
    qi = pl.program_id(2)
    kv = pl.program_id(1)
    scale = NSA_HD ** -0.5
    g_n = NSA_GROUP
    q = q_ref[...]
    qh = [q[:, g * NSA_HD:(g + 1) * NSA_HD].astype(BF16) for g in range(g_n)]
    q4 = jnp.concatenate(qh, axis=0)
    qpos = qi * tq + lax.broadcasted_iota(jnp.int32, (tq, 1), 0)
    qpos4 = jnp.concatenate([qpos] * g_n, axis=0)

    kc = kc_ref[0, 0].astype(BF16)
    ncp = kc.shape[0]
    s_c = _dot_nt(q4, kc) * scale
    cidx = lax.broadcasted_iota(jnp.int32, (1, ncp), 1)
    mask_c = (cidx * CMP_STRIDE + CMP_LEN - 1 <= qpos4) & (cidx < n_c)
    p_c = _softmax_lanes(s_c, mask_c)
    o_c = _dot(p_c.astype(BF16), vc_ref[0, 0].astype(BF16))

    p_sum = p_c[0:tq]
    for g in range(1, g_n):
        p_sum = p_sum + p_c[g * tq:(g + 1) * tq]
    imp = _dot_split3(p_sum, ov_ref[...])
    nsp = imp.shape[1]
    blk = lax.broadcasted_iota(jnp.int32, (tq, nsp), 1)
    cur = jnp.right_shift(qpos, SEL_SHIFT)
    forced = (blk == 0) | (blk == cur) | (blk == cur - 1)
    valid = (blk * SEL_LEN <= qpos) & (blk < n_s)
    score = jnp.where(valid, jnp.where(forced, jnp.inf, imp), -jnp.inf)
    rank = jnp.zeros((tq, nsp), F32)
    for j in range(n_s):
        sj = score[:, j:j + 1]
        ahead = (sj > score) | ((sj == score) & (blk > j))
        rank = rank + jnp.where(ahead, 1.0, 0.0)
    sel = valid & (rank < float(min(SEL_TOPK, n_s)))

    sel_b = jnp.where(sel, 1.0, 0.0).astype(BF16)
    for lo, hi, n_keys in key_limits:
        @pl.when((qi >= lo) & (qi < hi))
        def _(n_keys=n_keys):
            kidx = lax.broadcasted_iota(jnp.int32, (1, n_keys), 1)
            mask_s = (_dot(sel_b, ex_ref[:, 0:n_keys]) > 0.5) & (kidx <= qpos)
            kb = ks_ref[0:n_keys, :].astype(BF16)
            vb = vs_ref[0:n_keys, :].astype(BF16)
            for g in range(g_n):
                e, inv = _softmax_parts(_dot_nt(qh[g], kb) * scale, mask_s)
                os_sc[g * tq:(g + 1) * tq, :] = _dot(e.astype(BF16), vb) * inv

    o_s = os_sc[...]

    w0 = pl.multiple_of(jnp.clip(qi * tq - WINDOW, 0, t - span), LANE)
    kw = kw_ref[pl.ds(w0, span), :].astype(BF16)
    vw = vw_ref[pl.ds(w0, span), :].astype(BF16)
    kpos = w0 + lax.broadcasted_iota(jnp.int32, (1, span), 1)
    dist = qpos4 - kpos
    mask_w = (dist >= 0) & (dist <= WINDOW)
    e_w, inv_w = _softmax_parts(_dot_nt(q4, kw) * scale, mask_w)
    o_w = _dot(e_w.astype(BF16), vw) * inv_w

    gates = _sigmoid(ng_ref[...])
    for g in range(g_n):
        rows = slice(g * tq, (g + 1) * tq)
        col = (kv * g_n + g) * 3
        lane = lax.broadcasted_iota(jnp.int32, gates.shape, 1)
        pick = lambda j: jnp.sum(jnp.where(lane == col + j, gates, 0.0), axis=-1, keepdims=True)
        o = pick(0) * o_c[rows] + pick(1) * o_s[rows] + pick(2) * o_w[rows]
        o_ref[:, g * NSA_HD:(g + 1) * NSA_HD] = o.astype(o_ref.dtype)


KEY_PREFIX_VARIANTS = 4


def _nsa_prompt(z, zc, cmp_kv, *, nb, t):
    m = z.shape[0]
    tq = 128
    nq = t // tq
    ends = sorted({-(-v * nq // KEY_PREFIX_VARIANTS) for v in range(1, KEY_PREFIX_VARIANTS + 1)})
    key_limits = tuple((lo, hi, hi * tq) for lo, hi in zip([0] + ends[:-1], ends))
    n_c = (t - CMP_LEN) // CMP_STRIDE + 1
    n_s = -(-t // SEL_LEN)
    ncp = cmp_kv.shape[2]
    nsp = LANE
    assert n_s <= nsp
    span = min(WINDOW + tq, t)
    ov = _overlap_t(ncp, nsp, n_c, n_s)
    ex = jnp.asarray((np.arange(t)[None, :] // SEL_LEN) == np.arange(nsp)[:, None], BF16)
    kvw = NSA_GROUP * NSA_HD
    nk = A_NKV // NSA_HD
    full = lambda j0: pl.BlockSpec((t, NSA_HD), lambda b, kv, qi: (b, nk + j0 + kv))
    return pl.pallas_call(
        functools.partial(_nsa_p_kernel, t=t, tq=tq, n_c=n_c, n_s=n_s, span=span, key_limits=key_limits),
        grid=(nb, NSA_KV, nq),
        in_specs=[pl.BlockSpec((tq, kvw), lambda b, kv, qi: (b * nq + qi, A_NQ // kvw + kv)),
                  pl.BlockSpec((1, 1, ncp, NSA_HD), lambda b, kv, qi: (b, kv, 0, 0)),
                  pl.BlockSpec((1, 1, ncp, NSA_HD), lambda b, kv, qi: (b, 2 + kv, 0, 0)),
                  full(4), full(6), full(8), full(10),
                  pl.BlockSpec((tq, LANE), lambda b, kv, qi: (b * nq + qi, 0)),
                  pl.BlockSpec((ncp, nsp), lambda b, kv, qi: (0, 0)),
                  pl.BlockSpec((nsp, t), lambda b, kv, qi: (0, 0))],
        out_specs=pl.BlockSpec((tq, kvw), lambda b, kv, qi: (b * nq + qi, kv)),
        out_shape=jax.ShapeDtypeStruct((m, W_MIX), BF16),
        scratch_shapes=[pltpu.VMEM((NSA_GROUP * tq, NSA_HD), F32)],
        compiler_params=_params(("parallel", "parallel", "arbitrary"), VMEM_BIG),
        name="nsa_prompt",
    )(z, cmp_kv, cmp_kv, z, z, z, z, zc, ov, ex)


PAGES_PER_STEP = 8
PAGE_ROWS = PAGE_SIZE * 4 * NSA_KV
GROUPS_PER_PAGE = PAGE_SIZE // CMP_STRIDE


def _s_cmp_kernel(pt_ref, *refs):
    del pt_ref
    pages = refs[:PAGES_PER_STEP]
    w_ref, o_ref = refs[PAGES_PER_STEP], refs[PAGES_PER_STEP + 1]
    gp = PAGES_PER_STEP * GROUPS_PER_PAGE
    for which in range(2):
        acc = jnp.zeros((NSA_KV * gp, 2 * NSA_HD), F32)
        for l in range(CMP_STRIDE):
            pieces = [pages[p][pl.ds(l * 4 * NSA_KV + which * NSA_KV + kv, GROUPS_PER_PAGE, stride=CMP_STRIDE * 4 * NSA_KV), :]
                      for kv in range(NSA_KV) for p in range(PAGES_PER_STEP)]
            acc = acc + _dot(jnp.concatenate(pieces, axis=0).astype(BF16), w_ref[which, l])
        for kv in range(NSA_KV):
            o_ref[0, which, kv] = acc[kv * gp:(kv + 1) * gp]


def _page_specs(layer, n):
    def spec(p):
        return pl.BlockSpec((None, None, PAGE_ROWS, NSA_HD),
                            lambda b, s, pt: (layer, pt[b, s * PAGES_PER_STEP + p], 0, 0))
    return [spec(p) for p in range(n)]


def _compress_sample_partial(cache4, page_table, w1ab, *, layer, nb, n_pages):
    steps = n_pages // PAGES_PER_STEP
    gp = PAGES_PER_STEP * GROUPS_PER_PAGE
    ng = n_pages * GROUPS_PER_PAGE
    return pl.pallas_call(
        _s_cmp_kernel,
        grid_spec=pltpu.PrefetchScalarGridSpec(
            num_scalar_prefetch=1,
            grid=(nb, steps),
            in_specs=_page_specs(layer, PAGES_PER_STEP)
            + [pl.BlockSpec((2, CMP_STRIDE, NSA_HD, 2 * NSA_HD), lambda b, s, pt: (0, 0, 0, 0))],
            out_specs=pl.BlockSpec((1, 2, NSA_KV, gp, 2 * NSA_HD), lambda b, s, pt: (b, 0, 0, s, 0)),
        ),
        out_shape=jax.ShapeDtypeStruct((nb, 2, NSA_KV, ng, 2 * NSA_HD), F32),
        compiler_params=_params(("parallel", "arbitrary"), VMEM_BIG),
        name="nsa_compress_sample",
    )(page_table, *([cache4] * PAGES_PER_STEP), w1ab)


def _s_sel_kernel(uv_ref, pef_ref, w1_ref, w2_ref, q_ref, ng_ref, kwn_ref, vwn_ref, kwp_ref, vwp_ref, ov_ref,
                  selt_ref, op_ref, *, q0, valid_rows, n_c, n_s):
    kv = pl.program_id(1)
    scale = NSA_HD ** -0.5
    g_n = NSA_GROUP
    rows = SLAB
    q = q_ref[...]
    q4 = jnp.concatenate([q[:, g * NSA_HD:(g + 1) * NSA_HD] for g in range(g_n)], axis=0).astype(BF16)
    tpos = lax.broadcasted_iota(jnp.int32, (rows, 1), 0)
    qpos = q0 + tpos
    qpos4 = jnp.concatenate([qpos] * g_n, axis=0)

    def compressed(which):
        uv = uv_ref[0, which, 0]
        ng = uv.shape[0]
        u = uv[:, :NSA_HD]
        v_next = pltpu.roll(uv[:, NSA_HD:], ng - 1, 0)
        const = _dot(pef_ref[which], w1_ref[which])[0:1]
        return _dot(_gelu(u + v_next + const).astype(BF16), w2_ref[which])

    kc = compressed(0)
    vc = compressed(1)
    ncp = kc.shape[0]
    s_c = _dot_nt(q4, kc.astype(BF16)) * scale
    cidx = lax.broadcasted_iota(jnp.int32, (1, ncp), 1)
    mask_c = (cidx * CMP_STRIDE + CMP_LEN - 1 <= qpos4) & (cidx < n_c)
    p_c = _softmax_lanes(s_c, mask_c)
    o_c = _dot(p_c.astype(BF16), vc.astype(BF16))

    p_sum = p_c[0:rows]
    for g in range(1, g_n):
        p_sum = p_sum + p_c[g * rows:(g + 1) * rows]
    imp = _dot_split3(p_sum, ov_ref[...])
    nsp = imp.shape[1]
    blk = lax.broadcasted_iota(jnp.int32, (rows, nsp), 1)
    cur = jnp.right_shift(qpos, SEL_SHIFT)
    forced = (blk == 0) | (blk == cur) | (blk == cur - 1)
    valid = (blk * SEL_LEN <= qpos) & (blk < n_s)
    score = jnp.where(valid, jnp.where(forced, jnp.inf, imp), -jnp.inf)
    score_pad = jnp.concatenate([score, jnp.zeros((LANE - rows, nsp), F32)], axis=0)
    score_t = jnp.transpose(score_pad)
    bi = lax.broadcasted_iota(jnp.int32, (nsp, nsp), 0)
    bj = lax.broadcasted_iota(jnp.int32, (nsp, nsp), 1)
    lane = lax.broadcasted_iota(jnp.int32, (nsp, LANE), 1)
    sel_t = jnp.zeros((nsp, LANE), F32)
    k_take = float(min(SEL_TOPK, n_s))
    for tt in range(valid_rows):
        s_i = score_t[:, tt:tt + 1]
        s_j = score[tt:tt + 1, :]
        ahead = (s_j > s_i) | ((s_j == s_i) & (bj < bi))
        rank_i = jnp.sum(jnp.where(ahead, 1.0, 0.0), axis=-1, keepdims=True)
        ok_i = (rank_i < k_take) & (s_i > -jnp.inf)
        sel_t = jnp.where(((lane & (rows - 1)) == tt) & (lane < g_n * rows) & ok_i, 1.0, sel_t)
    selt_ref[0, 0] = sel_t

    kw = jnp.concatenate([kwp_ref[...], kwn_ref[...]], axis=0).astype(BF16)
    vw = jnp.concatenate([vwp_ref[...], vwn_ref[...]], axis=0).astype(BF16)
    n_before = kwp_ref.shape[0]
    widx = lax.broadcasted_iota(jnp.int32, (1, n_before + rows), 1)
    kpos = jnp.where(widx < n_before, q0 - n_before + widx, q0 + widx - n_before)
    dist = qpos4 - kpos
    mask_w = (dist >= 0) & (dist <= WINDOW) & (kpos >= 0)
    s_w = _dot_nt(q4, kw) * scale
    p_w = _softmax_lanes(s_w, mask_w)
    o_w = _dot(p_w.astype(BF16), vw)

    gates = _sigmoid(ng_ref[...])
    glane = lax.broadcasted_iota(jnp.int32, gates.shape, 1)
    for g in range(g_n):
        r = slice(g * rows, (g + 1) * rows)
        col = (kv * g_n + g) * 3
        pick = lambda j: jnp.sum(jnp.where(glane == col + j, gates, 0.0), axis=-1, keepdims=True)
        op_ref[0, 0, r, :] = pick(0) * o_c[r] + pick(2) * o_w[r]


def _s_attn_kernel(pt_ref, *refs, n_steps, valid_rows):
    del pt_ref
    pages = refs[:PAGES_PER_STEP]
    q_ref, kvn_ref, ng_ref, selt_ref, op_ref, o_ref, m_sc, l_sc, acc_sc = refs[PAGES_PER_STEP:]
    s_id = pl.program_id(1)
    scale = NSA_HD ** -0.5
    g_n = NSA_GROUP
    rows = SLAB
    blocks_per_step = PAGES_PER_STEP * PAGE_SIZE // SEL_LEN

    @pl.when(s_id == 0)
    def _():
        m_sc[...] = jnp.full(m_sc.shape, NEG, F32)
        l_sc[...] = jnp.zeros(l_sc.shape, F32)
        acc_sc[...] = jnp.zeros(acc_sc.shape, F32)

    q = q_ref[...]

    def q_rows(kv):
        q4 = jnp.concatenate([q[:, (kv * g_n + g) * NSA_HD:(kv * g_n + g + 1) * NSA_HD] for g in range(g_n)], axis=0)
        return jnp.concatenate([q4, jnp.zeros((LANE - g_n * rows, NSA_HD), F32)], axis=0).astype(BF16)

    def update(kv, s_t, mask_t, v):
        m_old = m_sc[kv]
        m_new = jnp.maximum(m_old, jnp.max(jnp.where(mask_t, s_t, NEG), axis=0, keepdims=True))
        alpha = jnp.exp(m_old - m_new)
        p_t = jnp.where(mask_t, jnp.exp(jnp.minimum(s_t - m_new, 0.0)), 0.0)
        l_sc[kv] = alpha * l_sc[kv] + jnp.sum(p_t, axis=0, keepdims=True)
        acc_sc[kv] = alpha * acc_sc[kv] + _dot_tn(v, p_t.astype(BF16))
        m_sc[kv] = m_new

    stride = 4 * NSA_KV
    for kv in range(NSA_KV):
        qk = q_rows(kv)
        k = jnp.concatenate([pg[pl.ds(2 * NSA_KV + kv, PAGE_SIZE, stride=stride), :] for pg in pages], axis=0)
        v = jnp.concatenate([pg[pl.ds(3 * NSA_KV + kv, PAGE_SIZE, stride=stride), :] for pg in pages], axis=0)
        s_t = _dot_nt(k.astype(BF16), qk) * scale
        start = pl.multiple_of(s_id * blocks_per_step, blocks_per_step)
        chunk = selt_ref[0, kv, pl.ds(start, blocks_per_step), :]
        mask_t = jnp.concatenate([jnp.broadcast_to(chunk[c:c + 1, :], (SEL_LEN, LANE))
                                  for c in range(blocks_per_step)], axis=0) > 0.5
        update(kv, s_t, mask_t, v.astype(BF16))

    @pl.when(s_id == n_steps - 1)
    def _():
        n_past_blocks = n_steps * blocks_per_step
        kvn = jnp.concatenate([kvn_ref[...], jnp.zeros((LANE - rows, 4 * NSA_HD), F32)], axis=0)
        gates = _sigmoid(ng_ref[...])
        glane = lax.broadcasted_iota(jnp.int32, gates.shape, 1)
        for kv in range(NSA_KV):
            k_new = kvn[:, kv * NSA_HD:(kv + 1) * NSA_HD].astype(BF16)
            v_new = kvn[:, (NSA_KV + kv) * NSA_HD:(NSA_KV + kv + 1) * NSA_HD].astype(BF16)
            s_t = _dot_nt(k_new, q_rows(kv)) * scale
            key = lax.broadcasted_iota(jnp.int32, (LANE, LANE), 0)
            tok = lax.broadcasted_iota(jnp.int32, (LANE, LANE), 1) & (rows - 1)
            blk_ok = selt_ref[0, kv, n_past_blocks:n_past_blocks + 1, :] > 0.5
            mask_t = (key <= tok) & (key < valid_rows) & blk_ok
            update(kv, s_t, mask_t, v_new)
            l = l_sc[kv]
            o_t = acc_sc[kv] / jnp.where(l > 0.0, l, 1.0)
            o_sel = jnp.transpose(o_t)
            for g in range(g_n):
                r = slice(g * rows, (g + 1) * rows)
                col = (kv * g_n + g) * 3 + 1
                gate = jnp.sum(jnp.where(glane == col, gates, 0.0), axis=-1, keepdims=True)
                o = op_ref[0, kv, r, :] + gate * o_sel[r]
                o_ref[:, (kv * g_n + g) * NSA_HD:(kv * g_n + g + 1) * NSA_HD] = o.astype(o_ref.dtype)


def _nsa_sample(zs, zc, cache4, page_table, win_past, pe, w1, w2, *, layer, nb, valid_rows, q0):
    n_pages = page_table.shape[1]
    assert n_pages % PAGES_PER_STEP == 0 and q0 == n_pages * PAGE_SIZE
    s_len = q0 + valid_rows
    n_c = (s_len - CMP_LEN) // CMP_STRIDE + 1
    n_s = -(-s_len // SEL_LEN)
    ng = n_pages * GROUPS_PER_PAGE
    assert n_c <= ng - 1
    nsp = _round_up(n_s, LANE)
    steps = n_pages // PAGES_PER_STEP
    w1b = w1.astype(BF16)
    w1r = w1b.reshape(2, CMP_LEN, NSA_HD, NSA_HD)
    w1ab = jnp.concatenate([w1r[:, :CMP_STRIDE], w1r[:, CMP_STRIDE:]], axis=-1)
    uv = _compress_sample_partial(cache4, page_table, w1ab, layer=layer, nb=nb, n_pages=n_pages)
    pef = jnp.broadcast_to(pe.reshape(2, 1, CMP_LEN * NSA_HD), (2, SUBLANE, CMP_LEN * NSA_HD)).astype(BF16)
    ov = _overlap_t(ng, nsp, n_c, n_s)
    kvw = NSA_GROUP * NSA_HD
    nk = A_NKV // NSA_HD
    n_before = win_past.shape[4]
    selt, o_part = pl.pallas_call(
        functools.partial(_s_sel_kernel, q0=q0, valid_rows=valid_rows, n_c=n_c, n_s=n_s),
        grid=(nb, NSA_KV),
        in_specs=[pl.BlockSpec((1, 2, 1, ng, 2 * NSA_HD), lambda b, kv: (b, 0, kv, 0, 0)),
                  pl.BlockSpec((2, SUBLANE, CMP_LEN * NSA_HD), lambda b, kv: (0, 0, 0)),
                  pl.BlockSpec((2, CMP_LEN * NSA_HD, NSA_HD), lambda b, kv: (0, 0, 0)),
                  pl.BlockSpec((2, NSA_HD, NSA_HD), lambda b, kv: (0, 0, 0)),
                  pl.BlockSpec((SLAB, kvw), lambda b, kv: (b, A_NQ // kvw + kv)),
                  pl.BlockSpec((SLAB, LANE), lambda b, kv: (b, 0)),
                  pl.BlockSpec((SLAB, NSA_HD), lambda b, kv: (b, nk + 8 + kv)),
                  pl.BlockSpec((SLAB, NSA_HD), lambda b, kv: (b, nk + 10 + kv)),
                  pl.BlockSpec((None, None, None, None, n_before, NSA_HD), lambda b, kv: (layer, b, 0, kv, 0, 0)),
                  pl.BlockSpec((None, None, None, None, n_before, NSA_HD), lambda b, kv: (layer, b, 1, kv, 0, 0)),
                  pl.BlockSpec((ng, nsp), lambda b, kv: (0, 0))],
        out_specs=[pl.BlockSpec((1, 1, nsp, LANE), lambda b, kv: (b, kv, 0, 0)),
                   pl.BlockSpec((1, 1, NSA_GROUP * SLAB, NSA_HD), lambda b, kv: (b, kv, 0, 0))],
        out_shape=[jax.ShapeDtypeStruct((nb, NSA_KV, nsp, LANE), F32),
                   jax.ShapeDtypeStruct((nb, NSA_KV, NSA_GROUP * SLAB, NSA_HD), F32)],
        compiler_params=_params(("parallel", "parallel"), VMEM_BIG),
        name="nsa_select_sample",
    )(uv, pef, w1b, w2.astype(BF16), zs, zc, zs, zs, win_past, win_past, ov)
    return pl.pallas_call(
        functools.partial(_s_attn_kernel, n_steps=steps, valid_rows=valid_rows),
        grid_spec=pltpu.PrefetchScalarGridSpec(
            num_scalar_prefetch=1,
            grid=(nb, steps),
            in_specs=_page_specs(layer, PAGES_PER_STEP)
            + [pl.BlockSpec((SLAB, NSA_HEADS * NSA_HD), lambda b, s, pt: (b, A_NQ // (NSA_HEADS * NSA_HD))),
               pl.BlockSpec((SLAB, 4 * NSA_HD), lambda b, s, pt: (b, (A_NKV + 4 * NSA_HD) // (4 * NSA_HD))),
               pl.BlockSpec((SLAB, LANE), lambda b, s, pt: (b, 0)),
               pl.BlockSpec((1, NSA_KV, nsp, LANE), lambda b, s, pt: (b, 0, 0, 0)),
               pl.BlockSpec((1, NSA_KV, NSA_GROUP * SLAB, NSA_HD), lambda b, s, pt: (b, 0, 0, 0))],
            out_specs=pl.BlockSpec((SLAB, W_MIX), lambda b, s, pt: (b, 0)),
            scratch_shapes=[pltpu.VMEM((NSA_KV, 1, LANE), F32), pltpu.VMEM((NSA_KV, 1, LANE), F32),
                            pltpu.VMEM((NSA_KV, NSA_HD, LANE), F32)],
        ),
        out_shape=jax.ShapeDtypeStruct((nb * SLAB, W_MIX), _act_dtype(SLAB)),
        compiler_params=_params(("parallel", "arbitrary"), VMEM_BIG),
        name="nsa_attend_sample",
    )(page_table, *([cache4] * PAGES_PER_STEP), zs, zs, zc, selt, o_part)


def _merge_kernel(a_ref, b_ref, c_ref, d_ref, w_ref, g0_ref, g1_ref, g2_ref, g3_ref, o_ref):
    acc = None
    for n, (x_ref, g_ref) in enumerate(((a_ref, g0_ref), (b_ref, g1_ref), (c_ref, g2_ref), (d_ref, g3_ref))):
        term = _sigmoid(g_ref[...]) * _dot(x_ref[...].astype(BF16), w_ref[n])
        acc = term if acc is None else acc + term
    o_ref[...] = acc.astype(o_ref.dtype)


def _merge(branches, w_branch, zb, *, layer):
    m = zb.shape[0]
    tm = _tile(m, 512)
    tn = 1024
    br = pl.BlockSpec((tm, W_MIX), lambda i, j: (i, 0))
    gz = lambda n: pl.BlockSpec((tm, tn), lambda i, j: (i, (B_GZ + n * D_MODEL) // tn + j))
    return pl.pallas_call(
        _merge_kernel,
        grid=(m // tm, D_MODEL // tn),
        in_specs=[br, br, br, br, pl.BlockSpec((None, N_BRANCH, W_MIX, tn), lambda i, j: (layer, 0, 0, j)),
                  gz(0), gz(1), gz(2), gz(3)],
        out_specs=pl.BlockSpec((tm, tn), lambda i, j: (i, j)),
        out_shape=jax.ShapeDtypeStruct((m, D_MODEL), BF16),
        compiler_params=_params(("parallel", "parallel"), VMEM_BIG),
        name="branch_merge",
    )(*branches, w_branch, zb, zb, zb, zb)


def kernel(x_prompt, x_sample, cache_nsa_kv, state_ret, state_win_kv, state_conv, page_table, norm1, w_in, ret_gn_w,
           ret_gn_b, gm_ln_w, gm_ln_b, gm_ws, gm_bs, nsa_pe, nsa_w1, nsa_w2, conv_w, conv_b, conv_ln_w, conv_ln_b,
           w_branch, w_out, norm2, w_up, w_down, final_norm):
    bp, t, d = x_prompt.shape
    bs, ts, _ = x_sample.shape
    assert d == D_MODEL and t % RET_CHUNK == 0 and ts <= SLAB and norm1.shape[0] == DEPTH
    n_pages = page_table.shape[1]
    past_len = n_pages * PAGE_SIZE
    conv_rows = 128

    w_in_b = w_in[:, :, _O_CA:_O_END].astype(BF16)
    w_in_c = jnp.pad(w_in[:, :, _O_NG:_O_CA], ((0, 0), (0, 0), (0, N_C - (_O_CA - _O_NG))))
    w_branch_b = w_branch.astype(BF16)
    nsa_w1_b = nsa_w1.astype(BF16)
    nsa_w2_b = nsa_w2.astype(BF16)

    xp = x_prompt.reshape(bp * t, d)
    xs = jnp.pad(x_sample, ((0, 0), (0, SLAB - ts), (0, 0))).reshape(bs * SLAB, d)
    cache4 = cache_nsa_kv.reshape(DEPTH, cache_nsa_kv.shape[1], PAGE_ROWS, NSA_HD)
    win_t = jnp.transpose(state_win_kv, (0, 1, 3, 4, 2, 5))
    conv_pad = jnp.pad(state_conv, ((0, 0), (0, 0), (HALO - (CONV_W - 1), 0), (0, 0)))
    ret0_p = jnp.zeros((bp, RET_HEADS, RET_DK, RET_DV), F32)
    conv0_p = jnp.zeros((bp, HALO, W_MIX), F32)
    cos_p, sin_p = _rope_tables(np.arange(t), t)
    cos_s, sin_s = _rope_tables(past_len + np.arange(ts), SLAB)
    c_eff_s = math.gcd(ts, RET_CHUNK)
    assert c_eff_s == ts

    keep_p = min(WINDOW, t)
    keep_s = min(WINDOW, past_len + ts)
    ret_p, ret_s, kv_p, kv_s, win_p, win_s, conv_p, conv_s, gm_s = ([] for _ in range(9))
    for l in range(DEPTH):
        hp = _rmsnorm(xp, norm1[l], slab=1, valid=1, out_dtype=BF16)
        hs = _rmsnorm(xs, norm1[l], slab=SLAB, valid=ts, out_dtype=BF16)
        zap, zas = _dense(hp, hs, w_in, layer=l, n=N_A, tn=256, name="in_proj_a")
        zbp, zbs = _dense(hp, hs, w_in_b, layer=l, n=N_B, tn=256, name="in_proj_b")
        zcp, zcs = _dense(hp, hs, w_in_c, layer=l, n=N_C, tn=N_C, name="in_proj_c")

        o_ret_p, r_p = _retention(zap, ret0_p, ret_gn_w[l], ret_gn_b[l], nb=bp, rows=RET_CHUNK,
                                  n_chunks=t // RET_CHUNK, c_eff=RET_CHUNK, cos=cos_p, sin=sin_p)
        (o_gm_p,) = _gmlp(zap, gm_ln_w[l], gm_ln_b[l], gm_ws[l], gm_bs[l], rows=GM_CHUNK, keep_v=False)
        cmp_kv = _compress_prompt(zap, nsa_pe[l], nsa_w1_b[l], nsa_w2_b[l], nb=bp, t=t)
        o_nsa_p = _nsa_prompt(zap, zcp, cmp_kv, nb=bp, t=t)
        o_conv_p, ext_p = _conv_module(zbp, conv0_p, conv_w[l], conv_b[l], conv_ln_w[l], conv_ln_b[l], nb=bp,
                                       rows=conv_rows, n_tiles=t // conv_rows)
        merged_p = _merge((o_ret_p, o_gm_p, o_nsa_p, o_conv_p), w_branch_b, zbp, layer=l)

        o_ret_s, r_s = _retention(zas, state_ret[l], ret_gn_w[l], ret_gn_b[l], nb=bs, rows=SLAB, n_chunks=1,
                                  c_eff=c_eff_s, cos=cos_s, sin=sin_s)
        o_gm_s, gv_s = _gmlp(zas, gm_ln_w[l], gm_ln_b[l], gm_ws[l], gm_bs[l], rows=SLAB, keep_v=True)
        o_nsa_s = _nsa_sample(zas, zcs, cache4, page_table, win_t, nsa_pe[l], nsa_w1[l], nsa_w2[l], layer=l, nb=bs,
                              valid_rows=ts, q0=past_len)
        o_conv_s, ext_s = _conv_module(zbs, conv_pad[l], conv_w[l], conv_b[l], conv_ln_w[l], conv_ln_b[l], nb=bs,
                                       rows=SLAB, n_tiles=1)
        merged_s = _merge((o_ret_s, o_gm_s, o_nsa_s, o_conv_s), w_branch_b, zbs, layer=l)

        xp, xs = _dense(merged_p, merged_s, w_out, layer=l, n=D_MODEL, tn=256, tm=1024, resp=xp, ress=xs,
                        name="out_proj")
        h2p = _rmsnorm(xp, norm2[l], slab=1, valid=1, out_dtype=BF16)
        h2s = _rmsnorm(xs, norm2[l], slab=1, valid=1, out_dtype=BF16)
        up, us = _dense(h2p, h2s, w_up, layer=l, n=D_FF, tn=256, act="relu2", out_dtype=BF16, name="mlp_up")
        xp, xs = _dense(up, us, w_down, layer=l, n=D_MODEL, tn=1024, tk=1024, resp=xp, ress=xs, name="mlp_down")

        ret_p.append(r_p)
        ret_s.append(r_s)
        z3p = zap.reshape(bp, t, N_A)
        kv_p.append(z3p[:, :, A_NKV:A_NKV + 8 * NSA_HD].reshape(bp, t, 4, NSA_KV, NSA_HD))
        win_p.append(z3p[:, t - keep_p:, A_NKV + 8 * NSA_HD:A_NKV + 12 * NSA_HD].reshape(bp, keep_p, 2, NSA_KV, NSA_HD))
        z3s = zas.reshape(bs, SLAB, N_A)[:, :ts]
        kv_s.append(z3s[:, :, A_NKV:A_NKV + 8 * NSA_HD].reshape(bs, ts, 4, NSA_KV, NSA_HD))
        new_win = z3s[:, :, A_NKV + 8 * NSA_HD:A_NKV + 12 * NSA_HD].reshape(bs, ts, 2, NSA_KV, NSA_HD)
        win_s.append(jnp.concatenate([state_win_kv[l], new_win], axis=1)[:, -keep_s:])
        conv_p.append(ext_p[:, HALO + conv_rows - (CONV_W - 1):HALO + conv_rows])
        conv_s.append(ext_s[:, HALO + ts - (CONV_W - 1):HALO + ts])
        gm_s.append(gv_s.reshape(bs, SLAB, W_MIX)[:, :ts])

    y_p = _rmsnorm(xp, final_norm, slab=1, valid=1, out_dtype=F32)
    y_s = _rmsnorm(xs, final_norm, slab=1, valid=1, out_dtype=F32)
    return (y_p.reshape(bp, t, d), y_s.reshape(bs, SLAB, d)[:, :ts],
            jnp.stack(ret_p), jnp.stack(ret_s), jnp.stack(kv_p), jnp.stack(kv_s), jnp.stack(win_p), jnp.stack(win_s),
            jnp.stack(conv_p), jnp.stack(conv_s), jnp.stack(gm_s))
```

```python
import functools
import math

import numpy as np
import jax
import jax.numpy as jnp
from jax import lax
from jax.experimental import pallas as pl
from jax.experimental.pallas import tpu as pltpu

F32 = jnp.float32
BF16 = jnp.bfloat16

D_MODEL = 4096
DEPTH = 2
PAGE_SIZE = 128
W_MIX = D_MODEL // 4
N_BRANCH = 4
RET_HEADS = 4
RET_DK = W_MIX // RET_HEADS
RET_DV = W_MIX // RET_HEADS
RET_CHUNK = 128
ROPE_BASE = 10000.0
GM_CHUNK = 128
GM_GROUPS = 4
NSA_HEADS = 8
NSA_KV = 2
NSA_HD = W_MIX // NSA_HEADS
NSA_GROUP = NSA_HEADS // NSA_KV
CMP_LEN = 32
CMP_STRIDE = 16
SEL_LEN = 64
SEL_SHIFT = 6
SEL_TOPK = 16
WINDOW = 512
CONV_W = 31
D_FF = 4 * D_MODEL
EPS = 1e-6
LN_EPS = 1e-5

LANE = 128
SUBLANE = 8
SLAB = 8
VMEM_BIG = 56 * 1024 * 1024

A_RQ, A_RK, A_RV, A_RG = 0, 1024, 2048, 3072
A_GU, A_GV = 4096, 5120
A_NQ = 6144
B_CA, B_CB, B_GZ = 7168, 8192, 9216
A_NKV = 25600
C_NG = 27136
N_Z = 27648
W_BLK = 256
_O_NKV, _O_NG, _O_CA, _O_END = 7168, 8704, 8728, 27160

NEG = -1e30


def _round_up(a, b):
    return -(-a // b) * b


def _tile(m, pref):
    best = None
    for t in range(SUBLANE, min(m, pref) + 1, SUBLANE):
        if m % t == 0:
            best = t
    assert best is not None, (m, pref)
    return best


def _act_dtype(rows):
    return BF16 if rows % (2 * SUBLANE) == 0 else F32


def _params(sem, vmem=None):
    return pltpu.CompilerParams(dimension_semantics=sem, vmem_limit_bytes=vmem)


def _gelu(x):
    return 0.5 * x * (1.0 + jnp.tanh(0.7978845608028654 * (x + 0.044715 * (x * x * x))))


def _sigmoid(x):
    return 1.0 / (1.0 + jnp.exp(-x))


def _layernorm(x, w, b):
    mu = jnp.mean(x, axis=-1, keepdims=True)
    xc = x - mu
    var = jnp.mean(xc * xc, axis=-1, keepdims=True)
    return xc * lax.rsqrt(var + LN_EPS) * w + b


def _softmax_lanes(s, mask):
    sm = jnp.where(mask, s, NEG)
    m = jnp.max(sm, axis=-1, keepdims=True)
    m = jnp.where(m > 0.5 * NEG, m, 0.0)
    e = jnp.where(mask, jnp.exp(sm - m), 0.0)
    den = jnp.sum(e, axis=-1, keepdims=True)
    return e / jnp.where(den > 0.0, den, 1.0)


def _softmax_parts(s, mask):
    sm = jnp.where(mask, s, NEG)
    m = jnp.max(sm, axis=-1, keepdims=True)
    m = jnp.where(m > 0.5 * NEG, m, 0.0)
    e = jnp.exp(sm - m)
    den = jnp.sum(e, axis=-1, keepdims=True)
    return e, 1.0 / jnp.where(den > 0.0, den, 1.0)


def _dot(a, b):
    return jnp.dot(a, b, preferred_element_type=F32)


def _dot_nt(a, b):
    return lax.dot_general(a, b, (((1,), (1,)), ((), ())), preferred_element_type=F32)


def _dot_tn(a, b):
    return lax.dot_general(a, b, (((0,), (0,)), ((), ())), preferred_element_type=F32)


def _dot_split3(a, b_bf16):
    a1 = a.astype(BF16)
    r1 = a - a1.astype(F32)
    a2 = r1.astype(BF16)
    a3 = (r1 - a2.astype(F32)).astype(BF16)
    return _dot(a1, b_bf16) + _dot(a2, b_bf16) + _dot(a3, b_bf16)


def _rmsnorm_kernel(x_ref, w_ref, o_ref, *, slab, valid):
    x = x_ref[...]
    y = x * lax.rsqrt(jnp.mean(x * x, axis=-1, keepdims=True) + EPS) * w_ref[...]
    if valid < slab:
        row = lax.broadcasted_iota(jnp.int32, y.shape, 0)
        y = jnp.where((row & (slab - 1)) < valid, y, 0.0)
    o_ref[...] = y.astype(o_ref.dtype)


def _rmsnorm(x, w, *, slab, valid, out_dtype):
    m, d = x.shape
    tr = _tile(m, 256)
    assert tr % slab == 0 or valid == slab
    return pl.pallas_call(
        functools.partial(_rmsnorm_kernel, slab=slab, valid=valid),
        grid=(m // tr,),
        in_specs=[pl.BlockSpec((tr, d), lambda i: (i, 0)), pl.BlockSpec((1, d), lambda i: (0, 0))],
        out_specs=pl.BlockSpec((tr, d), lambda i: (i, 0)),
        out_shape=jax.ShapeDtypeStruct((m, d), out_dtype),
        compiler_params=_params(("parallel",)),
        name="rmsnorm",
    )(x, w.reshape(1, d))


def _cast_kernel(x_ref, o_ref):
    o_ref[...] = x_ref[...].astype(o_ref.dtype)


def _cast_bf16(w, *, rows):
    depth, r, c = w.shape
    return pl.pallas_call(
        _cast_kernel,
        grid=(depth, r // rows),
        in_specs=[pl.BlockSpec((None, rows, c), lambda l, i: (l, i, 0))],
        out_specs=pl.BlockSpec((None, rows, c), lambda l, i: (l, i, 0)),
        out_shape=jax.ShapeDtypeStruct(w.shape, BF16),
        compiler_params=_params(("parallel", "parallel"), VMEM_BIG),
        name="cast_weight",
    )(w)


def _w_in_source_row(j):
    nq_blocks = _O_NKV // W_BLK
    gz_end = nq_blocks + (_O_END - _O_CA) // W_BLK
    nkv_end = gz_end + (_O_NG - _O_NKV) // W_BLK
    src = jnp.where(j < nq_blocks, j * W_BLK,
                    jnp.where(j < gz_end, _O_CA + (j - nq_blocks) * W_BLK,
                              jnp.where(j < nkv_end, _O_NKV + (j - gz_end) * W_BLK, _O_NG)))
    return pl.multiple_of(src, SUBLANE)


def _cast_w_in_kernel(x_ref, o_ref):
    o_ref[...] = jnp.transpose(x_ref[0]).astype(o_ref.dtype)


def _cast_w_in(w_in):
    depth, k, n_cols = w_in.shape
    assert n_cols == _O_END and N_Z % W_BLK == 0 and _O_NG + W_BLK <= n_cols
    wt = jnp.swapaxes(w_in, 1, 2)
    return pl.pallas_call(
        _cast_w_in_kernel,
        grid=(depth, N_Z // W_BLK),
        in_specs=[pl.BlockSpec((pl.Element(1), pl.Element(W_BLK), pl.Element(k)),
                               lambda l, j: (l, _w_in_source_row(j), 0))],
        out_specs=pl.BlockSpec((None, k, W_BLK), lambda l, j: (l, 0, j)),
        out_shape=jax.ShapeDtypeStruct((depth, k, N_Z), BF16),
        compiler_params=_params(("parallel", "parallel")),
        name="cast_w_in",
    )(wt)


def _dense_kernel(*refs, act, has_res, kgrid):
    if has_res:
        xp_ref, xs_ref, w_ref, rp_ref, rs_ref, op_ref, os_ref = refs
    else:
        xp_ref, xs_ref, w_ref, op_ref, os_ref = refs
        rp_ref = rs_ref = None
    first_tile = pl.program_id(0) == 0
    wb = w_ref[...]
    mm = _dot

    def finish(acc, r_ref):
        if act == "relu2":
            acc = jnp.square(jnp.maximum(acc, 0.0))
        if r_ref is not None:
            acc = r_ref[...] + acc
        return acc

    if kgrid:
        @pl.when(pl.program_id(2) == 0)
        def _():
            op_ref[...] = rp_ref[...]
            os_ref[...] = rs_ref[...]

        op_ref[...] += mm(xp_ref[...], wb)

        @pl.when(first_tile)
        def _():
            os_ref[...] += mm(xs_ref[...], wb)
    else:
        op_ref[...] = finish(mm(xp_ref[...], wb), rp_ref).astype(op_ref.dtype)

        @pl.when(first_tile)
        def _():
            os_ref[...] = finish(mm(xs_ref[...], wb), rs_ref).astype(os_ref.dtype)

        @pl.when(jnp.logical_not(first_tile))
        def _():
            os_ref[...] = jnp.zeros(os_ref.shape, os_ref.dtype)


def _dense(xp, xs, w, *, layer, n, tn, tm=1024, tk=None, act=None, out_dtype=F32, resp=None, ress=None,
           name="dense"):
    mp, k = xp.shape
    ms = xs.shape[0]
    tm = _tile(mp, tm)
    kgrid = tk is not None
    has_res = resp is not None
    assert n % tn == 0 and (not kgrid or (has_res and act is None and k % tk == 0))
    if kgrid:
        grid = (mp // tm, n // tn, k // tk)
        row = lambda i, j, kk: (i, kk)
        srow = lambda i, j, kk: (0, kk)
        wix = lambda i, j, kk: (layer, kk, j)
        out = lambda i, j, kk: (i, j)
        sres = lambda i, j, kk: (0, j)
        sout = lambda i, j, kk: (i, j)
        kb = tk
        sem = ("arbitrary", "arbitrary", "arbitrary")
    else:
        grid = (mp // tm, n // tn)
        row = lambda i, j: (i, 0)
        srow = lambda i, j: (0, 0)
        wix = lambda i, j: (layer, 0, j)
        out = lambda i, j: (i, j)
        sres = lambda i, j: (0, j)
        sout = lambda i, j: (i, j)
        kb = k
        sem = ("arbitrary", "arbitrary")
    wblock = (None, kb, tn)
    x_mode = {} if kgrid else dict(pipeline_mode=pl.Buffered(1))
    in_specs = [pl.BlockSpec((tm, kb), row, **x_mode), pl.BlockSpec((ms, kb), srow, **x_mode),
                pl.BlockSpec(wblock, wix)]
    args = [xp, xs, w]
    if has_res:
        r_mode = dict(pipeline_mode=pl.Buffered(1)) if kgrid else {}
        in_specs += [pl.BlockSpec((tm, tn), out, **r_mode), pl.BlockSpec((ms, tn), sres, **r_mode)]
        args += [resp, ress]
    op, os2 = pl.pallas_call(
        functools.partial(_dense_kernel, act=act, has_res=has_res, kgrid=kgrid),
        grid=grid,
        in_specs=in_specs,
        out_specs=[pl.BlockSpec((tm, tn), out), pl.BlockSpec((ms, tn), sout)],
        out_shape=[jax.ShapeDtypeStruct((mp, n), out_dtype),
                   jax.ShapeDtypeStruct((mp // tm * ms, n), out_dtype)],
        compiler_params=_params(sem, VMEM_BIG),
        name=name,
    )(*args)
    return op, os2[:ms]


def _ret_kernel(q_ref, k_ref, v_ref, g_ref, cos_ref, sin_ref, dm_ref, qd_ref, kd_ref, cd_ref, gw_ref, gb_ref,
                r0_ref, o_ref, ro_ref, r_sc, *, rows):
    ci = pl.program_id(1)

    @pl.when(ci == 0)
    def _():
        r_sc[...] = r0_ref[0]

    cos = cos_ref[...]
    sin = sin_ref[...]
    half = RET_DK // 2

    def rot(x):
        x1, x2 = x[:, :half], x[:, half:]
        return jnp.concatenate([x1 * cos - x2 * sin, x2 * cos + x1 * sin], axis=-1)

    def pad(x):
        if rows == RET_CHUNK:
            return x
        return jnp.concatenate([x, jnp.zeros((RET_CHUNK - rows, x.shape[1]), x.dtype)], axis=0)

    for h in range(RET_HEADS):
        cols = slice(h * RET_DK, (h + 1) * RET_DK)
        q = pad(rot(q_ref[:, cols]))
        k = pad(rot(k_ref[:, cols]) * (RET_DK ** -0.5))
        vb = pad(v_ref[:, cols]).astype(BF16)
        r = r_sc[h]
        s = _dot_nt(q.astype(BF16), k.astype(BF16)) * dm_ref[h]
        o = _dot(s.astype(BF16), vb) + _dot((q * qd_ref[h]).astype(BF16), r.astype(BF16))
        r_new = cd_ref[h] * r + _dot_tn((k * kd_ref[h]).astype(BF16), vb)
        r_sc[h] = r_new
        ro_ref[0, h] = r_new
        o = o[:rows]
        mu = jnp.mean(o, axis=-1, keepdims=True)
        oc = o - mu
        var = jnp.mean(oc * oc, axis=-1, keepdims=True)
        y = oc * lax.rsqrt(var + LN_EPS) * gw_ref[:, cols] + gb_ref[:, cols]
        g = g_ref[:, cols]
        o_ref[:, cols] = (g * _sigmoid(g) * y).astype(o_ref.dtype)


def _ret_tables(c_eff):
    log_g = np.log1p(-np.exp2(-5.0 - np.arange(RET_HEADS, dtype=np.float64)))
    i = np.arange(RET_CHUNK, dtype=np.float64)
    live = i < c_eff
    diff = i[:, None] - i[None, :]
    dmask = np.where(diff >= 0, np.exp(np.maximum(diff, 0.0)[None] * log_g[:, None, None]), 0.0)
    dmask = dmask * (live[:, None] & live[None, :])[None]
    q_dec = np.exp((i[None, :] + 1.0) * log_g[:, None]) * live[None]
    k_dec = np.exp((c_eff - 1.0 - i)[None, :] * log_g[:, None]) * live[None]
    c_dec = np.exp(c_eff * log_g)
    bc = lambda a: np.broadcast_to(a[:, :, None], (RET_HEADS, RET_CHUNK, RET_DK))
    return (jnp.asarray(dmask, F32), jnp.asarray(bc(q_dec), F32), jnp.asarray(bc(k_dec), F32),
            jnp.asarray(np.broadcast_to(c_dec[:, None, None], (RET_HEADS, 1, RET_DK)), F32))


def _rope_tables(positions, rows):
    half = RET_DK // 2
    inv = ROPE_BASE ** (-np.arange(half, dtype=np.float64) / half)
    ang = np.asarray(positions, np.float64)[:, None] * inv[None, :]
    cos = np.zeros((rows, half)); sin = np.zeros((rows, half))
    cos[:len(positions)] = np.cos(ang); sin[:len(positions)] = np.sin(ang)
    return jnp.asarray(cos, F32), jnp.asarray(sin, F32)


def _retention(z, r0, gn_w, gn_b, *, nb, rows, n_chunks, c_eff, cos, sin):
    m = z.shape[0]
    dmask, q_dec, k_dec, c_dec = _ret_tables(c_eff)
    zspec = lambda off: pl.BlockSpec((rows, W_MIX), lambda b, c: (b * n_chunks + c, off // W_MIX))
    tab = pl.BlockSpec((rows, RET_DK // 2), lambda b, c: (c, 0))
    whole = lambda a: pl.BlockSpec(a.shape, lambda b, c: (0,) * a.ndim)
    state = pl.BlockSpec((1, RET_HEADS, RET_DK, RET_DV), lambda b, c: (b, 0, 0, 0))
    gw, gb = gn_w.reshape(1, W_MIX), gn_b.reshape(1, W_MIX)
    return pl.pallas_call(
        functools.partial(_ret_kernel, rows=rows),
        grid=(nb, n_chunks),
        in_specs=[zspec(A_RQ), zspec(A_RK), zspec(A_RV), zspec(A_RG), tab, tab,
                  whole(dmask), whole(q_dec), whole(k_dec), whole(c_dec), whole(gw), whole(gb), state],
        out_specs=[pl.BlockSpec((rows, W_MIX), lambda b, c: (b * n_chunks + c, 0)), state],
        out_shape=[jax.ShapeDtypeStruct((m, W_MIX), _act_dtype(rows)),
                   jax.ShapeDtypeStruct((nb, RET_HEADS, RET_DK, RET_DV), F32)],
        scratch_shapes=[pltpu.VMEM((RET_HEADS, RET_DK, RET_DV), F32)],
        compiler_params=_params(("parallel", "arbitrary")),
        name="retention",
    )(z, z, z, z, cos, sin, dmask, q_dec, k_dec, c_dec, gw, gb, r0)


def _gm_kernel(u_ref, v_ref, lw_ref, lb_ref, ws_ref, bst_ref, o_ref, *maybe_gv_ref, rows):
    u = _gelu(u_ref[...])
    v = _layernorm(_gelu(v_ref[...]), lw_ref[...], lb_ref[...])
    for gv_ref in maybe_gv_ref:
        gv_ref[...] = v
    if rows < GM_CHUNK:
        v = jnp.concatenate([v, jnp.zeros((GM_CHUNK - rows, v.shape[1]), F32)], axis=0)
    ri = lax.broadcasted_iota(jnp.int32, (GM_CHUNK, GM_CHUNK), 0)
    cj = lax.broadcasted_iota(jnp.int32, (GM_CHUNK, GM_CHUNK), 1)
    gw = W_MIX // GM_GROUPS
    for g in range(GM_GROUPS):
        wm = jnp.where(cj <= ri, ws_ref[g], 0.0).astype(BF16)
        s = _dot(wm, v[:, g * gw:(g + 1) * gw].astype(BF16)) + bst_ref[:, g:g + 1]
        o_ref[:, g * gw:(g + 1) * gw] = (u[:, g * gw:(g + 1) * gw] * s[:rows]).astype(o_ref.dtype)


def _gmlp(z, ln_w, ln_b, ws, bs, *, rows, keep_v):
    m = z.shape[0]
    row = lambda: pl.BlockSpec((1, W_MIX), lambda i: (0, 0))
    n_out = 2 if keep_v else 1
    return pl.pallas_call(
        functools.partial(_gm_kernel, rows=rows),
        grid=(m // rows,),
        in_specs=[pl.BlockSpec((rows, W_MIX), lambda i: (i, A_GU // W_MIX)),
                  pl.BlockSpec((rows, W_MIX), lambda i: (i, A_GV // W_MIX)),
                  row(), row(),
                  pl.BlockSpec((GM_GROUPS, GM_CHUNK, GM_CHUNK), lambda i: (0, 0, 0)),
                  pl.BlockSpec((GM_CHUNK, GM_GROUPS), lambda i: (0, 0))],
        out_specs=[pl.BlockSpec((rows, W_MIX), lambda i: (i, 0)), pl.BlockSpec((rows, W_MIX), lambda i: (i, 0))][:n_out],
        out_shape=[jax.ShapeDtypeStruct((m, W_MIX), _act_dtype(rows)), jax.ShapeDtypeStruct((m, W_MIX), F32)][:n_out],
        compiler_params=_params(("parallel",)),
        name="gmlp",
    )(z, z, ln_w.reshape(1, W_MIX), ln_b.reshape(1, W_MIX), ws, bs.T)


HALO = 32
CONV_RB = 64


def _conv_kernel(a_ref, b_ref, buf_ref, cw_ref, cb_ref, lw_ref, lb_ref, o_ref, ext_ref, ext_sc, y_sc, *, rows):
    ti = pl.program_id(1)

    @pl.when(ti == 0)
    def _():
        ext_sc[0:HALO, :] = buf_ref[0]

    @pl.when(ti > 0)
    def _():
        ext_sc[0:HALO, :] = ext_sc[rows:rows + HALO, :]

    ext_sc[HALO:HALO + rows, :] = a_ref[...] * _sigmoid(b_ref[...])
    rb = min(CONV_RB, rows)
    first = HALO - (CONV_W - 1)
    for cc in range(W_MIX // LANE):
        lanes = slice(cc * LANE, (cc + 1) * LANE)
        for r0 in range(0, rows, rb):
            acc = jnp.broadcast_to(cb_ref[:, lanes], (rb, LANE))
            for w in range(CONV_W):
                acc = acc + ext_sc[first + w + r0:first + w + r0 + rb, lanes] * cw_ref[w:w + 1, lanes]
            y_sc[r0:r0 + rb, lanes] = acc
    y = _layernorm(y_sc[...], lw_ref[...], lb_ref[...])
    o_ref[...] = (y * _sigmoid(y)).astype(o_ref.dtype)
    ext_ref[0] = ext_sc[...]


def _conv_module(z, buf, cw, cb, ln_w, ln_b, *, nb, rows, n_tiles):
    m = z.shape[0]
    row = lambda: pl.BlockSpec((1, W_MIX), lambda b, t: (0, 0))
    cwp = jnp.concatenate([cw, jnp.zeros((HALO - CONV_W, W_MIX), F32)], axis=0)
    return pl.pallas_call(
        functools.partial(_conv_kernel, rows=rows),
        grid=(nb, n_tiles),
        in_specs=[pl.BlockSpec((rows, W_MIX), lambda b, t: (b * n_tiles + t, B_CA // W_MIX)),
                  pl.BlockSpec((rows, W_MIX), lambda b, t: (b * n_tiles + t, B_CB // W_MIX)),
                  pl.BlockSpec((1, HALO, W_MIX), lambda b, t: (b, 0, 0)),
                  pl.BlockSpec((HALO, W_MIX), lambda b, t: (0, 0)),
                  row(), row(), row()],
        out_specs=[pl.BlockSpec((rows, W_MIX), lambda b, t: (b * n_tiles + t, 0)),
                   pl.BlockSpec((1, HALO + rows, W_MIX), lambda b, t: (b, 0, 0))],
        out_shape=[jax.ShapeDtypeStruct((m, W_MIX), _act_dtype(rows)),
                   jax.ShapeDtypeStruct((nb, HALO + rows, W_MIX), F32)],
        scratch_shapes=[pltpu.VMEM((HALO + rows, W_MIX), F32), pltpu.VMEM((rows, W_MIX), F32)],
        compiler_params=_params(("parallel", "arbitrary")),
        name="conv_module",
    )(z, z, buf, cwp, cb.reshape(1, W_MIX), ln_w.reshape(1, W_MIX), ln_b.reshape(1, W_MIX))


def _overlap_t(n_c_pad, n_s_pad, n_c, n_s):
    cs = np.arange(n_c_pad)[:, None] * CMP_STRIDE
    ss = np.arange(n_s_pad)[None, :] * SEL_LEN
    ov = np.clip(np.minimum(ss + SEL_LEN, cs + CMP_LEN) - np.maximum(ss, cs), 0, None).astype(np.float64)
    ov = ov * (np.arange(n_c_pad)[:, None] < n_c) * (np.arange(n_s_pad)[None, :] < n_s)
    return jnp.asarray(ov, BF16)


def _cmp_p_kernel(x_ref, pe_ref, w1_ref, w2_ref, o_ref, xs_sc, *, t, ncp):
    xs_sc[0:t, :] = x_ref[...]
    xs_sc[t:, :] = jnp.zeros((xs_sc.shape[0] - t, NSA_HD), F32)
    acc = jnp.zeros((ncp, NSA_HD), F32)
    for l in range(CMP_LEN):
        rows = xs_sc[pl.ds(l, ncp, stride=CMP_STRIDE), :] + pe_ref[0, l:l + 1, :]
        acc = acc + _dot(rows.astype(BF16), w1_ref[0, l * NSA_HD:(l + 1) * NSA_HD, :])
    o_ref[0, 0] = _dot(_gelu(acc).astype(BF16), w2_ref[0])


def _compress_prompt(z, pe, w1, w2, *, nb, t):
    ncp = _round_up(t // CMP_STRIDE, LANE)
    pad_rows = _round_up(CMP_STRIDE * (ncp - 1) + CMP_LEN, SUBLANE)
    return pl.pallas_call(
        functools.partial(_cmp_p_kernel, t=t, ncp=ncp),
        grid=(nb, 4),
        in_specs=[pl.BlockSpec((t, NSA_HD), lambda b, j: (b, A_NKV // NSA_HD + j)),
                  pl.BlockSpec((1, CMP_LEN, NSA_HD), lambda b, j: (j // 2, 0, 0)),
                  pl.BlockSpec((1, CMP_LEN * NSA_HD, NSA_HD), lambda b, j: (j // 2, 0, 0)),
                  pl.BlockSpec((1, NSA_HD, NSA_HD), lambda b, j: (j // 2, 0, 0))],
        out_specs=pl.BlockSpec((1, 1, ncp, NSA_HD), lambda b, j: (b, j, 0, 0)),
        out_shape=jax.ShapeDtypeStruct((nb, 4, ncp, NSA_HD), F32),
        scratch_shapes=[pltpu.VMEM((max(pad_rows, t + SUBLANE), NSA_HD), F32)],
        compiler_params=_params(("parallel", "parallel")),
        name="nsa_compress_prompt",
    )(z, pe, w1, w2)


def _nsa_p_kernel(q_ref, kc_ref, vc_ref, ks_ref, vs_ref, kw_ref, vw_ref, ng_ref, ov_ref, ex_ref, o_ref, os_sc,
                  *, t, tq, n_c, n_s, span, key_limits):
    qi = pl.program_id(2)
    kv = pl.program_id(1)
    scale = NSA_HD ** -0.5
    g_n = NSA_GROUP
    q = q_ref[...]
    qh = [q[:, g * NSA_HD:(g + 1) * NSA_HD].astype(BF16) for g in range(g_n)]
    q4 = jnp.concatenate(qh, axis=0)
    qpos = qi * tq + lax.broadcasted_iota(jnp.int32, (tq, 1), 0)
    qpos4 = jnp.concatenate([qpos] * g_n, axis=0)

    kc = kc_ref[0, 0].astype(BF16)
    ncp = kc.shape[0]
    s_c = _dot_nt(q4, kc) * scale
    cidx = lax.broadcasted_iota(jnp.int32, (1, ncp), 1)
    mask_c = (cidx * CMP_STRIDE + CMP_LEN - 1 <= qpos4) & (cidx < n_c)
    p_c = _softmax_lanes(s_c, mask_c)
    o_c = _dot(p_c.astype(BF16), vc_ref[0, 0].astype(BF16))

    p_sum = p_c[0:tq]
    for g in range(1, g_n):
        p_sum = p_sum + p_c[g * tq:(g + 1) * tq]
    imp = _dot_split3(p_sum, ov_ref[...])
    nsp = imp.shape[1]
    blk = lax.broadcasted_iota(jnp.int32, (tq, nsp), 1)
    cur = jnp.right_shift(qpos, SEL_SHIFT)
    forced = (blk == 0) | (blk == cur) | (blk == cur - 1)
    valid = (blk * SEL_LEN <= qpos) & (blk < n_s)
    score = jnp.where(valid, jnp.where(forced, jnp.inf, imp), -jnp.inf)
    rank = jnp.zeros((tq, nsp), F32)
    for j in range(n_s):
        sj = score[:, j:j + 1]
        ahead = (sj > score) | ((sj == score) & (blk > j))
        rank = rank + jnp.where(ahead, 1.0, 0.0)
    sel = valid & (rank < float(min(SEL_TOPK, n_s)))

    sel_b = jnp.where(sel, 1.0, 0.0).astype(BF16)
    for lo, hi, n_keys in key_limits:
        @pl.when((qi >= lo) & (qi < hi))
        def _(n_keys=n_keys):
            kidx = lax.broadcasted_iota(jnp.int32, (1, n_keys), 1)
            mask_s = (_dot(sel_b, ex_ref[:, 0:n_keys]) > 0.5) & (kidx <= qpos)
            kb = ks_ref[0:n_keys, :].astype(BF16)
            vb = vs_ref[0:n_keys, :].astype(BF16)
            for g in range(g_n):
                e, inv = _softmax_parts(_dot_nt(qh[g], kb) * scale, mask_s)
                os_sc[g * tq:(g + 1) * tq, :] = _dot(e.astype(BF16), vb) * inv

    o_s = os_sc[...]

    w0 = pl.multiple_of(jnp.clip(qi * tq - WINDOW, 0, t - span), LANE)
    kw = kw_ref[pl.ds(w0, span), :].astype(BF16)
    vw = vw_ref[pl.ds(w0, span), :].astype(BF16)
    kpos = w0 + lax.broadcasted_iota(jnp.int32, (1, span), 1)
    dist = qpos4 - kpos
    mask_w = (dist >= 0) & (dist <= WINDOW)
    e_w, inv_w = _softmax_parts(_dot_nt(q4, kw) * scale, mask_w)
    o_w = _dot(e_w.astype(BF16), vw) * inv_w

    gates = _sigmoid(ng_ref[...])
    for g in range(g_n):
        rows = slice(g * tq, (g + 1) * tq)
        col = (kv * g_n + g) * 3
        lane = lax.broadcasted_iota(jnp.int32, gates.shape, 1)
        pick = lambda j: jnp.sum(jnp.where(lane == col + j, gates, 0.0), axis=-1, keepdims=True)
        o = pick(0) * o_c[rows] + pick(1) * o_s[rows] + pick(2) * o_w[rows]
        o_ref[:, g * NSA_HD:(g + 1) * NSA_HD] = o.astype(o_ref.dtype)


KEY_PREFIX_VARIANTS = 4


def _nsa_prompt(z, zc, cmp_kv, *, nb, t):
    m = z.shape[0]
    tq = 128
    nq = t // tq
    ends = sorted({-(-v * nq // KEY_PREFIX_VARIANTS) for v in range(1, KEY_PREFIX_VARIANTS + 1)})
    key_limits = tuple((lo, hi, hi * tq) for lo, hi in zip([0] + ends[:-1], ends))
    n_c = (t - CMP_LEN) // CMP_STRIDE + 1
    n_s = -(-t // SEL_LEN)
    ncp = cmp_kv.shape[2]
    nsp = LANE
    assert n_s <= nsp
    span = min(WINDOW + tq, t)
    ov = _overlap_t(ncp, nsp, n_c, n_s)
    ex = jnp.asarray((np.arange(t)[None, :] // SEL_LEN) == np.arange(nsp)[:, None], BF16)
    kvw = NSA_GROUP * NSA_HD
    nk = A_NKV // NSA_HD
    full = lambda j0: pl.BlockSpec((t, NSA_HD), lambda b, kv, qi: (b, nk + j0 + kv))
    return pl.pallas_call(
        functools.partial(_nsa_p_kernel, t=t, tq=tq, n_c=n_c, n_s=n_s, span=span, key_limits=key_limits),
        grid=(nb, NSA_KV, nq),
        in_specs=[pl.BlockSpec((tq, kvw), lambda b, kv, qi: (b * nq + qi, A_NQ // kvw + kv)),
                  pl.BlockSpec((1, 1, ncp, NSA_HD), lambda b, kv, qi: (b, kv, 0, 0)),
                  pl.BlockSpec((1, 1, ncp, NSA_HD), lambda b, kv, qi: (b, 2 + kv, 0, 0)),
                  full(4), full(6), full(8), full(10),
                  pl.BlockSpec((tq, LANE), lambda b, kv, qi: (b * nq + qi, C_NG // LANE)),
                  pl.BlockSpec((ncp, nsp), lambda b, kv, qi: (0, 0)),
                  pl.BlockSpec((nsp, t), lambda b, kv, qi: (0, 0))],
        out_specs=pl.BlockSpec((tq, kvw), lambda b, kv, qi: (b * nq + qi, kv)),
        out_shape=jax.ShapeDtypeStruct((m, W_MIX), BF16),
        scratch_shapes=[pltpu.VMEM((NSA_GROUP * tq, NSA_HD), F32)],
        compiler_params=_params(("parallel", "parallel", "arbitrary"), VMEM_BIG),
        name="nsa_prompt",
    )(z, cmp_kv, cmp_kv, z, z, z, z, zc, ov, ex)


PAGES_PER_STEP = 8
PAGE_ROWS = PAGE_SIZE * 4 * NSA_KV
GROUPS_PER_PAGE = PAGE_SIZE // CMP_STRIDE


def _s_cmp_kernel(pt_ref, *refs):
    del pt_ref
    pages = refs[:PAGES_PER_STEP]
    w_ref, o_ref = refs[PAGES_PER_STEP], refs[PAGES_PER_STEP + 1]
    gp = PAGES_PER_STEP * GROUPS_PER_PAGE
    for which in range(2):
        acc = jnp.zeros((NSA_KV * gp, 2 * NSA_HD), F32)
        for l in range(CMP_STRIDE):
            pieces = [pages[p][pl.ds(l * 4 * NSA_KV + which * NSA_KV + kv, GROUPS_PER_PAGE, stride=CMP_STRIDE * 4 * NSA_KV), :]
                      for kv in range(NSA_KV) for p in range(PAGES_PER_STEP)]
            acc = acc + _dot(jnp.concatenate(pieces, axis=0).astype(BF16), w_ref[which, l])
        for kv in range(NSA_KV):
            o_ref[0, which, kv] = acc[kv * gp:(kv + 1) * gp]


def _page_specs(layer, n):
    def spec(p):
        return pl.BlockSpec((None, None, PAGE_ROWS, NSA_HD),
                            lambda b, s, pt: (layer, pt[b, s * PAGES_PER_STEP + p], 0, 0))
    return [spec(p) for p in range(n)]


def _compress_sample_partial(cache4, page_table, w1ab, *, layer, nb, n_pages):
    steps = n_pages // PAGES_PER_STEP
    gp = PAGES_PER_STEP * GROUPS_PER_PAGE
    ng = n_pages * GROUPS_PER_PAGE
    return pl.pallas_call(
        _s_cmp_kernel,
        grid_spec=pltpu.PrefetchScalarGridSpec(
            num_scalar_prefetch=1,
            grid=(nb, steps),
            in_specs=_page_specs(layer, PAGES_PER_STEP)
            + [pl.BlockSpec((2, CMP_STRIDE, NSA_HD, 2 * NSA_HD), lambda b, s, pt: (0, 0, 0, 0))],
            out_specs=pl.BlockSpec((1, 2, NSA_KV, gp, 2 * NSA_HD), lambda b, s, pt: (b, 0, 0, s, 0)),
        ),
        out_shape=jax.ShapeDtypeStruct((nb, 2, NSA_KV, ng, 2 * NSA_HD), F32),
        compiler_params=_params(("parallel", "arbitrary"), VMEM_BIG),
        name="nsa_compress_sample",
    )(page_table, *([cache4] * PAGES_PER_STEP), w1ab)


def _s_sel_kernel(uv_ref, pef_ref, w1_ref, w2_ref, q_ref, ng_ref, kwn_ref, vwn_ref, kwp_ref, vwp_ref, ov_ref,
                  selt_ref, op_ref, *, q0, valid_rows, n_c, n_s):
    kv = pl.program_id(1)
    scale = NSA_HD ** -0.5
    g_n = NSA_GROUP
    rows = SLAB
    q = q_ref[...]
    q4 = jnp.concatenate([q[:, g * NSA_HD:(g + 1) * NSA_HD] for g in range(g_n)], axis=0).astype(BF16)
    tpos = lax.broadcasted_iota(jnp.int32, (rows, 1), 0)
    qpos = q0 + tpos
    qpos4 = jnp.concatenate([qpos] * g_n, axis=0)

    def compressed(which):
        uv = uv_ref[0, which, 0]
        ng = uv.shape[0]
        u = uv[:, :NSA_HD]
        v_next = pltpu.roll(uv[:, NSA_HD:], ng - 1, 0)
        const = _dot(pef_ref[which], w1_ref[which])[0:1]
        return _dot(_gelu(u + v_next + const).astype(BF16), w2_ref[which])

    kc = compressed(0)
    vc = compressed(1)
    ncp = kc.shape[0]
    s_c = _dot_nt(q4, kc.astype(BF16)) * scale
    cidx = lax.broadcasted_iota(jnp.int32, (1, ncp), 1)
    mask_c = (cidx * CMP_STRIDE + CMP_LEN - 1 <= qpos4) & (cidx < n_c)
    p_c = _softmax_lanes(s_c, mask_c)
    o_c = _dot(p_c.astype(BF16), vc.astype(BF16))

    p_sum = p_c[0:rows]
    for g in range(1, g_n):
        p_sum = p_sum + p_c[g * rows:(g + 1) * rows]
    imp = _dot_split3(p_sum, ov_ref[...])
    nsp = imp.shape[1]
    blk = lax.broadcasted_iota(jnp.int32, (rows, nsp), 1)
    cur = jnp.right_shift(qpos, SEL_SHIFT)
    forced = (blk == 0) | (blk == cur) | (blk == cur - 1)
    valid = (blk * SEL_LEN <= qpos) & (blk < n_s)
    score = jnp.where(valid, jnp.where(forced, jnp.inf, imp), -jnp.inf)
    score_pad = jnp.concatenate([score, jnp.zeros((LANE - rows, nsp), F32)], axis=0)
    score_t = jnp.transpose(score_pad)
    bi = lax.broadcasted_iota(jnp.int32, (nsp, nsp), 0)
    bj = lax.broadcasted_iota(jnp.int32, (nsp, nsp), 1)
    lane = lax.broadcasted_iota(jnp.int32, (nsp, LANE), 1)
    sel_t = jnp.zeros((nsp, LANE), F32)
    k_take = float(min(SEL_TOPK, n_s))
    for tt in range(valid_rows):
        s_i = score_t[:, tt:tt + 1]
        s_j = score[tt:tt + 1, :]
        ahead = (s_j > s_i) | ((s_j == s_i) & (bj < bi))
        rank_i = jnp.sum(jnp.where(ahead, 1.0, 0.0), axis=-1, keepdims=True)
        ok_i = (rank_i < k_take) & (s_i > -jnp.inf)
        sel_t = jnp.where(((lane & (rows - 1)) == tt) & (lane < g_n * rows) & ok_i, 1.0, sel_t)
    selt_ref[0, 0] = sel_t

    kw = jnp.concatenate([kwp_ref[...], kwn_ref[...]], axis=0).astype(BF16)
    vw = jnp.concatenate([vwp_ref[...], vwn_ref[...]], axis=0).astype(BF16)
    n_before = kwp_ref.shape[0]
    widx = lax.broadcasted_iota(jnp.int32, (1, n_before + rows), 1)
    kpos = jnp.where(widx < n_before, q0 - n_before + widx, q0 + widx - n_before)
    dist = qpos4 - kpos
    mask_w = (dist >= 0) & (dist <= WINDOW) & (kpos >= 0)
    s_w = _dot_nt(q4, kw) * scale
    p_w = _softmax_lanes(s_w, mask_w)
    o_w = _dot(p_w.astype(BF16), vw)

    gates = _sigmoid(ng_ref[...])
    glane = lax.broadcasted_iota(jnp.int32, gates.shape, 1)
    for g in range(g_n):
        r = slice(g * rows, (g + 1) * rows)
        col = (kv * g_n + g) * 3
        pick = lambda j: jnp.sum(jnp.where(glane == col + j, gates, 0.0), axis=-1, keepdims=True)
        op_ref[0, 0, r, :] = pick(0) * o_c[r] + pick(2) * o_w[r]


def _s_attn_kernel(pt_ref, *refs, n_steps, valid_rows):
    del pt_ref
    pages = refs[:PAGES_PER_STEP]
    q_ref, kvn_ref, ng_ref, selt_ref, op_ref, o_ref, m_sc, l_sc, acc_sc = refs[PAGES_PER_STEP:]
    s_id = pl.program_id(1)
    scale = NSA_HD ** -0.5
    g_n = NSA_GROUP
    rows = SLAB
    blocks_per_step = PAGES_PER_STEP * PAGE_SIZE // SEL_LEN

    @pl.when(s_id == 0)
    def _():
        m_sc[...] = jnp.full(m_sc.shape, NEG, F32)
        l_sc[...] = jnp.zeros(l_sc.shape, F32)
        acc_sc[...] = jnp.zeros(acc_sc.shape, F32)

    q = q_ref[...]

    def q_rows(kv):
        q4 = jnp.concatenate([q[:, (kv * g_n + g) * NSA_HD:(kv * g_n + g + 1) * NSA_HD] for g in range(g_n)], axis=0)
        return jnp.concatenate([q4, jnp.zeros((LANE - g_n * rows, NSA_HD), F32)], axis=0).astype(BF16)

    def update(kv, s_t, mask_t, v):
        m_old = m_sc[kv]
        m_new = jnp.maximum(m_old, jnp.max(jnp.where(mask_t, s_t, NEG), axis=0, keepdims=True))
        alpha = jnp.exp(m_old - m_new)
        p_t = jnp.where(mask_t, jnp.exp(jnp.minimum(s_t - m_new, 0.0)), 0.0)
        l_sc[kv] = alpha * l_sc[kv] + jnp.sum(p_t, axis=0, keepdims=True)
        acc_sc[kv] = alpha * acc_sc[kv] + _dot_tn(v, p_t.astype(BF16))
        m_sc[kv] = m_new

    stride = 4 * NSA_KV
    for kv in range(NSA_KV):
        qk = q_rows(kv)
        k = jnp.concatenate([pg[pl.ds(2 * NSA_KV + kv, PAGE_SIZE, stride=stride), :] for pg in pages], axis=0)
        v = jnp.concatenate([pg[pl.ds(3 * NSA_KV + kv, PAGE_SIZE, stride=stride), :] for pg in pages], axis=0)
        s_t = _dot_nt(k.astype(BF16), qk) * scale
        start = pl.multiple_of(s_id * blocks_per_step, blocks_per_step)
        chunk = selt_ref[0, kv, pl.ds(start, blocks_per_step), :]
        mask_t = jnp.concatenate([jnp.broadcast_to(chunk[c:c + 1, :], (SEL_LEN, LANE))
                                  for c in range(blocks_per_step)], axis=0) > 0.5
        update(kv, s_t, mask_t, v.astype(BF16))

    @pl.when(s_id == n_steps - 1)
    def _():
        n_past_blocks = n_steps * blocks_per_step
        kvn = jnp.concatenate([kvn_ref[...], jnp.zeros((LANE - rows, 4 * NSA_HD), F32)], axis=0)
        gates = _sigmoid(ng_ref[...])
        glane = lax.broadcasted_iota(jnp.int32, gates.shape, 1)
        for kv in range(NSA_KV):
            k_new = kvn[:, kv * NSA_HD:(kv + 1) * NSA_HD].astype(BF16)
            v_new = kvn[:, (NSA_KV + kv) * NSA_HD:(NSA_KV + kv + 1) * NSA_HD].astype(BF16)
            s_t = _dot_nt(k_new, q_rows(kv)) * scale
            key = lax.broadcasted_iota(jnp.int32, (LANE, LANE), 0)
            tok = lax.broadcasted_iota(jnp.int32, (LANE, LANE), 1) & (rows - 1)
            blk_ok = selt_ref[0, kv, n_past_blocks:n_past_blocks + 1, :] > 0.5
            mask_t = (key <= tok) & (key < valid_rows) & blk_ok
            update(kv, s_t, mask_t, v_new)
            l = l_sc[kv]
            o_t = acc_sc[kv] / jnp.where(l > 0.0, l, 1.0)
            o_sel = jnp.transpose(o_t)
            for g in range(g_n):
                r = slice(g * rows, (g + 1) * rows)
                col = (kv * g_n + g) * 3 + 1
                gate = jnp.sum(jnp.where(glane == col, gates, 0.0), axis=-1, keepdims=True)
                o = op_ref[0, kv, r, :] + gate * o_sel[r]
                o_ref[:, (kv * g_n + g) * NSA_HD:(kv * g_n + g + 1) * NSA_HD] = o.astype(o_ref.dtype)


def _nsa_sample(zs, zc, cache4, page_table, win_past, pe, w1, w2, *, layer, nb, valid_rows, q0):
    n_pages = page_table.shape[1]
    assert n_pages % PAGES_PER_STEP == 0 and q0 == n_pages * PAGE_SIZE
    s_len = q0 + valid_rows
    n_c = (s_len - CMP_LEN) // CMP_STRIDE + 1
    n_s = -(-s_len // SEL_LEN)
    ng = n_pages * GROUPS_PER_PAGE
    assert n_c <= ng - 1
    nsp = _round_up(n_s, LANE)
    steps = n_pages // PAGES_PER_STEP
    w1b = w1.astype(BF16)
    w1r = w1b.reshape(2, CMP_LEN, NSA_HD, NSA_HD)
    w1ab = jnp.concatenate([w1r[:, :CMP_STRIDE], w1r[:, CMP_STRIDE:]], axis=-1)
    uv = _compress_sample_partial(cache4, page_table, w1ab, layer=layer, nb=nb, n_pages=n_pages)
    pef = jnp.broadcast_to(pe.reshape(2, 1, CMP_LEN * NSA_HD), (2, SUBLANE, CMP_LEN * NSA_HD)).astype(BF16)
    ov = _overlap_t(ng, nsp, n_c, n_s)
    kvw = NSA_GROUP * NSA_HD
    nk = A_NKV // NSA_HD
    n_before = win_past.shape[4]
    selt, o_part = pl.pallas_call(
        functools.partial(_s_sel_kernel, q0=q0, valid_rows=valid_rows, n_c=n_c, n_s=n_s),
        grid=(nb, NSA_KV),
        in_specs=[pl.BlockSpec((1, 2, 1, ng, 2 * NSA_HD), lambda b, kv: (b, 0, kv, 0, 0)),
                  pl.BlockSpec((2, SUBLANE, CMP_LEN * NSA_HD), lambda b, kv: (0, 0, 0)),
                  pl.BlockSpec((2, CMP_LEN * NSA_HD, NSA_HD), lambda b, kv: (0, 0, 0)),
                  pl.BlockSpec((2, NSA_HD, NSA_HD), lambda b, kv: (0, 0, 0)),
                  pl.BlockSpec((SLAB, kvw), lambda b, kv: (b, A_NQ // kvw + kv)),
                  pl.BlockSpec((SLAB, LANE), lambda b, kv: (b, C_NG // LANE)),
                  pl.BlockSpec((SLAB, NSA_HD), lambda b, kv: (b, nk + 8 + kv)),
                  pl.BlockSpec((SLAB, NSA_HD), lambda b, kv: (b, nk + 10 + kv)),
                  pl.BlockSpec((None, None, None, None, n_before, NSA_HD), lambda b, kv: (layer, b, 0, kv, 0, 0)),
                  pl.BlockSpec((None, None, None, None, n_before, NSA_HD), lambda b, kv: (layer, b, 1, kv, 0, 0)),
                  pl.BlockSpec((ng, nsp), lambda b, kv: (0, 0))],
        out_specs=[pl.BlockSpec((1, 1, nsp, LANE), lambda b, kv: (b, kv, 0, 0)),
                   pl.BlockSpec((1, 1, NSA_GROUP * SLAB, NSA_HD), lambda b, kv: (b, kv, 0, 0))],
        out_shape=[jax.ShapeDtypeStruct((nb, NSA_KV, nsp, LANE), F32),
                   jax.ShapeDtypeStruct((nb, NSA_KV, NSA_GROUP * SLAB, NSA_HD), F32)],
        compiler_params=_params(("parallel", "parallel"), VMEM_BIG),
        name="nsa_select_sample",
    )(uv, pef, w1b, w2.astype(BF16), zs, zc, zs, zs, win_past, win_past, ov)
    return pl.pallas_call(
        functools.partial(_s_attn_kernel, n_steps=steps, valid_rows=valid_rows),
        grid_spec=pltpu.PrefetchScalarGridSpec(
            num_scalar_prefetch=1,
            grid=(nb, steps),
            in_specs=_page_specs(layer, PAGES_PER_STEP)
            + [pl.BlockSpec((SLAB, NSA_HEADS * NSA_HD), lambda b, s, pt: (b, A_NQ // (NSA_HEADS * NSA_HD))),
               pl.BlockSpec((SLAB, 4 * NSA_HD), lambda b, s, pt: (b, (A_NKV + 4 * NSA_HD) // (4 * NSA_HD))),
               pl.BlockSpec((SLAB, LANE), lambda b, s, pt: (b, C_NG // LANE)),
               pl.BlockSpec((1, NSA_KV, nsp, LANE), lambda b, s, pt: (b, 0, 0, 0)),
               pl.BlockSpec((1, NSA_KV, NSA_GROUP * SLAB, NSA_HD), lambda b, s, pt: (b, 0, 0, 0))],
            out_specs=pl.BlockSpec((SLAB, W_MIX), lambda b, s, pt: (b, 0)),
            scratch_shapes=[pltpu.VMEM((NSA_KV, 1, LANE), F32), pltpu.VMEM((NSA_KV, 1, LANE), F32),
                            pltpu.VMEM((NSA_KV, NSA_HD, LANE), F32)],
        ),
        out_shape=jax.ShapeDtypeStruct((nb * SLAB, W_MIX), _act_dtype(SLAB)),
        compiler_params=_params(("parallel", "arbitrary"), VMEM_BIG),
        name="nsa_attend_sample",
    )(page_table, *([cache4] * PAGES_PER_STEP), zs, zs, zc, selt, o_part)


def _merge_kernel(a_ref, b_ref, c_ref, d_ref, w_ref, g0_ref, g1_ref, g2_ref, g3_ref, o_ref):
    acc = None
    for n, (x_ref, g_ref) in enumerate(((a_ref, g0_ref), (b_ref, g1_ref), (c_ref, g2_ref), (d_ref, g3_ref))):
        term = _sigmoid(g_ref[...]) * _dot(x_ref[...].astype(BF16), w_ref[n])
        acc = term if acc is None else acc + term
    o_ref[...] = acc.astype(o_ref.dtype)


def _merge(branches, w_branch, zb, *, layer):
    m = zb.shape[0]
    tm = _tile(m, 512)
    tn = 1024
    br = pl.BlockSpec((tm, W_MIX), lambda i, j: (i, 0))
    gz = lambda n: pl.BlockSpec((tm, tn), lambda i, j: (i, (B_GZ + n * D_MODEL) // tn + j))
    return pl.pallas_call(
        _merge_kernel,
        grid=(m // tm, D_MODEL // tn),
        in_specs=[br, br, br, br, pl.BlockSpec((None, N_BRANCH, W_MIX, tn), lambda i, j: (layer, 0, 0, j)),
                  gz(0), gz(1), gz(2), gz(3)],
        out_specs=pl.BlockSpec((tm, tn), lambda i, j: (i, j)),
        out_shape=jax.ShapeDtypeStruct((m, D_MODEL), BF16),
        compiler_params=_params(("parallel", "parallel"), VMEM_BIG),
        name="branch_merge",
    )(*branches, w_branch, zb, zb, zb, zb)


def kernel(x_prompt, x_sample, cache_nsa_kv, state_ret, state_win_kv, state_conv, page_table, norm1, w_in, ret_gn_w,
           ret_gn_b, gm_ln_w, gm_ln_b, gm_ws, gm_bs, nsa_pe, nsa_w1, nsa_w2, conv_w, conv_b, conv_ln_w, conv_ln_b,
           w_branch, w_out, norm2, w_up, w_down, final_norm):
    bp, t, d = x_prompt.shape
    bs, ts, _ = x_sample.shape
    assert d == D_MODEL and t % RET_CHUNK == 0 and ts <= SLAB and norm1.shape[0] == DEPTH
    n_pages = page_table.shape[1]
    past_len = n_pages * PAGE_SIZE
    conv_rows = 128

    w_in_t = _cast_w_in(w_in)
    w_branch_b = _cast_bf16(w_branch.reshape(DEPTH, N_BRANCH * W_MIX, d), rows=512).reshape(w_branch.shape)
    w_out_b = _cast_bf16(w_out, rows=512)
    w_up_b = _cast_bf16(w_up, rows=128)
    w_down_b = _cast_bf16(w_down, rows=512)
    nsa_w1_b = nsa_w1.astype(BF16)
    nsa_w2_b = nsa_w2.astype(BF16)

    xp = x_prompt.reshape(bp * t, d)
    xs = jnp.pad(x_sample, ((0, 0), (0, SLAB - ts), (0, 0))).reshape(bs * SLAB, d)
    cache4 = cache_nsa_kv.reshape(DEPTH, cache_nsa_kv.shape[1], PAGE_ROWS, NSA_HD)
    win_t = jnp.transpose(state_win_kv, (0, 1, 3, 4, 2, 5))
    conv_pad = jnp.pad(state_conv, ((0, 0), (0, 0), (HALO - (CONV_W - 1), 0), (0, 0)))
    ret0_p = jnp.zeros((bp, RET_HEADS, RET_DK, RET_DV), F32)
    conv0_p = jnp.zeros((bp, HALO, W_MIX), F32)
    cos_p, sin_p = _rope_tables(np.arange(t), t)
    cos_s, sin_s = _rope_tables(past_len + np.arange(ts), SLAB)
    c_eff_s = math.gcd(ts, RET_CHUNK)
    assert c_eff_s == ts

    keep_p = min(WINDOW, t)
    keep_s = min(WINDOW, past_len + ts)
    ret_p, ret_s, kv_p, kv_s, win_p, win_s, conv_p, conv_s, gm_s = ([] for _ in range(9))
    for l in range(DEPTH):
        hp = _rmsnorm(xp, norm1[l], slab=1, valid=1, out_dtype=BF16)
        hs = _rmsnorm(xs, norm1[l], slab=SLAB, valid=ts, out_dtype=BF16)
        zap, zas = _dense(hp, hs, w_in_t, layer=l, n=N_Z, tn=1024, name="in_proj")
        zbp, zbs, zcp, zcs = zap, zas, zap, zas

        o_ret_p, r_p = _retention(zap, ret0_p, ret_gn_w[l], ret_gn_b[l], nb=bp, rows=RET_CHUNK,
                                  n_chunks=t // RET_CHUNK, c_eff=RET_CHUNK, cos=cos_p, sin=sin_p)
        (o_gm_p,) = _gmlp(zap, gm_ln_w[l], gm_ln_b[l], gm_ws[l], gm_bs[l], rows=GM_CHUNK, keep_v=False)
        cmp_kv = _compress_prompt(zap, nsa_pe[l], nsa_w1_b[l], nsa_w2_b[l], nb=bp, t=t)
        o_nsa_p = _nsa_prompt(zap, zcp, cmp_kv, nb=bp, t=t)
        o_conv_p, ext_p = _conv_module(zbp, conv0_p, conv_w[l], conv_b[l], conv_ln_w[l], conv_ln_b[l], nb=bp,
                                       rows=conv_rows, n_tiles=t // conv_rows)
        merged_p = _merge((o_ret_p, o_gm_p, o_nsa_p, o_conv_p), w_branch_b, zbp, layer=l)

        o_ret_s, r_s = _retention(zas, state_ret[l], ret_gn_w[l], ret_gn_b[l], nb=bs, rows=SLAB, n_chunks=1,
                                  c_eff=c_eff_s, cos=cos_s, sin=sin_s)
        o_gm_s, gv_s = _gmlp(zas, gm_ln_w[l], gm_ln_b[l], gm_ws[l], gm_bs[l], rows=SLAB, keep_v=True)
        o_nsa_s = _nsa_sample(zas, zcs, cache4, page_table, win_t, nsa_pe[l], nsa_w1[l], nsa_w2[l], layer=l, nb=bs,
                              valid_rows=ts, q0=past_len)
        o_conv_s, ext_s = _conv_module(zbs, conv_pad[l], conv_w[l], conv_b[l], conv_ln_w[l], conv_ln_b[l], nb=bs,
                                       rows=SLAB, n_tiles=1)
        merged_s = _merge((o_ret_s, o_gm_s, o_nsa_s, o_conv_s), w_branch_b, zbs, layer=l)

        xp, xs = _dense(merged_p, merged_s, w_out_b, layer=l, n=D_MODEL, tn=512, resp=xp, ress=xs, name="out_proj")
        h2p = _rmsnorm(xp, norm2[l], slab=1, valid=1, out_dtype=BF16)
        h2s = _rmsnorm(xs, norm2[l], slab=1, valid=1, out_dtype=BF16)
        up, us = _dense(h2p, h2s, w_up_b, layer=l, n=D_FF, tn=1024, act="relu2", out_dtype=BF16, name="mlp_up")
        xp, xs = _dense(up, us, w_down_b, layer=l, n=D_MODEL, tn=1024, tk=2048, resp=xp, ress=xs, name="mlp_down")

        ret_p.append(r_p)
        ret_s.append(r_s)
        z3p = zap.reshape(bp, t, N_Z)
        kv_p.append(z3p[:, :, A_NKV:A_NKV + 8 * NSA_HD].reshape(bp, t, 4, NSA_KV, NSA_HD))
        win_p.append(z3p[:, t - keep_p:, A_NKV + 8 * NSA_HD:A_NKV + 12 * NSA_HD].reshape(bp, keep_p, 2, NSA_KV, NSA_HD))
        z3s = zas.reshape(bs, SLAB, N_Z)[:, :ts]
        kv_s.append(z3s[:, :, A_NKV:A_NKV + 8 * NSA_HD].reshape(bs, ts, 4, NSA_KV, NSA_HD))
        new_win = z3s[:, :, A_NKV + 8 * NSA_HD:A_NKV + 12 * NSA_HD].reshape(bs, ts, 2, NSA_KV, NSA_HD)
        win_s.append(jnp.concatenate([state_win_kv[l], new_win], axis=1)[:, -keep_s:])
        conv_p.append(ext_p[:, HALO + conv_rows - (CONV_W - 1):HALO + conv_rows])
        conv_s.append(ext_s[:, HALO + ts - (CONV_W - 1):HALO + ts])
        gm_s.append(gv_s.reshape(bs, SLAB, W_MIX)[:, :ts])

    y_p = _rmsnorm(xp, final_norm, slab=1, valid=1, out_dtype=F32)
    y_s = _rmsnorm(xs, final_norm, slab=1, valid=1, out_dtype=F32)
    return (y_p.reshape(bp, t, d), y_s.reshape(bs, SLAB, d)[:, :ts],
            jnp.stack(ret_p), jnp.stack(ret_s), jnp.stack(kv_p), jnp.stack(kv_s), jnp.stack(win_p), jnp.stack(win_s),
            jnp.stack(conv_p), jnp.stack(conv_s), jnp.stack(gm_s))
```

```python
import functools
import math

import numpy as np
import jax
import jax.numpy as jnp
from jax import lax
from jax.experimental import pallas as pl
from jax.experimental.pallas import tpu as pltpu

F32 = jnp.float32
BF16 = jnp.bfloat16

D_MODEL = 4096
DEPTH = 2
PAGE_SIZE = 128
W_MIX = D_MODEL // 4
N_BRANCH = 4
RET_HEADS = 4
RET_DK = W_MIX // RET_HEADS
RET_DV = W_MIX // RET_HEADS
RET_CHUNK = 128
ROPE_BASE = 10000.0
GM_CHUNK = 128
GM_GROUPS = 4
NSA_HEADS = 8
NSA_KV = 2
NSA_HD = W_MIX // NSA_HEADS
NSA_GROUP = NSA_HEADS // NSA_KV
CMP_LEN = 32
CMP_STRIDE = 16
SEL_LEN = 64
SEL_SHIFT = 6
SEL_TOPK = 16
WINDOW = 512
CONV_W = 31
D_FF = 4 * D_MODEL
EPS = 1e-6
LN_EPS = 1e-5

LANE = 128
SUBLANE = 8
SLAB = 8
VMEM_BIG = 56 * 1024 * 1024

A_RQ, A_RK, A_RV, A_RG = 0, 1024, 2048, 3072
A_GU, A_GV = 4096, 5120
A_NQ = 6144
B_CA, B_CB, B_GZ = 7168, 8192, 9216
A_NKV = 25600
C_NG = 27136
N_Z = 27648
W_BLK = 256
_O_NKV, _O_NG, _O_CA, _O_END = 7168, 8704, 8728, 27160

NEG = -1e30


def _round_up(a, b):
    return -(-a // b) * b


def _tile(m, pref):
    best = None
    for t in range(SUBLANE, min(m, pref) + 1, SUBLANE):
        if m % t == 0:
            best = t
    assert best is not None, (m, pref)
    return best


def _act_dtype(rows):
    return BF16 if rows % (2 * SUBLANE) == 0 else F32


def _params(sem, vmem=None):
    return pltpu.CompilerParams(dimension_semantics=sem, vmem_limit_bytes=vmem)


def _gelu(x):
    return 0.5 * x * (1.0 + jnp.tanh(0.7978845608028654 * (x + 0.044715 * (x * x * x))))


def _sigmoid(x):
    return 1.0 / (1.0 + jnp.exp(-x))


def _layernorm(x, w, b):
    mu = jnp.mean(x, axis=-1, keepdims=True)
    xc = x - mu
    var = jnp.mean(xc * xc, axis=-1, keepdims=True)
    return xc * lax.rsqrt(var + LN_EPS) * w + b


def _softmax_lanes(s, mask):
    sm = jnp.where(mask, s, NEG)
    m = jnp.max(sm, axis=-1, keepdims=True)
    m = jnp.where(m > 0.5 * NEG, m, 0.0)
    e = jnp.where(mask, jnp.exp(sm - m), 0.0)
    den = jnp.sum(e, axis=-1, keepdims=True)
    return e / jnp.where(den > 0.0, den, 1.0)


def _softmax_parts(s, mask):
    sm = jnp.where(mask, s, NEG)
    m = jnp.max(sm, axis=-1, keepdims=True)
    m = jnp.where(m > 0.5 * NEG, m, 0.0)
    e = jnp.exp(sm - m)
    den = jnp.sum(e, axis=-1, keepdims=True)
    return e, 1.0 / jnp.where(den > 0.0, den, 1.0)


def _dot(a, b):
    return jnp.dot(a, b, preferred_element_type=F32)


def _dot_nt(a, b):
    return lax.dot_general(a, b, (((1,), (1,)), ((), ())), preferred_element_type=F32)


def _dot_tn(a, b):
    return lax.dot_general(a, b, (((0,), (0,)), ((), ())), preferred_element_type=F32)


def _dot_split3(a, b_bf16):
    a1 = a.astype(BF16)
    r1 = a - a1.astype(F32)
    a2 = r1.astype(BF16)
    a3 = (r1 - a2.astype(F32)).astype(BF16)
    return _dot(a1, b_bf16) + _dot(a2, b_bf16) + _dot(a3, b_bf16)


def _rmsnorm_kernel(x_ref, w_ref, o_ref, *, slab, valid):
    x = x_ref[...]
    y = x * lax.rsqrt(jnp.mean(x * x, axis=-1, keepdims=True) + EPS) * w_ref[...]
    if valid < slab:
        row = lax.broadcasted_iota(jnp.int32, y.shape, 0)
        y = jnp.where((row & (slab - 1)) < valid, y, 0.0)
    o_ref[...] = y.astype(o_ref.dtype)


def _rmsnorm(x, w, *, slab, valid, out_dtype):
    m, d = x.shape
    tr = _tile(m, 256)
    assert tr % slab == 0 or valid == slab
    return pl.pallas_call(
        functools.partial(_rmsnorm_kernel, slab=slab, valid=valid),
        grid=(m // tr,),
        in_specs=[pl.BlockSpec((tr, d), lambda i: (i, 0)), pl.BlockSpec((1, d), lambda i: (0, 0))],
        out_specs=pl.BlockSpec((tr, d), lambda i: (i, 0)),
        out_shape=jax.ShapeDtypeStruct((m, d), out_dtype),
        compiler_params=_params(("parallel",)),
        name="rmsnorm",
    )(x, w.reshape(1, d))


def _cast_kernel(x_ref, o_ref):
    o_ref[...] = x_ref[...].astype(o_ref.dtype)


def _cast_bf16(w, *, rows):
    depth, r, c = w.shape
    return pl.pallas_call(
        _cast_kernel,
        grid=(depth, r // rows),
        in_specs=[pl.BlockSpec((None, rows, c), lambda l, i: (l, i, 0))],
        out_specs=pl.BlockSpec((None, rows, c), lambda l, i: (l, i, 0)),
        out_shape=jax.ShapeDtypeStruct(w.shape, BF16),
        compiler_params=_params(("parallel", "parallel"), VMEM_BIG),
        name="cast_weight",
    )(w)


def _w_in_source_row(j):
    nq_blocks = _O_NKV // W_BLK
    gz_end = nq_blocks + (_O_END - _O_CA) // W_BLK
    nkv_end = gz_end + (_O_NG - _O_NKV) // W_BLK
    src = jnp.where(j < nq_blocks, j * W_BLK,
                    jnp.where(j < gz_end, _O_CA + (j - nq_blocks) * W_BLK,
                              jnp.where(j < nkv_end, _O_NKV + (j - gz_end) * W_BLK, _O_NG)))
    return pl.multiple_of(src, SUBLANE)


def _cast_w_in_kernel(x_ref, o_ref):
    o_ref[...] = jnp.transpose(x_ref[0]).astype(o_ref.dtype)


def _cast_w_in(w_in):
    depth, k, n_cols = w_in.shape
    assert n_cols == _O_END and N_Z % W_BLK == 0 and _O_NG + W_BLK <= n_cols
    wt = jnp.swapaxes(w_in, 1, 2)
    return pl.pallas_call(
        _cast_w_in_kernel,
        grid=(depth, N_Z // W_BLK),
        in_specs=[pl.BlockSpec((pl.Element(1), pl.Element(W_BLK), pl.Element(k)),
                               lambda l, j: (l, _w_in_source_row(j), 0))],
        out_specs=pl.BlockSpec((None, k, W_BLK), lambda l, j: (l, 0, j)),
        out_shape=jax.ShapeDtypeStruct((depth, k, N_Z), BF16),
        compiler_params=_params(("parallel", "parallel")),
        name="cast_w_in",
    )(wt)


def _dense_kernel(*refs, act, has_res, kgrid):
    if has_res:
        xp_ref, xs_ref, w_ref, rp_ref, rs_ref, op_ref, os_ref = refs
    else:
        xp_ref, xs_ref, w_ref, op_ref, os_ref = refs
        rp_ref = rs_ref = None
    first_tile = pl.program_id(0) == 0
    wb = w_ref[...]
    mm = _dot

    def finish(acc, r_ref):
        if act == "relu2":
            acc = jnp.square(jnp.maximum(acc, 0.0))
        if r_ref is not None:
            acc = r_ref[...] + acc
        return acc

    if kgrid:
        @pl.when(pl.program_id(2) == 0)
        def _():
            op_ref[...] = rp_ref[...]
            os_ref[...] = rs_ref[...]

        op_ref[...] += mm(xp_ref[...], wb)

        @pl.when(first_tile)
        def _():
            os_ref[...] += mm(xs_ref[...], wb)
    else:
        op_ref[...] = finish(mm(xp_ref[...], wb), rp_ref).astype(op_ref.dtype)

        @pl.when(first_tile)
        def _():
            os_ref[...] = finish(mm(xs_ref[...], wb), rs_ref).astype(os_ref.dtype)

        @pl.when(jnp.logical_not(first_tile))
        def _():
            os_ref[...] = jnp.zeros(os_ref.shape, os_ref.dtype)


def _dense(xp, xs, w, *, layer, n, tn, tm=1024, tk=None, act=None, out_dtype=F32, resp=None, ress=None,
           name="dense"):
    mp, k = xp.shape
    ms = xs.shape[0]
    tm = _tile(mp, tm)
    kgrid = tk is not None
    has_res = resp is not None
    assert n % tn == 0 and (not kgrid or (has_res and act is None and k % tk == 0))
    if kgrid:
        grid = (mp // tm, n // tn, k // tk)
        row = lambda i, j, kk: (i, kk)
        srow = lambda i, j, kk: (0, kk)
        wix = lambda i, j, kk: (layer, kk, j)
        out = lambda i, j, kk: (i, j)
        sres = lambda i, j, kk: (0, j)
        sout = lambda i, j, kk: (i, j)
        kb = tk
        sem = ("arbitrary", "arbitrary", "arbitrary")
    else:
        grid = (mp // tm, n // tn)
        row = lambda i, j: (i, 0)
        srow = lambda i, j: (0, 0)
        wix = lambda i, j: (layer, 0, j)
        out = lambda i, j: (i, j)
        sres = lambda i, j: (0, j)
        sout = lambda i, j: (i, j)
        kb = k
        sem = ("arbitrary", "arbitrary")
    wblock = (None, kb, tn)
    x_mode = {} if kgrid else dict(pipeline_mode=pl.Buffered(1))
    in_specs = [pl.BlockSpec((tm, kb), row, **x_mode), pl.BlockSpec((ms, kb), srow, **x_mode),
                pl.BlockSpec(wblock, wix)]
    args = [xp, xs, w]
    if has_res:
        in_specs += [pl.BlockSpec((tm, tn), out), pl.BlockSpec((ms, tn), sres)]
        args += [resp, ress]
    op, os2 = pl.pallas_call(
        functools.partial(_dense_kernel, act=act, has_res=has_res, kgrid=kgrid),
        grid=grid,
        in_specs=in_specs,
        out_specs=[pl.BlockSpec((tm, tn), out), pl.BlockSpec((ms, tn), sout)],
        out_shape=[jax.ShapeDtypeStruct((mp, n), out_dtype),
                   jax.ShapeDtypeStruct((mp // tm * ms, n), out_dtype)],
        compiler_params=_params(sem, VMEM_BIG),
        name=name,
    )(*args)
    return op, os2[:ms]


def _ret_kernel(q_ref, k_ref, v_ref, g_ref, cos_ref, sin_ref, dm_ref, qd_ref, kd_ref, cd_ref, gw_ref, gb_ref,
                r0_ref, o_ref, ro_ref, r_sc, *, rows):
    ci = pl.program_id(1)

    @pl.when(ci == 0)
    def _():
        r_sc[...] = r0_ref[0]

    cos = cos_ref[...]
    sin = sin_ref[...]
    half = RET_DK // 2

    def rot(x):
        x1, x2 = x[:, :half], x[:, half:]
        return jnp.concatenate([x1 * cos - x2 * sin, x2 * cos + x1 * sin], axis=-1)

    def pad(x):
        if rows == RET_CHUNK:
            return x
        return jnp.concatenate([x, jnp.zeros((RET_CHUNK - rows, x.shape[1]), x.dtype)], axis=0)

    for h in range(RET_HEADS):
        cols = slice(h * RET_DK, (h + 1) * RET_DK)
        q = pad(rot(q_ref[:, cols]))
        k = pad(rot(k_ref[:, cols]) * (RET_DK ** -0.5))
        vb = pad(v_ref[:, cols]).astype(BF16)
        r = r_sc[h]
        s = _dot_nt(q.astype(BF16), k.astype(BF16)) * dm_ref[h]
        o = _dot(s.astype(BF16), vb) + _dot((q * qd_ref[h]).astype(BF16), r.astype(BF16))
        r_new = cd_ref[h] * r + _dot_tn((k * kd_ref[h]).astype(BF16), vb)
        r_sc[h] = r_new
        ro_ref[0, h] = r_new
        o = o[:rows]
        mu = jnp.mean(o, axis=-1, keepdims=True)
        oc = o - mu
        var = jnp.mean(oc * oc, axis=-1, keepdims=True)
        y = oc * lax.rsqrt(var + LN_EPS) * gw_ref[:, cols] + gb_ref[:, cols]
        g = g_ref[:, cols]
        o_ref[:, cols] = (g * _sigmoid(g) * y).astype(o_ref.dtype)


def _ret_tables(c_eff):
    log_g = np.log1p(-np.exp2(-5.0 - np.arange(RET_HEADS, dtype=np.float64)))
    i = np.arange(RET_CHUNK, dtype=np.float64)
    live = i < c_eff
    diff = i[:, None] - i[None, :]
    dmask = np.where(diff >= 0, np.exp(np.maximum(diff, 0.0)[None] * log_g[:, None, None]), 0.0)
    dmask = dmask * (live[:, None] & live[None, :])[None]
    q_dec = np.exp((i[None, :] + 1.0) * log_g[:, None]) * live[None]
    k_dec = np.exp((c_eff - 1.0 - i)[None, :] * log_g[:, None]) * live[None]
    c_dec = np.exp(c_eff * log_g)
    bc = lambda a: np.broadcast_to(a[:, :, None], (RET_HEADS, RET_CHUNK, RET_DK))
    return (jnp.asarray(dmask, F32), jnp.asarray(bc(q_dec), F32), jnp.asarray(bc(k_dec), F32),
            jnp.asarray(np.broadcast_to(c_dec[:, None, None], (RET_HEADS, 1, RET_DK)), F32))


def _rope_tables(positions, rows):
    half = RET_DK // 2
    inv = ROPE_BASE ** (-np.arange(half, dtype=np.float64) / half)
    ang = np.asarray(positions, np.float64)[:, None] * inv[None, :]
    cos = np.zeros((rows, half)); sin = np.zeros((rows, half))
    cos[:len(positions)] = np.cos(ang); sin[:len(positions)] = np.sin(ang)
    return jnp.asarray(cos, F32), jnp.asarray(sin, F32)


def _retention(z, r0, gn_w, gn_b, *, nb, rows, n_chunks, c_eff, cos, sin):
    m = z.shape[0]
    dmask, q_dec, k_dec, c_dec = _ret_tables(c_eff)
    zspec = lambda off: pl.BlockSpec((rows, W_MIX), lambda b, c: (b * n_chunks + c, off // W_MIX))
    tab = pl.BlockSpec((rows, RET_DK // 2), lambda b, c: (c, 0))
    whole = lambda a: pl.BlockSpec(a.shape, lambda b, c: (0,) * a.ndim)
    state = pl.BlockSpec((1, RET_HEADS, RET_DK, RET_DV), lambda b, c: (b, 0, 0, 0))
    gw, gb = gn_w.reshape(1, W_MIX), gn_b.reshape(1, W_MIX)
    return pl.pallas_call(
        functools.partial(_ret_kernel, rows=rows),
        grid=(nb, n_chunks),
        in_specs=[zspec(A_RQ), zspec(A_RK), zspec(A_RV), zspec(A_RG), tab, tab,
                  whole(dmask), whole(q_dec), whole(k_dec), whole(c_dec), whole(gw), whole(gb), state],
        out_specs=[pl.BlockSpec((rows, W_MIX), lambda b, c: (b * n_chunks + c, 0)), state],
        out_shape=[jax.ShapeDtypeStruct((m, W_MIX), _act_dtype(rows)),
                   jax.ShapeDtypeStruct((nb, RET_HEADS, RET_DK, RET_DV), F32)],
        scratch_shapes=[pltpu.VMEM((RET_HEADS, RET_DK, RET_DV), F32)],
        compiler_params=_params(("parallel", "arbitrary")),
        name="retention",
    )(z, z, z, z, cos, sin, dmask, q_dec, k_dec, c_dec, gw, gb, r0)


def _gm_kernel(u_ref, v_ref, lw_ref, lb_ref, ws_ref, bst_ref, o_ref, *maybe_gv_ref, rows):
    u = _gelu(u_ref[...])
    v = _layernorm(_gelu(v_ref[...]), lw_ref[...], lb_ref[...])
    for gv_ref in maybe_gv_ref:
        gv_ref[...] = v
    if rows < GM_CHUNK:
        v = jnp.concatenate([v, jnp.zeros((GM_CHUNK - rows, v.shape[1]), F32)], axis=0)
    ri = lax.broadcasted_iota(jnp.int32, (GM_CHUNK, GM_CHUNK), 0)
    cj = lax.broadcasted_iota(jnp.int32, (GM_CHUNK, GM_CHUNK), 1)
    gw = W_MIX // GM_GROUPS
    for g in range(GM_GROUPS):
        wm = jnp.where(cj <= ri, ws_ref[g], 0.0).astype(BF16)
        s = _dot(wm, v[:, g * gw:(g + 1) * gw].astype(BF16)) + bst_ref[:, g:g + 1]
        o_ref[:, g * gw:(g + 1) * gw] = (u[:, g * gw:(g + 1) * gw] * s[:rows]).astype(o_ref.dtype)


def _gmlp(z, ln_w, ln_b, ws, bs, *, rows, keep_v):
    m = z.shape[0]
    row = lambda: pl.BlockSpec((1, W_MIX), lambda i: (0, 0))
    n_out = 2 if keep_v else 1
    return pl.pallas_call(
        functools.partial(_gm_kernel, rows=rows),
        grid=(m // rows,),
        in_specs=[pl.BlockSpec((rows, W_MIX), lambda i: (i, A_GU // W_MIX)),
                  pl.BlockSpec((rows, W_MIX), lambda i: (i, A_GV // W_MIX)),
                  row(), row(),
                  pl.BlockSpec((GM_GROUPS, GM_CHUNK, GM_CHUNK), lambda i: (0, 0, 0)),
                  pl.BlockSpec((GM_CHUNK, GM_GROUPS), lambda i: (0, 0))],
        out_specs=[pl.BlockSpec((rows, W_MIX), lambda i: (i, 0)), pl.BlockSpec((rows, W_MIX), lambda i: (i, 0))][:n_out],
        out_shape=[jax.ShapeDtypeStruct((m, W_MIX), _act_dtype(rows)), jax.ShapeDtypeStruct((m, W_MIX), F32)][:n_out],
        compiler_params=_params(("parallel",)),
        name="gmlp",
    )(z, z, ln_w.reshape(1, W_MIX), ln_b.reshape(1, W_MIX), ws, bs.T)


HALO = 32
CONV_RB = 64


def _conv_kernel(a_ref, b_ref, buf_ref, cw_ref, cb_ref, lw_ref, lb_ref, o_ref, ext_ref, ext_sc, y_sc, *, rows):
    ti = pl.program_id(1)

    @pl.when(ti == 0)
    def _():
        ext_sc[0:HALO, :] = buf_ref[0]

    @pl.when(ti > 0)
    def _():
        ext_sc[0:HALO, :] = ext_sc[rows:rows + HALO, :]

    ext_sc[HALO:HALO + rows, :] = a_ref[...] * _sigmoid(b_ref[...])
    rb = min(CONV_RB, rows)
    first = HALO - (CONV_W - 1)
    for cc in range(W_MIX // LANE):
        lanes = slice(cc * LANE, (cc + 1) * LANE)
        for r0 in range(0, rows, rb):
            acc = jnp.broadcast_to(cb_ref[:, lanes], (rb, LANE))
            for w in range(CONV_W):
                acc = acc + ext_sc[first + w + r0:first + w + r0 + rb, lanes] * cw_ref[w:w + 1, lanes]
            y_sc[r0:r0 + rb, lanes] = acc
    y = _layernorm(y_sc[...], lw_ref[...], lb_ref[...])
    o_ref[...] = (y * _sigmoid(y)).astype(o_ref.dtype)
    ext_ref[0] = ext_sc[...]


def _conv_module(z, buf, cw, cb, ln_w, ln_b, *, nb, rows, n_tiles):
    m = z.shape[0]
    row = lambda: pl.BlockSpec((1, W_MIX), lambda b, t: (0, 0))
    cwp = jnp.concatenate([cw, jnp.zeros((HALO - CONV_W, W_MIX), F32)], axis=0)
    return pl.pallas_call(
        functools.partial(_conv_kernel, rows=rows),
        grid=(nb, n_tiles),
        in_specs=[pl.BlockSpec((rows, W_MIX), lambda b, t: (b * n_tiles + t, B_CA // W_MIX)),
                  pl.BlockSpec((rows, W_MIX), lambda b, t: (b * n_tiles + t, B_CB // W_MIX)),
                  pl.BlockSpec((1, HALO, W_MIX), lambda b, t: (b, 0, 0)),
                  pl.BlockSpec((HALO, W_MIX), lambda b, t: (0, 0)),
                  row(), row(), row()],
        out_specs=[pl.BlockSpec((rows, W_MIX), lambda b, t: (b * n_tiles + t, 0)),
                   pl.BlockSpec((1, HALO + rows, W_MIX), lambda b, t: (b, 0, 0))],
        out_shape=[jax.ShapeDtypeStruct((m, W_MIX), _act_dtype(rows)),
                   jax.ShapeDtypeStruct((nb, HALO + rows, W_MIX), F32)],
        scratch_shapes=[pltpu.VMEM((HALO + rows, W_MIX), F32), pltpu.VMEM((rows, W_MIX), F32)],
        compiler_params=_params(("parallel", "arbitrary")),
        name="conv_module",
    )(z, z, buf, cwp, cb.reshape(1, W_MIX), ln_w.reshape(1, W_MIX), ln_b.reshape(1, W_MIX))


def _overlap_t(n_c_pad, n_s_pad, n_c, n_s):
    cs = np.arange(n_c_pad)[:, None] * CMP_STRIDE
    ss = np.arange(n_s_pad)[None, :] * SEL_LEN
    ov = np.clip(np.minimum(ss + SEL_LEN, cs + CMP_LEN) - np.maximum(ss, cs), 0, None).astype(np.float64)
    ov = ov * (np.arange(n_c_pad)[:, None] < n_c) * (np.arange(n_s_pad)[None, :] < n_s)
    return jnp.asarray(ov, BF16)


def _cmp_p_kernel(x_ref, pe_ref, w1_ref, w2_ref, o_ref, xs_sc, *, t, ncp):
    xs_sc[0:t, :] = x_ref[...]
    xs_sc[t:, :] = jnp.zeros((xs_sc.shape[0] - t, NSA_HD), F32)
    acc = jnp.zeros((ncp, NSA_HD), F32)
    for l in range(CMP_LEN):
        rows = xs_sc[pl.ds(l, ncp, stride=CMP_STRIDE), :] + pe_ref[0, l:l + 1, :]
        acc = acc + _dot(rows.astype(BF16), w1_ref[0, l * NSA_HD:(l + 1) * NSA_HD, :])
    o_ref[0, 0] = _dot(_gelu(acc).astype(BF16), w2_ref[0])


def _compress_prompt(z, pe, w1, w2, *, nb, t):
    ncp = _round_up(t // CMP_STRIDE, LANE)
    pad_rows = _round_up(CMP_STRIDE * (ncp - 1) + CMP_LEN, SUBLANE)
    return pl.pallas_call(
        functools.partial(_cmp_p_kernel, t=t, ncp=ncp),
        grid=(nb, 4),
        in_specs=[pl.BlockSpec((t, NSA_HD), lambda b, j: (b, A_NKV // NSA_HD + j)),
                  pl.BlockSpec((1, CMP_LEN, NSA_HD), lambda b, j: (j // 2, 0, 0)),
                  pl.BlockSpec((1, CMP_LEN * NSA_HD, NSA_HD), lambda b, j: (j // 2, 0, 0)),
                  pl.BlockSpec((1, NSA_HD, NSA_HD), lambda b, j: (j // 2, 0, 0))],
        out_specs=pl.BlockSpec((1, 1, ncp, NSA_HD), lambda b, j: (b, j, 0, 0)),
        out_shape=jax.ShapeDtypeStruct((nb, 4, ncp, NSA_HD), F32),
        scratch_shapes=[pltpu.VMEM((max(pad_rows, t + SUBLANE), NSA_HD), F32)],
        compiler_params=_params(("parallel", "parallel")),
        name="nsa_compress_prompt",
    )(z, pe, w1, w2)


def _nsa_p_kernel(q_ref, kc_ref, vc_ref, ks_ref, vs_ref, kw_ref, vw_ref, ng_ref, ov_ref, ex_ref, o_ref, os_sc,
                  *, t, tq, n_c, n_s, span, key_limits):
    qi = pl.program_id(2)
    kv = pl.program_id(1)
    scale = NSA_HD ** -0.5
    g_n = NSA_GROUP
    q = q_ref[...]
    qh = [q[:, g * NSA_HD:(g + 1) * NSA_HD].astype(BF16) for g in range(g_n)]
    q4 = jnp.concatenate(qh, axis=0)
    qpos = qi * tq + lax.broadcasted_iota(jnp.int32, (tq, 1), 0)
    qpos4 = jnp.concatenate([qpos] * g_n, axis=0)

    kc = kc_ref[0, 0].astype(BF16)
    ncp = kc.shape[0]
    s_c = _dot_nt(q4, kc) * scale
    cidx = lax.broadcasted_iota(jnp.int32, (1, ncp), 1)
    mask_c = (cidx * CMP_STRIDE + CMP_LEN - 1 <= qpos4) & (cidx < n_c)
    p_c = _softmax_lanes(s_c, mask_c)
    o_c = _dot(p_c.astype(BF16), vc_ref[0, 0].astype(BF16))

    p_sum = p_c[0:tq]
    for g in range(1, g_n):
        p_sum = p_sum + p_c[g * tq:(g + 1) * tq]
    imp = _dot_split3(p_sum, ov_ref[...])
    nsp = imp.shape[1]
    blk = lax.broadcasted_iota(jnp.int32, (tq, nsp), 1)
    cur = jnp.right_shift(qpos, SEL_SHIFT)
    forced = (blk == 0) | (blk == cur) | (blk == cur - 1)
    valid = (blk * SEL_LEN <= qpos) & (blk < n_s)
    score = jnp.where(valid, jnp.where(forced, jnp.inf, imp), -jnp.inf)
    live = _round_up(n_s, SUBLANE)
    score_t = jnp.transpose(score)[0:live]
    blk_t = lax.broadcasted_iota(jnp.int32, (live, tq), 0)
    rank_t = jnp.zeros((live, tq), F32)
    for j in range(n_s):
        sj = score_t[j:j + 1, :]
        ahead = (sj > score_t) | ((sj == score_t) & (blk_t > j))
        rank_t = rank_t + jnp.where(ahead, 1.0, 0.0)
    sel_t = jnp.where((score_t > -jnp.inf) & (rank_t < float(min(SEL_TOPK, n_s))), 1.0, 0.0)
    sel_tb = jnp.concatenate([sel_t, jnp.zeros((nsp - live, tq), F32)], axis=0).astype(BF16)

    for lo, hi, n_keys in key_limits:
        @pl.when((qi >= lo) & (qi < hi))
        def _(n_keys=n_keys):
            kidx = lax.broadcasted_iota(jnp.int32, (1, n_keys), 1)
            mask_s = (_dot_tn(sel_tb, ex_ref[:, 0:n_keys]) > 0.5) & (kidx <= qpos)
            kb = ks_ref[0:n_keys, :].astype(BF16)
            vb = vs_ref[0:n_keys, :].astype(BF16)
            for g in range(g_n):
                e, inv = _softmax_parts(_dot_nt(qh[g], kb) * scale, mask_s)
                os_sc[g * tq:(g + 1) * tq, :] = _dot(e.astype(BF16), vb) * inv

    o_s = os_sc[...]

    w0 = pl.multiple_of(jnp.clip(qi * tq - WINDOW, 0, t - span), LANE)
    kw = kw_ref[pl.ds(w0, span), :].astype(BF16)
    vw = vw_ref[pl.ds(w0, span), :].astype(BF16)
    kpos = w0 + lax.broadcasted_iota(jnp.int32, (1, span), 1)
    dist = qpos4 - kpos
    mask_w = (dist >= 0) & (dist <= WINDOW)
    e_w, inv_w = _softmax_parts(_dot_nt(q4, kw) * scale, mask_w)
    o_w = _dot(e_w.astype(BF16), vw) * inv_w

    gates = _sigmoid(ng_ref[...])
    for g in range(g_n):
        rows = slice(g * tq, (g + 1) * tq)
        col = (kv * g_n + g) * 3
        lane = lax.broadcasted_iota(jnp.int32, gates.shape, 1)
        pick = lambda j: jnp.sum(jnp.where(lane == col + j, gates, 0.0), axis=-1, keepdims=True)
        o = pick(0) * o_c[rows] + pick(1) * o_s[rows] + pick(2) * o_w[rows]
        o_ref[:, g * NSA_HD:(g + 1) * NSA_HD] = o.astype(o_ref.dtype)


KEY_PREFIX_VARIANTS = 4


def _nsa_prompt(z, zc, cmp_kv, *, nb, t):
    m = z.shape[0]
    tq = 128
    nq = t // tq
    ends = sorted({-(-v * nq // KEY_PREFIX_VARIANTS) for v in range(1, KEY_PREFIX_VARIANTS + 1)})
    key_limits = tuple((lo, hi, hi * tq) for lo, hi in zip([0] + ends[:-1], ends))
    n_c = (t - CMP_LEN) // CMP_STRIDE + 1
    n_s = -(-t // SEL_LEN)
    ncp = cmp_kv.shape[2]
    nsp = LANE
    assert n_s <= nsp
    span = min(WINDOW + tq, t)
    ov = _overlap_t(ncp, nsp, n_c, n_s)
    ex = jnp.asarray((np.arange(t)[None, :] // SEL_LEN) == np.arange(nsp)[:, None], BF16)
    kvw = NSA_GROUP * NSA_HD
    nk = A_NKV // NSA_HD
    full = lambda j0: pl.BlockSpec((t, NSA_HD), lambda b, kv, qi: (b, nk + j0 + kv))
    return pl.pallas_call(
        functools.partial(_nsa_p_kernel, t=t, tq=tq, n_c=n_c, n_s=n_s, span=span, key_limits=key_limits),
        grid=(nb, NSA_KV, nq),
        in_specs=[pl.BlockSpec((tq, kvw), lambda b, kv, qi: (b * nq + qi, A_NQ // kvw + kv)),
                  pl.BlockSpec((1, 1, ncp, NSA_HD), lambda b, kv, qi: (b, kv, 0, 0)),
                  pl.BlockSpec((1, 1, ncp, NSA_HD), lambda b, kv, qi: (b, 2 + kv, 0, 0)),
                  full(4), full(6), full(8), full(10),
                  pl.BlockSpec((tq, LANE), lambda b, kv, qi: (b * nq + qi, C_NG // LANE)),
                  pl.BlockSpec((ncp, nsp), lambda b, kv, qi: (0, 0)),
                  pl.BlockSpec((nsp, t), lambda b, kv, qi: (0, 0))],
        out_specs=pl.BlockSpec((tq, kvw), lambda b, kv, qi: (b * nq + qi, kv)),
        out_shape=jax.ShapeDtypeStruct((m, W_MIX), BF16),
        scratch_shapes=[pltpu.VMEM((NSA_GROUP * tq, NSA_HD), F32)],
        compiler_params=_params(("parallel", "parallel", "arbitrary"), VMEM_BIG),
        name="nsa_prompt",
    )(z, cmp_kv, cmp_kv, z, z, z, z, zc, ov, ex)


PAGES_PER_STEP = 8
PAGE_ROWS = PAGE_SIZE * 4 * NSA_KV
GROUPS_PER_PAGE = PAGE_SIZE // CMP_STRIDE


def _s_cmp_kernel(pt_ref, *refs):
    del pt_ref
    pages = refs[:PAGES_PER_STEP]
    w_ref, o_ref = refs[PAGES_PER_STEP], refs[PAGES_PER_STEP + 1]
    gp = PAGES_PER_STEP * GROUPS_PER_PAGE
    for which in range(2):
        acc = jnp.zeros((NSA_KV * gp, 2 * NSA_HD), F32)
        for l in range(CMP_STRIDE):
            pieces = [pages[p][pl.ds(l * 4 * NSA_KV + which * NSA_KV + kv, GROUPS_PER_PAGE, stride=CMP_STRIDE * 4 * NSA_KV), :]
                      for kv in range(NSA_KV) for p in range(PAGES_PER_STEP)]
            acc = acc + _dot(jnp.concatenate(pieces, axis=0).astype(BF16), w_ref[which, l])
        for kv in range(NSA_KV):
            o_ref[0, which, kv] = acc[kv * gp:(kv + 1) * gp]


def _page_specs(layer, n):
    def spec(p):
        return pl.BlockSpec((None, None, PAGE_ROWS, NSA_HD),
                            lambda b, s, pt: (layer, pt[b, s * PAGES_PER_STEP + p], 0, 0))
    return [spec(p) for p in range(n)]


def _compress_sample_partial(cache4, page_table, w1ab, *, layer, nb, n_pages):
    steps = n_pages // PAGES_PER_STEP
    gp = PAGES_PER_STEP * GROUPS_PER_PAGE
    ng = n_pages * GROUPS_PER_PAGE
    return pl.pallas_call(
        _s_cmp_kernel,
        grid_spec=pltpu.PrefetchScalarGridSpec(
            num_scalar_prefetch=1,
            grid=(nb, steps),
            in_specs=_page_specs(layer, PAGES_PER_STEP)
            + [pl.BlockSpec((2, CMP_STRIDE, NSA_HD, 2 * NSA_HD), lambda b, s, pt: (0, 0, 0, 0))],
            out_specs=pl.BlockSpec((1, 2, NSA_KV, gp, 2 * NSA_HD), lambda b, s, pt: (b, 0, 0, s, 0)),
        ),
        out_shape=jax.ShapeDtypeStruct((nb, 2, NSA_KV, ng, 2 * NSA_HD), F32),
        compiler_params=_params(("parallel", "arbitrary"), VMEM_BIG),
        name="nsa_compress_sample",
    )(page_table, *([cache4] * PAGES_PER_STEP), w1ab)


def _s_sel_kernel(uv_ref, pef_ref, w1_ref, w2_ref, q_ref, ng_ref, kwn_ref, vwn_ref, kwp_ref, vwp_ref, ov_ref,
                  selt_ref, op_ref, *, q0, valid_rows, n_c, n_s):
    kv = pl.program_id(1)
    scale = NSA_HD ** -0.5
    g_n = NSA_GROUP
    rows = SLAB
    q = q_ref[...]
    q4 = jnp.concatenate([q[:, g * NSA_HD:(g + 1) * NSA_HD] for g in range(g_n)], axis=0).astype(BF16)
    tpos = lax.broadcasted_iota(jnp.int32, (rows, 1), 0)
    qpos = q0 + tpos
    qpos4 = jnp.concatenate([qpos] * g_n, axis=0)

    def compressed(which):
        uv = uv_ref[0, which, 0]
        ng = uv.shape[0]
        u = uv[:, :NSA_HD]
        v_next = pltpu.roll(uv[:, NSA_HD:], ng - 1, 0)
        const = _dot(pef_ref[which], w1_ref[which])[0:1]
        return _dot(_gelu(u + v_next + const).astype(BF16), w2_ref[which])

    kc = compressed(0)
    vc = compressed(1)
    ncp = kc.shape[0]
    s_c = _dot_nt(q4, kc.astype(BF16)) * scale
    cidx = lax.broadcasted_iota(jnp.int32, (1, ncp), 1)
    mask_c = (cidx * CMP_STRIDE + CMP_LEN - 1 <= qpos4) & (cidx < n_c)
    p_c = _softmax_lanes(s_c, mask_c)
    o_c = _dot(p_c.astype(BF16), vc.astype(BF16))

    p_sum = p_c[0:rows]
    for g in range(1, g_n):
        p_sum = p_sum + p_c[g * rows:(g + 1) * rows]
    imp = _dot_split3(p_sum, ov_ref[...])
    nsp = imp.shape[1]
    blk = lax.broadcasted_iota(jnp.int32, (rows, nsp), 1)
    cur = jnp.right_shift(qpos, SEL_SHIFT)
    forced = (blk == 0) | (blk == cur) | (blk == cur - 1)
    valid = (blk * SEL_LEN <= qpos) & (blk < n_s)
    score = jnp.where(valid, jnp.where(forced, jnp.inf, imp), -jnp.inf)
    score_pad = jnp.concatenate([score, jnp.zeros((LANE - rows, nsp), F32)], axis=0)
    score_t = jnp.transpose(score_pad)
    bi = lax.broadcasted_iota(jnp.int32, (nsp, nsp), 0)
    bj = lax.broadcasted_iota(jnp.int32, (nsp, nsp), 1)
    lane = lax.broadcasted_iota(jnp.int32, (nsp, LANE), 1)
    sel_t = jnp.zeros((nsp, LANE), F32)
    k_take = float(min(SEL_TOPK, n_s))
    for tt in range(valid_rows):
        s_i = score_t[:, tt:tt + 1]
        s_j = score[tt:tt + 1, :]
        ahead = (s_j > s_i) | ((s_j == s_i) & (bj < bi))
        rank_i = jnp.sum(jnp.where(ahead, 1.0, 0.0), axis=-1, keepdims=True)
        ok_i = (rank_i < k_take) & (s_i > -jnp.inf)
        sel_t = jnp.where(((lane & (rows - 1)) == tt) & (lane < g_n * rows) & ok_i, 1.0, sel_t)
    selt_ref[0, 0] = sel_t

    kw = jnp.concatenate([kwp_ref[...], kwn_ref[...]], axis=0).astype(BF16)
    vw = jnp.concatenate([vwp_ref[...], vwn_ref[...]], axis=0).astype(BF16)
    n_before = kwp_ref.shape[0]
    widx = lax.broadcasted_iota(jnp.int32, (1, n_before + rows), 1)
    kpos = jnp.where(widx < n_before, q0 - n_before + widx, q0 + widx - n_before)
    dist = qpos4 - kpos
    mask_w = (dist >= 0) & (dist <= WINDOW) & (kpos >= 0)
    s_w = _dot_nt(q4, kw) * scale
    p_w = _softmax_lanes(s_w, mask_w)
    o_w = _dot(p_w.astype(BF16), vw)

    gates = _sigmoid(ng_ref[...])
    glane = lax.broadcasted_iota(jnp.int32, gates.shape, 1)
    for g in range(g_n):
        r = slice(g * rows, (g + 1) * rows)
        col = (kv * g_n + g) * 3
        pick = lambda j: jnp.sum(jnp.where(glane == col + j, gates, 0.0), axis=-1, keepdims=True)
        op_ref[0, 0, r, :] = pick(0) * o_c[r] + pick(2) * o_w[r]


def _s_attn_kernel(pt_ref, *refs, n_steps, valid_rows):
    del pt_ref
    pages = refs[:PAGES_PER_STEP]
    q_ref, kvn_ref, ng_ref, selt_ref, op_ref, o_ref, m_sc, l_sc, acc_sc = refs[PAGES_PER_STEP:]
    s_id = pl.program_id(1)
    scale = NSA_HD ** -0.5
    g_n = NSA_GROUP
    rows = SLAB
    blocks_per_step = PAGES_PER_STEP * PAGE_SIZE // SEL_LEN

    @pl.when(s_id == 0)
    def _():
        m_sc[...] = jnp.full(m_sc.shape, NEG, F32)
        l_sc[...] = jnp.zeros(l_sc.shape, F32)
        acc_sc[...] = jnp.zeros(acc_sc.shape, F32)

    q = q_ref[...]

    def q_rows(kv):
        q4 = jnp.concatenate([q[:, (kv * g_n + g) * NSA_HD:(kv * g_n + g + 1) * NSA_HD] for g in range(g_n)], axis=0)
        return jnp.concatenate([q4, jnp.zeros((LANE - g_n * rows, NSA_HD), F32)], axis=0).astype(BF16)

    def update(kv, s_t, mask_t, v):
        m_old = m_sc[kv]
        m_new = jnp.maximum(m_old, jnp.max(jnp.where(mask_t, s_t, NEG), axis=0, keepdims=True))
        alpha = jnp.exp(m_old - m_new)
        p_t = jnp.where(mask_t, jnp.exp(jnp.minimum(s_t - m_new, 0.0)), 0.0)
        l_sc[kv] = alpha * l_sc[kv] + jnp.sum(p_t, axis=0, keepdims=True)
        acc_sc[kv] = alpha * acc_sc[kv] + _dot_tn(v, p_t.astype(BF16))
        m_sc[kv] = m_new

    stride = 4 * NSA_KV
    for kv in range(NSA_KV):
        qk = q_rows(kv)
        k = jnp.concatenate([pg[pl.ds(2 * NSA_KV + kv, PAGE_SIZE, stride=stride), :] for pg in pages], axis=0)
        v = jnp.concatenate([pg[pl.ds(3 * NSA_KV + kv, PAGE_SIZE, stride=stride), :] for pg in pages], axis=0)
        s_t = _dot_nt(k.astype(BF16), qk) * scale
        start = pl.multiple_of(s_id * blocks_per_step, blocks_per_step)
        chunk = selt_ref[0, kv, pl.ds(start, blocks_per_step), :]
        mask_t = jnp.concatenate([jnp.broadcast_to(chunk[c:c + 1, :], (SEL_LEN, LANE))
                                  for c in range(blocks_per_step)], axis=0) > 0.5
        update(kv, s_t, mask_t, v.astype(BF16))

    @pl.when(s_id == n_steps - 1)
    def _():
        n_past_blocks = n_steps * blocks_per_step
        kvn = jnp.concatenate([kvn_ref[...], jnp.zeros((LANE - rows, 4 * NSA_HD), F32)], axis=0)
        gates = _sigmoid(ng_ref[...])
        glane = lax.broadcasted_iota(jnp.int32, gates.shape, 1)
        for kv in range(NSA_KV):
            k_new = kvn[:, kv * NSA_HD:(kv + 1) * NSA_HD].astype(BF16)
            v_new = kvn[:, (NSA_KV + kv) * NSA_HD:(NSA_KV + kv + 1) * NSA_HD].astype(BF16)
            s_t = _dot_nt(k_new, q_rows(kv)) * scale
            key = lax.broadcasted_iota(jnp.int32, (LANE, LANE), 0)
            tok = lax.broadcasted_iota(jnp.int32, (LANE, LANE), 1) & (rows - 1)
            blk_ok = selt_ref[0, kv, n_past_blocks:n_past_blocks + 1, :] > 0.5
            mask_t = (key <= tok) & (key < valid_rows) & blk_ok
            update(kv, s_t, mask_t, v_new)
            l = l_sc[kv]
            o_t = acc_sc[kv] / jnp.where(l > 0.0, l, 1.0)
            o_sel = jnp.transpose(o_t)
            for g in range(g_n):
                r = slice(g * rows, (g + 1) * rows)
                col = (kv * g_n + g) * 3 + 1
                gate = jnp.sum(jnp.where(glane == col, gates, 0.0), axis=-1, keepdims=True)
                o = op_ref[0, kv, r, :] + gate * o_sel[r]
                o_ref[:, (kv * g_n + g) * NSA_HD:(kv * g_n + g + 1) * NSA_HD] = o.astype(o_ref.dtype)


def _nsa_sample(zs, zc, cache4, page_table, win_past, pe, w1, w2, *, layer, nb, valid_rows, q0):
    n_pages = page_table.shape[1]
    assert n_pages % PAGES_PER_STEP == 0 and q0 == n_pages * PAGE_SIZE
    s_len = q0 + valid_rows
    n_c = (s_len - CMP_LEN) // CMP_STRIDE + 1
    n_s = -(-s_len // SEL_LEN)
    ng = n_pages * GROUPS_PER_PAGE
    assert n_c <= ng - 1
    nsp = _round_up(n_s, LANE)
    steps = n_pages // PAGES_PER_STEP
    w1b = w1.astype(BF16)
    w1r = w1b.reshape(2, CMP_LEN, NSA_HD, NSA_HD)
    w1ab = jnp.concatenate([w1r[:, :CMP_STRIDE], w1r[:, CMP_STRIDE:]], axis=-1)
    uv = _compress_sample_partial(cache4, page_table, w1ab, layer=layer, nb=nb, n_pages=n_pages)
    pef = jnp.broadcast_to(pe.reshape(2, 1, CMP_LEN * NSA_HD), (2, SUBLANE, CMP_LEN * NSA_HD)).astype(BF16)
    ov = _overlap_t(ng, nsp, n_c, n_s)
    kvw = NSA_GROUP * NSA_HD
    nk = A_NKV // NSA_HD
    n_before = win_past.shape[4]
    selt, o_part = pl.pallas_call(
        functools.partial(_s_sel_kernel, q0=q0, valid_rows=valid_rows, n_c=n_c, n_s=n_s),
        grid=(nb, NSA_KV),
        in_specs=[pl.BlockSpec((1, 2, 1, ng, 2 * NSA_HD), lambda b, kv: (b, 0, kv, 0, 0)),
                  pl.BlockSpec((2, SUBLANE, CMP_LEN * NSA_HD), lambda b, kv: (0, 0, 0)),
                  pl.BlockSpec((2, CMP_LEN * NSA_HD, NSA_HD), lambda b, kv: (0, 0, 0)),
                  pl.BlockSpec((2, NSA_HD, NSA_HD), lambda b, kv: (0, 0, 0)),
                  pl.BlockSpec((SLAB, kvw), lambda b, kv: (b, A_NQ // kvw + kv)),
                  pl.BlockSpec((SLAB, LANE), lambda b, kv: (b, C_NG // LANE)),
                  pl.BlockSpec((SLAB, NSA_HD), lambda b, kv: (b, nk + 8 + kv)),
                  pl.BlockSpec((SLAB, NSA_HD), lambda b, kv: (b, nk + 10 + kv)),
                  pl.BlockSpec((None, None, None, None, n_before, NSA_HD), lambda b, kv: (layer, b, 0, kv, 0, 0)),
                  pl.BlockSpec((None, None, None, None, n_before, NSA_HD), lambda b, kv: (layer, b, 1, kv, 0, 0)),
                  pl.BlockSpec((ng, nsp), lambda b, kv: (0, 0))],
        out_specs=[pl.BlockSpec((1, 1, nsp, LANE), lambda b, kv: (b, kv, 0, 0)),
                   pl.BlockSpec((1, 1, NSA_GROUP * SLAB, NSA_HD), lambda b, kv: (b, kv, 0, 0))],
        out_shape=[jax.ShapeDtypeStruct((nb, NSA_KV, nsp, LANE), F32),
                   jax.ShapeDtypeStruct((nb, NSA_KV, NSA_GROUP * SLAB, NSA_HD), F32)],
        compiler_params=_params(("parallel", "parallel"), VMEM_BIG),
        name="nsa_select_sample",
    )(uv, pef, w1b, w2.astype(BF16), zs, zc, zs, zs, win_past, win_past, ov)
    return pl.pallas_call(
        functools.partial(_s_attn_kernel, n_steps=steps, valid_rows=valid_rows),
        grid_spec=pltpu.PrefetchScalarGridSpec(
            num_scalar_prefetch=1,
            grid=(nb, steps),
            in_specs=_page_specs(layer, PAGES_PER_STEP)
            + [pl.BlockSpec((SLAB, NSA_HEADS * NSA_HD), lambda b, s, pt: (b, A_NQ // (NSA_HEADS * NSA_HD))),
               pl.BlockSpec((SLAB, 4 * NSA_HD), lambda b, s, pt: (b, (A_NKV + 4 * NSA_HD) // (4 * NSA_HD))),
               pl.BlockSpec((SLAB, LANE), lambda b, s, pt: (b, C_NG // LANE)),
               pl.BlockSpec((1, NSA_KV, nsp, LANE), lambda b, s, pt: (b, 0, 0, 0)),
               pl.BlockSpec((1, NSA_KV, NSA_GROUP * SLAB, NSA_HD), lambda b, s, pt: (b, 0, 0, 0))],
            out_specs=pl.BlockSpec((SLAB, W_MIX), lambda b, s, pt: (b, 0)),
            scratch_shapes=[pltpu.VMEM((NSA_KV, 1, LANE), F32), pltpu.VMEM((NSA_KV, 1, LANE), F32),
                            pltpu.VMEM((NSA_KV, NSA_HD, LANE), F32)],
        ),
        out_shape=jax.ShapeDtypeStruct((nb * SLAB, W_MIX), _act_dtype(SLAB)),
        compiler_params=_params(("parallel", "arbitrary"), VMEM_BIG),
        name="nsa_attend_sample",
    )(page_table, *([cache4] * PAGES_PER_STEP), zs, zs, zc, selt, o_part)


def _merge_kernel(a_ref, b_ref, c_ref, d_ref, w_ref, g0_ref, g1_ref, g2_ref, g3_ref, o_ref):
    acc = None
    for n, (x_ref, g_ref) in enumerate(((a_ref, g0_ref), (b_ref, g1_ref), (c_ref, g2_ref), (d_ref, g3_ref))):
        term = _sigmoid(g_ref[...]) * _dot(x_ref[...].astype(BF16), w_ref[n])
        acc = term if acc is None else acc + term
    o_ref[...] = acc.astype(o_ref.dtype)


def _merge(branches, w_branch, zb, *, layer):
    m = zb.shape[0]
    tm = _tile(m, 1024)
    tn = 512
    br = pl.BlockSpec((tm, W_MIX), lambda i, j: (i, 0), pipeline_mode=pl.Buffered(1))
    gz = lambda n: pl.BlockSpec((tm, tn), lambda i, j: (i, (B_GZ + n * D_MODEL) // tn + j))
    return pl.pallas_call(
        _merge_kernel,
        grid=(m // tm, D_MODEL // tn),
        in_specs=[br, br, br, br, pl.BlockSpec((None, N_BRANCH, W_MIX, tn), lambda i, j: (layer, 0, 0, j)),
                  gz(0), gz(1), gz(2), gz(3)],
        out_specs=pl.BlockSpec((tm, tn), lambda i, j: (i, j)),
        out_shape=jax.ShapeDtypeStruct((m, D_MODEL), BF16),
        compiler_params=_params(("parallel", "parallel"), VMEM_BIG),
        name="branch_merge",
    )(*branches, w_branch, zb, zb, zb, zb)


def kernel(x_prompt, x_sample, cache_nsa_kv, state_ret, state_win_kv, state_conv, page_table, norm1, w_in, ret_gn_w,
           ret_gn_b, gm_ln_w, gm_ln_b, gm_ws, gm_bs, nsa_pe, nsa_w1, nsa_w2, conv_w, conv_b, conv_ln_w, conv_ln_b,
           w_branch, w_out, norm2, w_up, w_down, final_norm):
    bp, t, d = x_prompt.shape
    bs, ts, _ = x_sample.shape
    assert d == D_MODEL and t % RET_CHUNK == 0 and ts <= SLAB and norm1.shape[0] == DEPTH
    n_pages = page_table.shape[1]
    past_len = n_pages * PAGE_SIZE
    conv_rows = 128

    w_in_t = _cast_w_in(w_in)
    w_branch_b = _cast_bf16(w_branch.reshape(DEPTH, N_BRANCH * W_MIX, d), rows=512).reshape(w_branch.shape)
    w_out_b = _cast_bf16(w_out, rows=512)
    w_up_b = _cast_bf16(w_up, rows=128)
    w_down_b = _cast_bf16(w_down, rows=512)
    nsa_w1_b = nsa_w1.astype(BF16)
    nsa_w2_b = nsa_w2.astype(BF16)

    xp = x_prompt.reshape(bp * t, d)
    xs = jnp.pad(x_sample, ((0, 0), (0, SLAB - ts), (0, 0))).reshape(bs * SLAB, d)
    cache4 = cache_nsa_kv.reshape(DEPTH, cache_nsa_kv.shape[1], PAGE_ROWS, NSA_HD)
    win_t = jnp.transpose(state_win_kv, (0, 1, 3, 4, 2, 5))
    conv_pad = jnp.pad(state_conv, ((0, 0), (0, 0), (HALO - (CONV_W - 1), 0), (0, 0)))
    ret0_p = jnp.zeros((bp, RET_HEADS, RET_DK, RET_DV), F32)
    conv0_p = jnp.zeros((bp, HALO, W_MIX), F32)
    cos_p, sin_p = _rope_tables(np.arange(t), t)
    cos_s, sin_s = _rope_tables(past_len + np.arange(ts), SLAB)
    c_eff_s = math.gcd(ts, RET_CHUNK)
    assert c_eff_s == ts

    keep_p = min(WINDOW, t)
    keep_s = min(WINDOW, past_len + ts)
    ret_p, ret_s, kv_p, kv_s, win_p, win_s, conv_p, conv_s, gm_s = ([] for _ in range(9))
    for l in range(DEPTH):
        hp = _rmsnorm(xp, norm1[l], slab=1, valid=1, out_dtype=BF16)
        hs = _rmsnorm(xs, norm1[l], slab=SLAB, valid=ts, out_dtype=BF16)
        zap, zas = _dense(hp, hs, w_in_t, layer=l, n=N_Z, tn=1024, name="in_proj")
        zbp, zbs, zcp, zcs = zap, zas, zap, zas

        o_ret_p, r_p = _retention(zap, ret0_p, ret_gn_w[l], ret_gn_b[l], nb=bp, rows=RET_CHUNK,
                                  n_chunks=t // RET_CHUNK, c_eff=RET_CHUNK, cos=cos_p, sin=sin_p)
        (o_gm_p,) = _gmlp(zap, gm_ln_w[l], gm_ln_b[l], gm_ws[l], gm_bs[l], rows=GM_CHUNK, keep_v=False)
        cmp_kv = _compress_prompt(zap, nsa_pe[l], nsa_w1_b[l], nsa_w2_b[l], nb=bp, t=t)
        o_nsa_p = _nsa_prompt(zap, zcp, cmp_kv, nb=bp, t=t)
        o_conv_p, ext_p = _conv_module(zbp, conv0_p, conv_w[l], conv_b[l], conv_ln_w[l], conv_ln_b[l], nb=bp,
                                       rows=conv_rows, n_tiles=t // conv_rows)
        merged_p = _merge((o_ret_p, o_gm_p, o_nsa_p, o_conv_p), w_branch_b, zbp, layer=l)

        o_ret_s, r_s = _retention(zas, state_ret[l], ret_gn_w[l], ret_gn_b[l], nb=bs, rows=SLAB, n_chunks=1,
                                  c_eff=c_eff_s, cos=cos_s, sin=sin_s)
        o_gm_s, gv_s = _gmlp(zas, gm_ln_w[l], gm_ln_b[l], gm_ws[l], gm_bs[l], rows=SLAB, keep_v=True)
        o_nsa_s = _nsa_sample(zas, zcs, cache4, page_table, win_t, nsa_pe[l], nsa_w1[l], nsa_w2[l], layer=l, nb=bs,
                              valid_rows=ts, q0=past_len)
        o_conv_s, ext_s = _conv_module(zbs, conv_pad[l], conv_w[l], conv_b[l], conv_ln_w[l], conv_ln_b[l], nb=bs,
                                       rows=SLAB, n_tiles=1)
        merged_s = _merge((o_ret_s, o_gm_s, o_nsa_s, o_conv_s), w_branch_b, zbs, layer=l)

        xp, xs = _dense(merged_p, merged_s, w_out_b, layer=l, n=D_MODEL, tn=512, resp=xp, ress=xs, name="out_proj")
        h2p = _rmsnorm(xp, norm2[l], slab=1, valid=1, out_dtype=BF16)
        h2s = _rmsnorm(xs, norm2[l], slab=1, valid=1, out_dtype=BF16)
        up, us = _dense(h2p, h2s, w_up_b, layer=l, n=D_FF, tn=1024, act="relu2", out_dtype=BF16, name="mlp_up")
        xp, xs = _dense(up, us, w_down_b, layer=l, n=D_MODEL, tn=1024, tk=2048, resp=xp, ress=xs, name="mlp_down")

        ret_p.append(r_p)
        ret_s.append(r_s)
        z3p = zap.reshape(bp, t, N_Z)
        kv_p.append(z3p[:, :, A_NKV:A_NKV + 8 * NSA_HD].reshape(bp, t, 4, NSA_KV, NSA_HD))
        win_p.append(z3p[:, t - keep_p:, A_NKV + 8 * NSA_HD:A_NKV + 12 * NSA_HD].reshape(bp, keep_p, 2, NSA_KV, NSA_HD))
        z3s = zas.reshape(bs, SLAB, N_Z)[:, :ts]
        kv_s.append(z3s[:, :, A_NKV:A_NKV + 8 * NSA_HD].reshape(bs, ts, 4, NSA_KV, NSA_HD))
        new_win = z3s[:, :, A_NKV + 8 * NSA_HD:A_NKV + 12 * NSA_HD].reshape(bs, ts, 2, NSA_KV, NSA_HD)
        win_s.append(jnp.concatenate([state_win_kv[l], new_win], axis=1)[:, -keep_s:])
        conv_p.append(ext_p[:, HALO + conv_rows - (CONV_W - 1):HALO + conv_rows])
        conv_s.append(ext_s[:, HALO + ts - (CONV_W - 1):HALO + ts])
        gm_s.append(gv_s.reshape(bs, SLAB, W_MIX)[:, :ts])

    y_p = _rmsnorm(xp, final_norm, slab=1, valid=1, out_dtype=F32)
    y_s = _rmsnorm(xs, final_norm, slab=1, valid=1, out_dtype=F32)
    return (y_p.reshape(bp, t, d), y_s.reshape(bs, SLAB, d)[:, :ts],
            jnp.stack(ret_p), jnp.stack(ret_s), jnp.stack(kv_p), jnp.stack(kv_s), jnp.stack(win_p), jnp.stack(win_s),
            jnp.stack(conv_p), jnp.stack(conv_s), jnp.stack(gm_s))
```

```python
import functools
import math

import numpy as np
import jax
import jax.numpy as jnp
from jax import lax
from jax.experimental import pallas as pl
from jax.experimental.pallas import tpu as pltpu

F32 = jnp.float32
BF16 = jnp.bfloat16

D_MODEL = 4096
DEPTH = 2
PAGE_SIZE = 128
W_MIX = D_MODEL // 4
N_BRANCH = 4
RET_HEADS = 4
RET_DK = W_MIX // RET_HEADS
RET_DV = W_MIX // RET_HEADS
RET_CHUNK = 128
ROPE_BASE = 10000.0
GM_CHUNK = 128
GM_GROUPS = 4
NSA_HEADS = 8
NSA_KV = 2
NSA_HD = W_MIX // NSA_HEADS
NSA_GROUP = NSA_HEADS // NSA_KV
CMP_LEN = 32
CMP_STRIDE = 16
SEL_LEN = 64
SEL_SHIFT = 6
SEL_TOPK = 16
WINDOW = 512
CONV_W = 31
D_FF = 4 * D_MODEL
EPS = 1e-6
LN_EPS = 1e-5

LANE = 128
SUBLANE = 8
SLAB = 8
VMEM_BIG = 56 * 1024 * 1024

A_RQ, A_RK, A_RV, A_RG = 0, 1024, 2048, 3072
A_GU, A_GV = 4096, 5120
A_NQ = 6144
B_CA, B_CB, B_GZ = 7168, 8192, 9216
A_NKV = 25600
C_NG = 27136
N_Z = 27648
W_BLK = 512
_O_NKV, _O_NG, _O_CA, _O_END = 7168, 8704, 8728, 27160

NEG = -1e30


def _round_up(a, b):
    return -(-a // b) * b


def _tile(m, pref):
    best = None
    for t in range(SUBLANE, min(m, pref) + 1, SUBLANE):
        if m % t == 0:
            best = t
    assert best is not None, (m, pref)
    return best


def _act_dtype(rows):
    return BF16 if rows % (2 * SUBLANE) == 0 else F32


def _params(sem, vmem=None):
    return pltpu.CompilerParams(dimension_semantics=sem, vmem_limit_bytes=vmem)


def _gelu(x):
    return 0.5 * x * (1.0 + jnp.tanh(0.7978845608028654 * (x + 0.044715 * (x * x * x))))


def _sigmoid(x):
    return 1.0 / (1.0 + jnp.exp(-x))


def _layernorm(x, w, b):
    mu = jnp.mean(x, axis=-1, keepdims=True)
    xc = x - mu
    var = jnp.mean(xc * xc, axis=-1, keepdims=True)
    return xc * lax.rsqrt(var + LN_EPS) * w + b


def _softmax_lanes(s, mask):
    sm = jnp.where(mask, s, NEG)
    m = jnp.max(sm, axis=-1, keepdims=True)
    m = jnp.where(m > 0.5 * NEG, m, 0.0)
    e = jnp.where(mask, jnp.exp(sm - m), 0.0)
    den = jnp.sum(e, axis=-1, keepdims=True)
    return e / jnp.where(den > 0.0, den, 1.0)


def _softmax_parts(s, mask):
    sm = jnp.where(mask, s, NEG)
    m = jnp.max(sm, axis=-1, keepdims=True)
    m = jnp.where(m > 0.5 * NEG, m, 0.0)
    e = jnp.exp(sm - m)
    den = jnp.sum(e, axis=-1, keepdims=True)
    return e, 1.0 / jnp.where(den > 0.0, den, 1.0)


def _dot(a, b):
    return jnp.dot(a, b, preferred_element_type=F32)


def _dot_nt(a, b):
    return lax.dot_general(a, b, (((1,), (1,)), ((), ())), preferred_element_type=F32)


def _dot_tn(a, b):
    return lax.dot_general(a, b, (((0,), (0,)), ((), ())), preferred_element_type=F32)


def _dot_split3(a, b_bf16):
    a1 = a.astype(BF16)
    r1 = a - a1.astype(F32)
    a2 = r1.astype(BF16)
    a3 = (r1 - a2.astype(F32)).astype(BF16)
    return _dot(a1, b_bf16) + _dot(a2, b_bf16) + _dot(a3, b_bf16)


def _rmsnorm_kernel(x_ref, w_ref, o_ref, *, slab, valid):
    x = x_ref[...]
    y = x * lax.rsqrt(jnp.mean(x * x, axis=-1, keepdims=True) + EPS) * w_ref[...]
    if valid < slab:
        row = lax.broadcasted_iota(jnp.int32, y.shape, 0)
        y = jnp.where((row & (slab - 1)) < valid, y, 0.0)
    o_ref[...] = y.astype(o_ref.dtype)


def _rmsnorm(x, w, *, slab, valid, out_dtype):
    m, d = x.shape
    tr = _tile(m, 256)
    assert tr % slab == 0 or valid == slab
    return pl.pallas_call(
        functools.partial(_rmsnorm_kernel, slab=slab, valid=valid),
        grid=(m // tr,),
        in_specs=[pl.BlockSpec((tr, d), lambda i: (i, 0)), pl.BlockSpec((1, d), lambda i: (0, 0))],
        out_specs=pl.BlockSpec((tr, d), lambda i: (i, 0)),
        out_shape=jax.ShapeDtypeStruct((m, d), out_dtype),
        compiler_params=_params(("parallel",)),
        name="rmsnorm",
    )(x, w.reshape(1, d))


def _cast_kernel(x_ref, o_ref):
    o_ref[...] = x_ref[...].astype(o_ref.dtype)


def _cast_bf16(w, *, rows):
    depth, r, c = w.shape
    return pl.pallas_call(
        _cast_kernel,
        grid=(depth, r // rows),
        in_specs=[pl.BlockSpec((None, rows, c), lambda l, i: (l, i, 0))],
        out_specs=pl.BlockSpec((None, rows, c), lambda l, i: (l, i, 0)),
        out_shape=jax.ShapeDtypeStruct(w.shape, BF16),
        compiler_params=_params(("parallel", "parallel"), VMEM_BIG),
        name="cast_weight",
    )(w)


def _w_in_source_row(j):
    nq_blocks = _O_NKV // W_BLK
    gz_end = nq_blocks + (_O_END - _O_CA) // W_BLK
    nkv_end = gz_end + (_O_NG - _O_NKV) // W_BLK
    src = jnp.where(j < nq_blocks, j * W_BLK,
                    jnp.where(j < gz_end, _O_CA + (j - nq_blocks) * W_BLK,
                              jnp.where(j < nkv_end, _O_NKV + (j - gz_end) * W_BLK, _O_NG)))
    return pl.multiple_of(src, SUBLANE)


def _dense_kernel(*refs, act, has_res, kgrid, head, w_transposed):
    refs = list(refs)
    xp_ref = refs.pop(0)
    xs_ref = refs.pop(0) if head else None
    w_ref = refs.pop(0)
    rp_ref = refs.pop(0) if has_res else None
    rs_ref = refs.pop(0) if has_res and head else None
    if not head:
        refs.pop(0)
    op_ref = refs.pop(0)
    if head:
        os_ref, wo_ref = refs
        w = jnp.transpose(w_ref[0]) if w_transposed else w_ref[...]
        wb = w.astype(BF16)
        wo_ref[...] = wb
    else:
        wb = w_ref[...]

    def finish(acc, r_ref):
        if act == "relu2":
            acc = jnp.square(jnp.maximum(acc, 0.0))
        if r_ref is not None:
            acc = r_ref[...] + acc
        return acc

    if kgrid:
        @pl.when(pl.program_id(2) == 0)
        def _():
            op_ref[...] = rp_ref[...]
            if head:
                os_ref[...] = rs_ref[...]

        op_ref[...] += _dot(xp_ref[...], wb)
        if head:
            os_ref[...] += _dot(xs_ref[...], wb)
    else:
        op_ref[...] = finish(_dot(xp_ref[...], wb), rp_ref).astype(op_ref.dtype)
        if head:
            os_ref[...] = finish(_dot(xs_ref[...], wb), rs_ref).astype(os_ref.dtype)


def _dense(xp, xs, w, *, layer, n, tn, tn_head, tm=1024, tk=None, act=None, out_dtype=F32, resp=None, ress=None,
           w_rows=None, name="dense"):
    mp, k = xp.shape
    ms = xs.shape[0]
    tm = _tile(mp, tm)
    kgrid = tk is not None
    has_res = resp is not None
    assert n % tn == 0 and n % tn_head == 0 and (not kgrid or (has_res and act is None and k % tk == 0 and not w_rows))
    kb = tk if kgrid else k
    nk = k // kb
    wrap = (lambda f: f) if kgrid else (lambda f: (lambda i, j: f(i, j, 0)))
    sem = ("arbitrary",) * (3 if kgrid else 2)
    body = functools.partial(_dense_kernel, act=act, has_res=has_res, kgrid=kgrid, w_transposed=bool(w_rows))

    if w_rows:
        w_spec = pl.BlockSpec((pl.Element(1), pl.Element(tn_head), pl.Element(k)),
                              wrap(lambda i, j, kk: (layer, w_rows(j), 0)))
    else:
        w_spec = pl.BlockSpec((None, kb, tn_head), wrap(lambda i, j, kk: (layer, kk, j)))
    once = dict(pipeline_mode=pl.Buffered(1)) if not kgrid else {}
    in_specs = [pl.BlockSpec((tm, kb), wrap(lambda i, j, kk: (0, kk)), **once),
                pl.BlockSpec((ms, kb), wrap(lambda i, j, kk: (0, kk)), **once), w_spec]
    args = [xp, xs, w]
    if has_res:
        in_specs += [pl.BlockSpec((tm, tn_head), wrap(lambda i, j, kk: (0, j))),
                     pl.BlockSpec((ms, tn_head), wrap(lambda i, j, kk: (0, j)))]
        args += [resp, ress]
    op, os_, wb = pl.pallas_call(
        functools.partial(body, head=True),
        grid=(1, n // tn_head) + ((nk,) if kgrid else ()),
        in_specs=in_specs,
        out_specs=[pl.BlockSpec((tm, tn_head), wrap(lambda i, j, kk: (0, j))),
                   pl.BlockSpec((ms, tn_head), wrap(lambda i, j, kk: (0, j))),
                   pl.BlockSpec((kb, tn_head), wrap(lambda i, j, kk: (kk, j)))],
        out_shape=[jax.ShapeDtypeStruct((mp, n), out_dtype), jax.ShapeDtypeStruct((ms, n), out_dtype),
                   jax.ShapeDtypeStruct((k, n), BF16)],
        compiler_params=_params(sem, VMEM_BIG),
        name=name + "_head",
    )(*args)
    if mp == tm:
        return op, os_

    in_specs = [pl.BlockSpec((tm, kb), wrap(lambda i, j, kk: (i + 1, kk))),
                pl.BlockSpec((kb, tn), wrap(lambda i, j, kk: (kk, j)))]
    args = [xp, wb]
    if has_res:
        in_specs.append(pl.BlockSpec((tm, tn), wrap(lambda i, j, kk: (i + 1, j))))
        args.append(resp)
    in_specs.append(pl.BlockSpec(memory_space=pl.ANY))
    args.append(op)
    op = pl.pallas_call(
        functools.partial(body, head=False),
        grid=(mp // tm - 1, n // tn) + ((nk,) if kgrid else ()),
        in_specs=in_specs,
        out_specs=pl.BlockSpec((tm, tn), wrap(lambda i, j, kk: (i + 1, j))),
        out_shape=jax.ShapeDtypeStruct((mp, n), out_dtype),
        input_output_aliases={len(args) - 1: 0},
        compiler_params=_params(sem, VMEM_BIG),
        name=name + "_tail",
    )(*args)
    return op, os_


def _ret_kernel(q_ref, k_ref, v_ref, g_ref, cos_ref, sin_ref, dm_ref, qd_ref, kd_ref, cd_ref, gw_ref, gb_ref,
                r0_ref, o_ref, ro_ref, r_sc, *, rows):
    ci = pl.program_id(1)

    @pl.when(ci == 0)
    def _():
        r_sc[...] = r0_ref[0]

    cos = cos_ref[...]
    sin = sin_ref[...]
    half = RET_DK // 2

    def rot(x):
        x1, x2 = x[:, :half], x[:, half:]
        return jnp.concatenate([x1 * cos - x2 * sin, x2 * cos + x1 * sin], axis=-1)

    def pad(x):
        if rows == RET_CHUNK:
            return x
        return jnp.concatenate([x, jnp.zeros((RET_CHUNK - rows, x.shape[1]), x.dtype)], axis=0)

    for h in range(RET_HEADS):
        cols = slice(h * RET_DK, (h + 1) * RET_DK)
        q = pad(rot(q_ref[:, cols]))
        k = pad(rot(k_ref[:, cols]) * (RET_DK ** -0.5))
        vb = pad(v_ref[:, cols]).astype(BF16)
        r = r_sc[h]
        s = _dot_nt(q.astype(BF16), k.astype(BF16)) * dm_ref[h]
        o = _dot(s.astype(BF16), vb) + _dot((q * qd_ref[h]).astype(BF16), r.astype(BF16))
        r_new = cd_ref[h] * r + _dot_tn((k * kd_ref[h]).astype(BF16), vb)
        r_sc[h] = r_new
        ro_ref[0, h] = r_new
        o = o[:rows]
        mu = jnp.mean(o, axis=-1, keepdims=True)
        oc = o - mu
        var = jnp.mean(oc * oc, axis=-1, keepdims=True)
        y = oc * lax.rsqrt(var + LN_EPS) * gw_ref[:, cols] + gb_ref[:, cols]
        g = g_ref[:, cols]
        o_ref[:, cols] = (g * _sigmoid(g) * y).astype(o_ref.dtype)


def _ret_tables(c_eff):
    log_g = np.log1p(-np.exp2(-5.0 - np.arange(RET_HEADS, dtype=np.float64)))
    i = np.arange(RET_CHUNK, dtype=np.float64)
    live = i < c_eff
    diff = i[:, None] - i[None, :]
    dmask = np.where(diff >= 0, np.exp(np.maximum(diff, 0.0)[None] * log_g[:, None, None]), 0.0)
    dmask = dmask * (live[:, None] & live[None, :])[None]
    q_dec = np.exp((i[None, :] + 1.0) * log_g[:, None]) * live[None]
    k_dec = np.exp((c_eff - 1.0 - i)[None, :] * log_g[:, None]) * live[None]
    c_dec = np.exp(c_eff * log_g)
    bc = lambda a: np.broadcast_to(a[:, :, None], (RET_HEADS, RET_CHUNK, RET_DK))
    return (jnp.asarray(dmask, F32), jnp.asarray(bc(q_dec), F32), jnp.asarray(bc(k_dec), F32),
            jnp.asarray(np.broadcast_to(c_dec[:, None, None], (RET_HEADS, 1, RET_DK)), F32))


def _rope_tables(positions, rows):
    half = RET_DK // 2
    inv = ROPE_BASE ** (-np.arange(half, dtype=np.float64) / half)
    ang = np.asarray(positions, np.float64)[:, None] * inv[None, :]
    cos = np.zeros((rows, half)); sin = np.zeros((rows, half))
    cos[:len(positions)] = np.cos(ang); sin[:len(positions)] = np.sin(ang)
    return jnp.asarray(cos, F32), jnp.asarray(sin, F32)


def _retention(z, r0, gn_w, gn_b, *, nb, rows, n_chunks, c_eff, cos, sin):
    m = z.shape[0]
    dmask, q_dec, k_dec, c_dec = _ret_tables(c_eff)
    zspec = lambda off: pl.BlockSpec((rows, W_MIX), lambda b, c: (b * n_chunks + c, off // W_MIX))
    tab = pl.BlockSpec((rows, RET_DK // 2), lambda b, c: (c, 0))
    whole = lambda a: pl.BlockSpec(a.shape, lambda b, c: (0,) * a.ndim)
    state = pl.BlockSpec((1, RET_HEADS, RET_DK, RET_DV), lambda b, c: (b, 0, 0, 0))
    gw, gb = gn_w.reshape(1, W_MIX), gn_b.reshape(1, W_MIX)
    return pl.pallas_call(
        functools.partial(_ret_kernel, rows=rows),
        grid=(nb, n_chunks),
        in_specs=[zspec(A_RQ), zspec(A_RK), zspec(A_RV), zspec(A_RG), tab, tab,
                  whole(dmask), whole(q_dec), whole(k_dec), whole(c_dec), whole(gw), whole(gb), state],
        out_specs=[pl.BlockSpec((rows, W_MIX), lambda b, c: (b * n_chunks + c, 0)), state],
        out_shape=[jax.ShapeDtypeStruct((m, W_MIX), _act_dtype(rows)),
                   jax.ShapeDtypeStruct((nb, RET_HEADS, RET_DK, RET_DV), F32)],
        scratch_shapes=[pltpu.VMEM((RET_HEADS, RET_DK, RET_DV), F32)],
        compiler_params=_params(("parallel", "arbitrary")),
        name="retention",
    )(z, z, z, z, cos, sin, dmask, q_dec, k_dec, c_dec, gw, gb, r0)


def _gm_kernel(u_ref, v_ref, lw_ref, lb_ref, ws_ref, bst_ref, o_ref, *maybe_gv_ref, rows):
    u = _gelu(u_ref[...])
    v = _layernorm(_gelu(v_ref[...]), lw_ref[...], lb_ref[...])
    for gv_ref in maybe_gv_ref:
        gv_ref[...] = v
    if rows < GM_CHUNK:
        v = jnp.concatenate([v, jnp.zeros((GM_CHUNK - rows, v.shape[1]), F32)], axis=0)
    ri = lax.broadcasted_iota(jnp.int32, (GM_CHUNK, GM_CHUNK), 0)
    cj = lax.broadcasted_iota(jnp.int32, (GM_CHUNK, GM_CHUNK), 1)
    gw = W_MIX // GM_GROUPS
    for g in range(GM_GROUPS):
        wm = jnp.where(cj <= ri, ws_ref[g], 0.0).astype(BF16)
        s = _dot(wm, v[:, g * gw:(g + 1) * gw].astype(BF16)) + bst_ref[:, g:g + 1]
        o_ref[:, g * gw:(g + 1) * gw] = (u[:, g * gw:(g + 1) * gw] * s[:rows]).astype(o_ref.dtype)


def _gmlp(z, ln_w, ln_b, ws, bs, *, rows, keep_v):
    m = z.shape[0]
    row = lambda: pl.BlockSpec((1, W_MIX), lambda i: (0, 0))
    n_out = 2 if keep_v else 1
    return pl.pallas_call(
        functools.partial(_gm_kernel, rows=rows),
        grid=(m // rows,),
        in_specs=[pl.BlockSpec((rows, W_MIX), lambda i: (i, A_GU // W_MIX)),
                  pl.BlockSpec((rows, W_MIX), lambda i: (i, A_GV // W_MIX)),
                  row(), row(),
                  pl.BlockSpec((GM_GROUPS, GM_CHUNK, GM_CHUNK), lambda i: (0, 0, 0)),
                  pl.BlockSpec((GM_CHUNK, GM_GROUPS), lambda i: (0, 0))],
        out_specs=[pl.BlockSpec((rows, W_MIX), lambda i: (i, 0)), pl.BlockSpec((rows, W_MIX), lambda i: (i, 0))][:n_out],
        out_shape=[jax.ShapeDtypeStruct((m, W_MIX), _act_dtype(rows)), jax.ShapeDtypeStruct((m, W_MIX), F32)][:n_out],
        compiler_params=_params(("parallel",)),
        name="gmlp",
    )(z, z, ln_w.reshape(1, W_MIX), ln_b.reshape(1, W_MIX), ws, bs.T)


HALO = 32
CONV_RB = 64


def _conv_kernel(a_ref, b_ref, buf_ref, cw_ref, cb_ref, lw_ref, lb_ref, o_ref, ext_ref, ext_sc, y_sc, *, rows):
    ti = pl.program_id(1)

    @pl.when(ti == 0)
    def _():
        ext_sc[0:HALO, :] = buf_ref[0]

    @pl.when(ti > 0)
    def _():
        ext_sc[0:HALO, :] = ext_sc[rows:rows + HALO, :]

    ext_sc[HALO:HALO + rows, :] = a_ref[...] * _sigmoid(b_ref[...])
    rb = min(CONV_RB, rows)
    first = HALO - (CONV_W - 1)
    for cc in range(W_MIX // LANE):
        lanes = slice(cc * LANE, (cc + 1) * LANE)
        for r0 in range(0, rows, rb):
            acc = jnp.broadcast_to(cb_ref[:, lanes], (rb, LANE))
            for w in range(CONV_W):
                acc = acc + ext_sc[first + w + r0:first + w + r0 + rb, lanes] * cw_ref[w:w + 1, lanes]
            y_sc[r0:r0 + rb, lanes] = acc
    y = _layernorm(y_sc[...], lw_ref[...], lb_ref[...])
    o_ref[...] = (y * _sigmoid(y)).astype(o_ref.dtype)
    ext_ref[0] = ext_sc[...]


def _conv_module(z, buf, cw, cb, ln_w, ln_b, *, nb, rows, n_tiles):
    m = z.shape[0]
    row = lambda: pl.BlockSpec((1, W_MIX), lambda b, t: (0, 0))
    cwp = jnp.concatenate([cw, jnp.zeros((HALO - CONV_W, W_MIX), F32)], axis=0)
    return pl.pallas_call(
        functools.partial(_conv_kernel, rows=rows),
        grid=(nb, n_tiles),
        in_specs=[pl.BlockSpec((rows, W_MIX), lambda b, t: (b * n_tiles + t, B_CA // W_MIX)),
                  pl.BlockSpec((rows, W_MIX), lambda b, t: (b * n_tiles + t, B_CB // W_MIX)),
                  pl.BlockSpec((1, HALO, W_MIX), lambda b, t: (b, 0, 0)),
                  pl.BlockSpec((HALO, W_MIX), lambda b, t: (0, 0)),
                  row(), row(), row()],
        out_specs=[pl.BlockSpec((rows, W_MIX), lambda b, t: (b * n_tiles + t, 0)),
                   pl.BlockSpec((1, HALO + rows, W_MIX), lambda b, t: (b, 0, 0))],
        out_shape=[jax.ShapeDtypeStruct((m, W_MIX), _act_dtype(rows)),
                   jax.ShapeDtypeStruct((nb, HALO + rows, W_MIX), F32)],
        scratch_shapes=[pltpu.VMEM((HALO + rows, W_MIX), F32), pltpu.VMEM((rows, W_MIX), F32)],
        compiler_params=_params(("parallel", "arbitrary")),
        name="conv_module",
    )(z, z, buf, cwp, cb.reshape(1, W_MIX), ln_w.reshape(1, W_MIX), ln_b.reshape(1, W_MIX))


def _overlap_t(n_c_pad, n_s_pad, n_c, n_s):
    cs = np.arange(n_c_pad)[:, None] * CMP_STRIDE
    ss = np.arange(n_s_pad)[None, :] * SEL_LEN
    ov = np.clip(np.minimum(ss + SEL_LEN, cs + CMP_LEN) - np.maximum(ss, cs), 0, None).astype(np.float64)
    ov = ov * (np.arange(n_c_pad)[:, None] < n_c) * (np.arange(n_s_pad)[None, :] < n_s)
    return jnp.asarray(ov, BF16)


def _cmp_p_kernel(x_ref, pe_ref, w1_ref, w2_ref, o_ref, xs_sc, *, t, ncp):
    xs_sc[0:t, :] = x_ref[...]
    xs_sc[t:, :] = jnp.zeros((xs_sc.shape[0] - t, NSA_HD), F32)
    acc = jnp.zeros((ncp, NSA_HD), F32)
    for l in range(CMP_LEN):
        rows = xs_sc[pl.ds(l, ncp, stride=CMP_STRIDE), :] + pe_ref[0, l:l + 1, :]
        acc = acc + _dot(rows.astype(BF16), w1_ref[0, l * NSA_HD:(l + 1) * NSA_HD, :])
    o_ref[0, 0] = _dot(_gelu(acc).astype(BF16), w2_ref[0])


def _compress_prompt(z, pe, w1, w2, *, nb, t):
    ncp = _round_up(t // CMP_STRIDE, LANE)
    pad_rows = _round_up(CMP_STRIDE * (ncp - 1) + CMP_LEN, SUBLANE)
    return pl.pallas_call(
        functools.partial(_cmp_p_kernel, t=t, ncp=ncp),
        grid=(nb, 4),
        in_specs=[pl.BlockSpec((t, NSA_HD), lambda b, j: (b, A_NKV // NSA_HD + j)),
                  pl.BlockSpec((1, CMP_LEN, NSA_HD), lambda b, j: (j // 2, 0, 0)),
                  pl.BlockSpec((1, CMP_LEN * NSA_HD, NSA_HD), lambda b, j: (j // 2, 0, 0)),
                  pl.BlockSpec((1, NSA_HD, NSA_HD), lambda b, j: (j // 2, 0, 0))],
        out_specs=pl.BlockSpec((1, 1, ncp, NSA_HD), lambda b, j: (b, j, 0, 0)),
        out_shape=jax.ShapeDtypeStruct((nb, 4, ncp, NSA_HD), F32),
        scratch_shapes=[pltpu.VMEM((max(pad_rows, t + SUBLANE), NSA_HD), F32)],
        compiler_params=_params(("parallel", "parallel")),
        name="nsa_compress_prompt",
    )(z, pe, w1, w2)


def _nsa_p_kernel(q_ref, kc_ref, vc_ref, ks_ref, vs_ref, kw_ref, vw_ref, ng_ref, ov_ref, ex_ref, o_ref, os_sc,
                  *, t, tq, n_c, n_s, span, key_limits):
    qi = pl.program_id(2)
    kv = pl.program_id(1)
    scale = NSA_HD ** -0.5
    g_n = NSA_GROUP
    q = q_ref[...]
    qh = [q[:, g * NSA_HD:(g + 1) * NSA_HD].astype(BF16) for g in range(g_n)]
    q4 = jnp.concatenate(qh, axis=0)
    qpos = qi * tq + lax.broadcasted_iota(jnp.int32, (tq, 1), 0)
    qpos4 = jnp.concatenate([qpos] * g_n, axis=0)

    kc = kc_ref[0, 0].astype(BF16)
    ncp = kc.shape[0]
    s_c = _dot_nt(q4, kc) * scale
    cidx = lax.broadcasted_iota(jnp.int32, (1, ncp), 1)
    mask_c = (cidx * CMP_STRIDE + CMP_LEN - 1 <= qpos4) & (cidx < n_c)
    p_c = _softmax_lanes(s_c, mask_c)
    o_c = _dot(p_c.astype(BF16), vc_ref[0, 0].astype(BF16))

    p_sum = p_c[0:tq]
    for g in range(1, g_n):
        p_sum = p_sum + p_c[g * tq:(g + 1) * tq]
    imp = _dot_split3(p_sum, ov_ref[...])
    nsp = imp.shape[1]
    blk = lax.broadcasted_iota(jnp.int32, (tq, nsp), 1)
    cur = jnp.right_shift(qpos, SEL_SHIFT)
    forced = (blk == 0) | (blk == cur) | (blk == cur - 1)
    valid = (blk * SEL_LEN <= qpos) & (blk < n_s)
    score = jnp.where(valid, jnp.where(forced, jnp.inf, imp), -jnp.inf)
    live = _round_up(n_s, SUBLANE)
    score_t = jnp.transpose(score)[0:live]
    blk_t = lax.broadcasted_iota(jnp.int32, (live, tq), 0)
    rank_t = jnp.zeros((live, tq), F32)
    for j in range(n_s):
        sj = score_t[j:j + 1, :]
        ahead = (sj > score_t) | ((sj == score_t) & (blk_t > j))
        rank_t = rank_t + jnp.where(ahead, 1.0, 0.0)
    sel_t = jnp.where((score_t > -jnp.inf) & (rank_t < float(min(SEL_TOPK, n_s))), 1.0, 0.0)
    sel_tb = jnp.concatenate([sel_t, jnp.zeros((nsp - live, tq), F32)], axis=0).astype(BF16)

    for lo, hi, n_keys in key_limits:
        @pl.when((qi >= lo) & (qi < hi))
        def _(n_keys=n_keys):
            kidx = lax.broadcasted_iota(jnp.int32, (1, n_keys), 1)
            mask_s = (_dot_tn(sel_tb, ex_ref[:, 0:n_keys]) > 0.5) & (kidx <= qpos)
            kb = ks_ref[0:n_keys, :].astype(BF16)
            vb = vs_ref[0:n_keys, :].astype(BF16)
            for g in range(g_n):
                e, inv = _softmax_parts(_dot_nt(qh[g], kb) * scale, mask_s)
                os_sc[g * tq:(g + 1) * tq, :] = _dot(e.astype(BF16), vb) * inv

    o_s = os_sc[...]

    w0 = pl.multiple_of(jnp.clip(qi * tq - WINDOW, 0, t - span), LANE)
    kw = kw_ref[pl.ds(w0, span), :].astype(BF16)
    vw = vw_ref[pl.ds(w0, span), :].astype(BF16)
    kpos = w0 + lax.broadcasted_iota(jnp.int32, (1, span), 1)
    dist = qpos4 - kpos
    mask_w = (dist >= 0) & (dist <= WINDOW)
    e_w, inv_w = _softmax_parts(_dot_nt(q4, kw) * scale, mask_w)
    o_w = _dot(e_w.astype(BF16), vw) * inv_w

    gates = _sigmoid(ng_ref[...])
    for g in range(g_n):
        rows = slice(g * tq, (g + 1) * tq)
        col = (kv * g_n + g) * 3
        lane = lax.broadcasted_iota(jnp.int32, gates.shape, 1)
        pick = lambda j: jnp.sum(jnp.where(lane == col + j, gates, 0.0), axis=-1, keepdims=True)
        o = pick(0) * o_c[rows] + pick(1) * o_s[rows] + pick(2) * o_w[rows]
        o_ref[:, g * NSA_HD:(g + 1) * NSA_HD] = o.astype(o_ref.dtype)


KEY_PREFIX_VARIANTS = 4


def _nsa_prompt(z, zc, cmp_kv, *, nb, t):
    m = z.shape[0]
    tq = 128
    nq = t // tq
    ends = sorted({-(-v * nq // KEY_PREFIX_VARIANTS) for v in range(1, KEY_PREFIX_VARIANTS + 1)})
    key_limits = tuple((lo, hi, hi * tq) for lo, hi in zip([0] + ends[:-1], ends))
    n_c = (t - CMP_LEN) // CMP_STRIDE + 1
    n_s = -(-t // SEL_LEN)
    ncp = cmp_kv.shape[2]
    nsp = LANE
    assert n_s <= nsp
    span = min(WINDOW + tq, t)
    ov = _overlap_t(ncp, nsp, n_c, n_s)
    ex = jnp.asarray((np.arange(t)[None, :] // SEL_LEN) == np.arange(nsp)[:, None], BF16)
    kvw = NSA_GROUP * NSA_HD
    nk = A_NKV // NSA_HD
    full = lambda j0: pl.BlockSpec((t, NSA_HD), lambda b, kv, qi: (b, nk + j0 + kv))
    return pl.pallas_call(
        functools.partial(_nsa_p_kernel, t=t, tq=tq, n_c=n_c, n_s=n_s, span=span, key_limits=key_limits),
        grid=(nb, NSA_KV, nq),
        in_specs=[pl.BlockSpec((tq, kvw), lambda b, kv, qi: (b * nq + qi, A_NQ // kvw + kv)),
                  pl.BlockSpec((1, 1, ncp, NSA_HD), lambda b, kv, qi: (b, kv, 0, 0)),
                  pl.BlockSpec((1, 1, ncp, NSA_HD), lambda b, kv, qi: (b, 2 + kv, 0, 0)),
                  full(4), full(6), full(8), full(10),
                  pl.BlockSpec((tq, LANE), lambda b, kv, qi: (b * nq + qi, C_NG // LANE)),
                  pl.BlockSpec((ncp, nsp), lambda b, kv, qi: (0, 0)),
                  pl.BlockSpec((nsp, t), lambda b, kv, qi: (0, 0))],
        out_specs=pl.BlockSpec((tq, kvw), lambda b, kv, qi: (b * nq + qi, kv)),
        out_shape=jax.ShapeDtypeStruct((m, W_MIX), BF16),
        scratch_shapes=[pltpu.VMEM((NSA_GROUP * tq, NSA_HD), F32)],
        compiler_params=_params(("parallel", "parallel", "arbitrary"), VMEM_BIG),
        name="nsa_prompt",
    )(z, cmp_kv, cmp_kv, z, z, z, z, zc, ov, ex)


PAGES_PER_STEP = 8
PAGE_ROWS = PAGE_SIZE * 4 * NSA_KV
GROUPS_PER_PAGE = PAGE_SIZE // CMP_STRIDE


def _s_cmp_kernel(pt_ref, *refs):
    del pt_ref
    pages = refs[:PAGES_PER_STEP]
    w_ref, o_ref = refs[PAGES_PER_STEP], refs[PAGES_PER_STEP + 1]
    gp = PAGES_PER_STEP * GROUPS_PER_PAGE
    for which in range(2):
        acc = jnp.zeros((NSA_KV * gp, 2 * NSA_HD), F32)
        for l in range(CMP_STRIDE):
            pieces = [pages[p][pl.ds(l * 4 * NSA_KV + which * NSA_KV + kv, GROUPS_PER_PAGE, stride=CMP_STRIDE * 4 * NSA_KV), :]
                      for kv in range(NSA_KV) for p in range(PAGES_PER_STEP)]
            acc = acc + _dot(jnp.concatenate(pieces, axis=0).astype(BF16), w_ref[which, l])
        for kv in range(NSA_KV):
            o_ref[0, which, kv] = acc[kv * gp:(kv + 1) * gp]


def _page_specs(layer, n):
    def spec(p):
        return pl.BlockSpec((None, None, PAGE_ROWS, NSA_HD),
                            lambda b, s, pt: (layer, pt[b, s * PAGES_PER_STEP + p], 0, 0))
    return [spec(p) for p in range(n)]


def _compress_sample_partial(cache4, page_table, w1ab, *, layer, nb, n_pages):
    steps = n_pages // PAGES_PER_STEP
    gp = PAGES_PER_STEP * GROUPS_PER_PAGE
    ng = n_pages * GROUPS_PER_PAGE
    return pl.pallas_call(
        _s_cmp_kernel,
        grid_spec=pltpu.PrefetchScalarGridSpec(
            num_scalar_prefetch=1,
            grid=(nb, steps),
            in_specs=_page_specs(layer, PAGES_PER_STEP)
            + [pl.BlockSpec((2, CMP_STRIDE, NSA_HD, 2 * NSA_HD), lambda b, s, pt: (0, 0, 0, 0))],
            out_specs=pl.BlockSpec((1, 2, NSA_KV, gp, 2 * NSA_HD), lambda b, s, pt: (b, 0, 0, s, 0)),
        ),
        out_shape=jax.ShapeDtypeStruct((nb, 2, NSA_KV, ng, 2 * NSA_HD), F32),
        compiler_params=_params(("parallel", "arbitrary"), VMEM_BIG),
        name="nsa_compress_sample",
    )(page_table, *([cache4] * PAGES_PER_STEP), w1ab)


def _s_sel_kernel(uv_ref, pef_ref, w1_ref, w2_ref, q_ref, ng_ref, kwn_ref, vwn_ref, kwp_ref, vwp_ref, ov_ref,
                  selt_ref, op_ref, *, q0, valid_rows, n_c, n_s):
    kv = pl.program_id(1)
    scale = NSA_HD ** -0.5
    g_n = NSA_GROUP
    rows = SLAB
    q = q_ref[...]
    q4 = jnp.concatenate([q[:, g * NSA_HD:(g + 1) * NSA_HD] for g in range(g_n)], axis=0).astype(BF16)
    tpos = lax.broadcasted_iota(jnp.int32, (rows, 1), 0)
    qpos = q0 + tpos
    qpos4 = jnp.concatenate([qpos] * g_n, axis=0)

    def compressed(which):
        uv = uv_ref[0, which, 0]
        ng = uv.shape[0]
        u = uv[:, :NSA_HD]
        v_next = pltpu.roll(uv[:, NSA_HD:], ng - 1, 0)
        const = _dot(pef_ref[which], w1_ref[which])[0:1]
        return _dot(_gelu(u + v_next + const).astype(BF16), w2_ref[which])

    kc = compressed(0)
    vc = compressed(1)
    ncp = kc.shape[0]
    s_c = _dot_nt(q4, kc.astype(BF16)) * scale
    cidx = lax.broadcasted_iota(jnp.int32, (1, ncp), 1)
    mask_c = (cidx * CMP_STRIDE + CMP_LEN - 1 <= qpos4) & (cidx < n_c)
    p_c = _softmax_lanes(s_c, mask_c)
    o_c = _dot(p_c.astype(BF16), vc.astype(BF16))

    p_sum = p_c[0:rows]
    for g in range(1, g_n):
        p_sum = p_sum + p_c[g * rows:(g + 1) * rows]
    imp = _dot_split3(p_sum, ov_ref[...])
    nsp = imp.shape[1]
    blk = lax.broadcasted_iota(jnp.int32, (rows, nsp), 1)
    cur = jnp.right_shift(qpos, SEL_SHIFT)
    forced = (blk == 0) | (blk == cur) | (blk == cur - 1)
    valid = (blk * SEL_LEN <= qpos) & (blk < n_s)
    score = jnp.where(valid, jnp.where(forced, jnp.inf, imp), -jnp.inf)
    score_pad = jnp.concatenate([score, jnp.zeros((LANE - rows, nsp), F32)], axis=0)
    score_t = jnp.transpose(score_pad)
    bi = lax.broadcasted_iota(jnp.int32, (nsp, nsp), 0)
    bj = lax.broadcasted_iota(jnp.int32, (nsp, nsp), 1)
    lane = lax.broadcasted_iota(jnp.int32, (nsp, LANE), 1)
    sel_t = jnp.zeros((nsp, LANE), F32)
    k_take = float(min(SEL_TOPK, n_s))
    for tt in range(valid_rows):
        s_i = score_t[:, tt:tt + 1]
        s_j = score[tt:tt + 1, :]
        ahead = (s_j > s_i) | ((s_j == s_i) & (bj < bi))
        rank_i = jnp.sum(jnp.where(ahead, 1.0, 0.0), axis=-1, keepdims=True)
        ok_i = (rank_i < k_take) & (s_i > -jnp.inf)
        sel_t = jnp.where(((lane & (rows - 1)) == tt) & (lane < g_n * rows) & ok_i, 1.0, sel_t)
    selt_ref[0, 0] = sel_t

    kw = jnp.concatenate([kwp_ref[...], kwn_ref[...]], axis=0).astype(BF16)
    vw = jnp.concatenate([vwp_ref[...], vwn_ref[...]], axis=0).astype(BF16)
    n_before = kwp_ref.shape[0]
    widx = lax.broadcasted_iota(jnp.int32, (1, n_before + rows), 1)
    kpos = jnp.where(widx < n_before, q0 - n_before + widx, q0 + widx - n_before)
    dist = qpos4 - kpos
    mask_w = (dist >= 0) & (dist <= WINDOW) & (kpos >= 0)
    s_w = _dot_nt(q4, kw) * scale
    p_w = _softmax_lanes(s_w, mask_w)
    o_w = _dot(p_w.astype(BF16), vw)

    gates = _sigmoid(ng_ref[...])
    glane = lax.broadcasted_iota(jnp.int32, gates.shape, 1)
    for g in range(g_n):
        r = slice(g * rows, (g + 1) * rows)
        col = (kv * g_n + g) * 3
        pick = lambda j: jnp.sum(jnp.where(glane == col + j, gates, 0.0), axis=-1, keepdims=True)
        op_ref[0, 0, r, :] = pick(0) * o_c[r] + pick(2) * o_w[r]


def _s_attn_kernel(pt_ref, *refs, n_steps, valid_rows):
    del pt_ref
    pages = refs[:PAGES_PER_STEP]
    q_ref, kvn_ref, ng_ref, selt_ref, op_ref, o_ref, m_sc, l_sc, acc_sc = refs[PAGES_PER_STEP:]
    s_id = pl.program_id(1)
    scale = NSA_HD ** -0.5
    g_n = NSA_GROUP
    rows = SLAB
    blocks_per_step = PAGES_PER_STEP * PAGE_SIZE // SEL_LEN

    @pl.when(s_id == 0)
    def _():
        m_sc[...] = jnp.full(m_sc.shape, NEG, F32)
        l_sc[...] = jnp.zeros(l_sc.shape, F32)
        acc_sc[...] = jnp.zeros(acc_sc.shape, F32)

    q = q_ref[...]

    def q_rows(kv):
        q4 = jnp.concatenate([q[:, (kv * g_n + g) * NSA_HD:(kv * g_n + g + 1) * NSA_HD] for g in range(g_n)], axis=0)
        return jnp.concatenate([q4, jnp.zeros((LANE - g_n * rows, NSA_HD), F32)], axis=0).astype(BF16)

    def update(kv, s_t, mask_t, v):
        m_old = m_sc[kv]
        m_new = jnp.maximum(m_old, jnp.max(jnp.where(mask_t, s_t, NEG), axis=0, keepdims=True))
        alpha = jnp.exp(m_old - m_new)
        p_t = jnp.where(mask_t, jnp.exp(jnp.minimum(s_t - m_new, 0.0)), 0.0)
        l_sc[kv] = alpha * l_sc[kv] + jnp.sum(p_t, axis=0, keepdims=True)
        acc_sc[kv] = alpha * acc_sc[kv] + _dot_tn(v, p_t.astype(BF16))
        m_sc[kv] = m_new

    stride = 4 * NSA_KV
    for kv in range(NSA_KV):
        qk = q_rows(kv)
        k = jnp.concatenate([pg[pl.ds(2 * NSA_KV + kv, PAGE_SIZE, stride=stride), :] for pg in pages], axis=0)
        v = jnp.concatenate([pg[pl.ds(3 * NSA_KV + kv, PAGE_SIZE, stride=stride), :] for pg in pages], axis=0)
        s_t = _dot_nt(k.astype(BF16), qk) * scale
        start = pl.multiple_of(s_id * blocks_per_step, blocks_per_step)
        chunk = selt_ref[0, kv, pl.ds(start, blocks_per_step), :]
        mask_t = jnp.concatenate([jnp.broadcast_to(chunk[c:c + 1, :], (SEL_LEN, LANE))
                                  for c in range(blocks_per_step)], axis=0) > 0.5
        update(kv, s_t, mask_t, v.astype(BF16))

    @pl.when(s_id == n_steps - 1)
    def _():
        n_past_blocks = n_steps * blocks_per_step
        kvn = jnp.concatenate([kvn_ref[...], jnp.zeros((LANE - rows, 4 * NSA_HD), F32)], axis=0)
        gates = _sigmoid(ng_ref[...])
        glane = lax.broadcasted_iota(jnp.int32, gates.shape, 1)
        for kv in range(NSA_KV):
            k_new = kvn[:, kv * NSA_HD:(kv + 1) * NSA_HD].astype(BF16)
            v_new = kvn[:, (NSA_KV + kv) * NSA_HD:(NSA_KV + kv + 1) * NSA_HD].astype(BF16)
            s_t = _dot_nt(k_new, q_rows(kv)) * scale
            key = lax.broadcasted_iota(jnp.int32, (LANE, LANE), 0)
            tok = lax.broadcasted_iota(jnp.int32, (LANE, LANE), 1) & (rows - 1)
            blk_ok = selt_ref[0, kv, n_past_blocks:n_past_blocks + 1, :] > 0.5
            mask_t = (key <= tok) & (key < valid_rows) & blk_ok
            update(kv, s_t, mask_t, v_new)
            l = l_sc[kv]
            o_t = acc_sc[kv] / jnp.where(l > 0.0, l, 1.0)
            o_sel = jnp.transpose(o_t)
            for g in range(g_n):
                r = slice(g * rows, (g + 1) * rows)
                col = (kv * g_n + g) * 3 + 1
                gate = jnp.sum(jnp.where(glane == col, gates, 0.0), axis=-1, keepdims=True)
                o = op_ref[0, kv, r, :] + gate * o_sel[r]
                o_ref[:, (kv * g_n + g) * NSA_HD:(kv * g_n + g + 1) * NSA_HD] = o.astype(o_ref.dtype)


def _nsa_sample(zs, zc, cache4, page_table, win_past, pe, w1, w2, *, layer, nb, valid_rows, q0):
    n_pages = page_table.shape[1]
    assert n_pages % PAGES_PER_STEP == 0 and q0 == n_pages * PAGE_SIZE
    s_len = q0 + valid_rows
    n_c = (s_len - CMP_LEN) // CMP_STRIDE + 1
    n_s = -(-s_len // SEL_LEN)
    ng = n_pages * GROUPS_PER_PAGE
    assert n_c <= ng - 1
    nsp = _round_up(n_s, LANE)
    steps = n_pages // PAGES_PER_STEP
    w1b = w1.astype(BF16)
    w1r = w1b.reshape(2, CMP_LEN, NSA_HD, NSA_HD)
    w1ab = jnp.concatenate([w1r[:, :CMP_STRIDE], w1r[:, CMP_STRIDE:]], axis=-1)
    uv = _compress_sample_partial(cache4, page_table, w1ab, layer=layer, nb=nb, n_pages=n_pages)
    pef = jnp.broadcast_to(pe.reshape(2, 1, CMP_LEN * NSA_HD), (2, SUBLANE, CMP_LEN * NSA_HD)).astype(BF16)
    ov = _overlap_t(ng, nsp, n_c, n_s)
    kvw = NSA_GROUP * NSA_HD
    nk = A_NKV // NSA_HD
    n_before = win_past.shape[4]
    selt, o_part = pl.pallas_call(
        functools.partial(_s_sel_kernel, q0=q0, valid_rows=valid_rows, n_c=n_c, n_s=n_s),
        grid=(nb, NSA_KV),
        in_specs=[pl.BlockSpec((1, 2, 1, ng, 2 * NSA_HD), lambda b, kv: (b, 0, kv, 0, 0)),
                  pl.BlockSpec((2, SUBLANE, CMP_LEN * NSA_HD), lambda b, kv: (0, 0, 0)),
                  pl.BlockSpec((2, CMP_LEN * NSA_HD, NSA_HD), lambda b, kv: (0, 0, 0)),
                  pl.BlockSpec((2, NSA_HD, NSA_HD), lambda b, kv: (0, 0, 0)),
                  pl.BlockSpec((SLAB, kvw), lambda b, kv: (b, A_NQ // kvw + kv)),
                  pl.BlockSpec((SLAB, LANE), lambda b, kv: (b, C_NG // LANE)),
                  pl.BlockSpec((SLAB, NSA_HD), lambda b, kv: (b, nk + 8 + kv)),
                  pl.BlockSpec((SLAB, NSA_HD), lambda b, kv: (b, nk + 10 + kv)),
                  pl.BlockSpec((None, None, None, None, n_before, NSA_HD), lambda b, kv: (layer, b, 0, kv, 0, 0)),
                  pl.BlockSpec((None, None, None, None, n_before, NSA_HD), lambda b, kv: (layer, b, 1, kv, 0, 0)),
                  pl.BlockSpec((ng, nsp), lambda b, kv: (0, 0))],
        out_specs=[pl.BlockSpec((1, 1, nsp, LANE), lambda b, kv: (b, kv, 0, 0)),
                   pl.BlockSpec((1, 1, NSA_GROUP * SLAB, NSA_HD), lambda b, kv: (b, kv, 0, 0))],
        out_shape=[jax.ShapeDtypeStruct((nb, NSA_KV, nsp, LANE), F32),
                   jax.ShapeDtypeStruct((nb, NSA_KV, NSA_GROUP * SLAB, NSA_HD), F32)],
        compiler_params=_params(("parallel", "parallel"), VMEM_BIG),
        name="nsa_select_sample",
    )(uv, pef, w1b, w2.astype(BF16), zs, zc, zs, zs, win_past, win_past, ov)
    return pl.pallas_call(
        functools.partial(_s_attn_kernel, n_steps=steps, valid_rows=valid_rows),
        grid_spec=pltpu.PrefetchScalarGridSpec(
            num_scalar_prefetch=1,
            grid=(nb, steps),
            in_specs=_page_specs(layer, PAGES_PER_STEP)
            + [pl.BlockSpec((SLAB, NSA_HEADS * NSA_HD), lambda b, s, pt: (b, A_NQ // (NSA_HEADS * NSA_HD))),
               pl.BlockSpec((SLAB, 4 * NSA_HD), lambda b, s, pt: (b, (A_NKV + 4 * NSA_HD) // (4 * NSA_HD))),
               pl.BlockSpec((SLAB, LANE), lambda b, s, pt: (b, C_NG // LANE)),
               pl.BlockSpec((1, NSA_KV, nsp, LANE), lambda b, s, pt: (b, 0, 0, 0)),
               pl.BlockSpec((1, NSA_KV, NSA_GROUP * SLAB, NSA_HD), lambda b, s, pt: (b, 0, 0, 0))],
            out_specs=pl.BlockSpec((SLAB, W_MIX), lambda b, s, pt: (b, 0)),
            scratch_shapes=[pltpu.VMEM((NSA_KV, 1, LANE), F32), pltpu.VMEM((NSA_KV, 1, LANE), F32),
                            pltpu.VMEM((NSA_KV, NSA_HD, LANE), F32)],
        ),
        out_shape=jax.ShapeDtypeStruct((nb * SLAB, W_MIX), _act_dtype(SLAB)),
        compiler_params=_params(("parallel", "arbitrary"), VMEM_BIG),
        name="nsa_attend_sample",
    )(page_table, *([cache4] * PAGES_PER_STEP), zs, zs, zc, selt, o_part)


def _merge_kernel(a_ref, b_ref, c_ref, d_ref, w_ref, g0_ref, g1_ref, g2_ref, g3_ref, o_ref):
    acc = None
    for n, (x_ref, g_ref) in enumerate(((a_ref, g0_ref), (b_ref, g1_ref), (c_ref, g2_ref), (d_ref, g3_ref))):
        term = _sigmoid(g_ref[...]) * _dot(x_ref[...].astype(BF16), w_ref[n])
        acc = term if acc is None else acc + term
    o_ref[...] = acc.astype(o_ref.dtype)


def _merge(branches, w_branch, zb, *, layer):
    m = zb.shape[0]
    tm = _tile(m, 1024)
    tn = 512
    br = pl.BlockSpec((tm, W_MIX), lambda i, j: (i, 0))
    gz = lambda n: pl.BlockSpec((tm, tn), lambda i, j: (i, (B_GZ + n * D_MODEL) // tn + j))
    return pl.pallas_call(
        _merge_kernel,
        grid=(m // tm, D_MODEL // tn),
        in_specs=[br, br, br, br, pl.BlockSpec((None, N_BRANCH, W_MIX, tn), lambda i, j: (layer, 0, 0, j)),
                  gz(0), gz(1), gz(2), gz(3)],
        out_specs=pl.BlockSpec((tm, tn), lambda i, j: (i, j)),
        out_shape=jax.ShapeDtypeStruct((m, D_MODEL), BF16),
        compiler_params=_params(("parallel", "parallel"), VMEM_BIG),
        name="branch_merge",
    )(*branches, w_branch, zb, zb, zb, zb)


def kernel(x_prompt, x_sample, cache_nsa_kv, state_ret, state_win_kv, state_conv, page_table, norm1, w_in, ret_gn_w,
           ret_gn_b, gm_ln_w, gm_ln_b, gm_ws, gm_bs, nsa_pe, nsa_w1, nsa_w2, conv_w, conv_b, conv_ln_w, conv_ln_b,
           w_branch, w_out, norm2, w_up, w_down, final_norm):
    bp, t, d = x_prompt.shape
    bs, ts, _ = x_sample.shape
    assert d == D_MODEL and t % RET_CHUNK == 0 and ts <= SLAB and norm1.shape[0] == DEPTH
    n_pages = page_table.shape[1]
    past_len = n_pages * PAGE_SIZE
    conv_rows = 128

    assert w_in.shape[2] == _O_END and _O_NG + W_BLK <= _O_END
    w_in_t = jnp.swapaxes(w_in, 1, 2)
    w_branch_b = _cast_bf16(w_branch.reshape(DEPTH, N_BRANCH * W_MIX, d), rows=512).reshape(w_branch.shape)
    nsa_w1_b = nsa_w1.astype(BF16)
    nsa_w2_b = nsa_w2.astype(BF16)

    xp = x_prompt.reshape(bp * t, d)
    xs = jnp.pad(x_sample, ((0, 0), (0, SLAB - ts), (0, 0))).reshape(bs * SLAB, d)
    cache4 = cache_nsa_kv.reshape(DEPTH, cache_nsa_kv.shape[1], PAGE_ROWS, NSA_HD)
    win_t = jnp.transpose(state_win_kv, (0, 1, 3, 4, 2, 5))
    conv_pad = jnp.pad(state_conv, ((0, 0), (0, 0), (HALO - (CONV_W - 1), 0), (0, 0)))
    ret0_p = jnp.zeros((bp, RET_HEADS, RET_DK, RET_DV), F32)
    conv0_p = jnp.zeros((bp, HALO, W_MIX), F32)
    cos_p, sin_p = _rope_tables(np.arange(t), t)
    cos_s, sin_s = _rope_tables(past_len + np.arange(ts), SLAB)
    c_eff_s = math.gcd(ts, RET_CHUNK)
    assert c_eff_s == ts

    keep_p = min(WINDOW, t)
    keep_s = min(WINDOW, past_len + ts)
    ret_p, ret_s, kv_p, kv_s, win_p, win_s, conv_p, conv_s, gm_s = ([] for _ in range(9))
    for l in range(DEPTH):
        hp = _rmsnorm(xp, norm1[l], slab=1, valid=1, out_dtype=BF16)
        hs = _rmsnorm(xs, norm1[l], slab=SLAB, valid=ts, out_dtype=BF16)
        zap, zas = _dense(hp, hs, w_in_t, layer=l, n=N_Z, tn=1024, tn_head=W_BLK, w_rows=_w_in_source_row,
                          name="in_proj")
        zbp, zbs, zcp, zcs = zap, zas, zap, zas

        o_ret_p, r_p = _retention(zap, ret0_p, ret_gn_w[l], ret_gn_b[l], nb=bp, rows=RET_CHUNK,
                                  n_chunks=t // RET_CHUNK, c_eff=RET_CHUNK, cos=cos_p, sin=sin_p)
        (o_gm_p,) = _gmlp(zap, gm_ln_w[l], gm_ln_b[l], gm_ws[l], gm_bs[l], rows=GM_CHUNK, keep_v=False)
        cmp_kv = _compress_prompt(zap, nsa_pe[l], nsa_w1_b[l], nsa_w2_b[l], nb=bp, t=t)
        o_nsa_p = _nsa_prompt(zap, zcp, cmp_kv, nb=bp, t=t)
        o_conv_p, ext_p = _conv_module(zbp, conv0_p, conv_w[l], conv_b[l], conv_ln_w[l], conv_ln_b[l], nb=bp,
                                       rows=conv_rows, n_tiles=t // conv_rows)
        merged_p = _merge((o_ret_p, o_gm_p, o_nsa_p, o_conv_p), w_branch_b, zbp, layer=l)

        o_ret_s, r_s = _retention(zas, state_ret[l], ret_gn_w[l], ret_gn_b[l], nb=bs, rows=SLAB, n_chunks=1,
                                  c_eff=c_eff_s, cos=cos_s, sin=sin_s)
        o_gm_s, gv_s = _gmlp(zas, gm_ln_w[l], gm_ln_b[l], gm_ws[l], gm_bs[l], rows=SLAB, keep_v=True)
        o_nsa_s = _nsa_sample(zas, zcs, cache4, page_table, win_t, nsa_pe[l], nsa_w1[l], nsa_w2[l], layer=l, nb=bs,
                              valid_rows=ts, q0=past_len)
        o_conv_s, ext_s = _conv_module(zbs, conv_pad[l], conv_w[l], conv_b[l], conv_ln_w[l], conv_ln_b[l], nb=bs,
                                       rows=SLAB, n_tiles=1)
        merged_s = _merge((o_ret_s, o_gm_s, o_nsa_s, o_conv_s), w_branch_b, zbs, layer=l)

        xp, xs = _dense(merged_p, merged_s, w_out, layer=l, n=D_MODEL, tn=512, tn_head=512, resp=xp, ress=xs,
                        name="out_proj")
        h2p = _rmsnorm(xp, norm2[l], slab=1, valid=1, out_dtype=BF16)
        h2s = _rmsnorm(xs, norm2[l], slab=1, valid=1, out_dtype=BF16)
        up, us = _dense(h2p, h2s, w_up, layer=l, n=D_FF, tn=1024, tn_head=512, act="relu2", out_dtype=BF16,
                        name="mlp_up")
        xp, xs = _dense(up, us, w_down, layer=l, n=D_MODEL, tn=1024, tn_head=512, tk=2048, resp=xp, ress=xs,
                        name="mlp_down")

        ret_p.append(r_p)
        ret_s.append(r_s)
        z3p = zap.reshape(bp, t, N_Z)
        kv_p.append(z3p[:, :, A_NKV:A_NKV + 8 * NSA_HD].reshape(bp, t, 4, NSA_KV, NSA_HD))
        win_p.append(z3p[:, t - keep_p:, A_NKV + 8 * NSA_HD:A_NKV + 12 * NSA_HD].reshape(bp, keep_p, 2, NSA_KV, NSA_HD))
        z3s = zas.reshape(bs, SLAB, N_Z)[:, :ts]
        kv_s.append(z3s[:, :, A_NKV:A_NKV + 8 * NSA_HD].reshape(bs, ts, 4, NSA_KV, NSA_HD))
        new_win = z3s[:, :, A_NKV + 8 * NSA_HD:A_NKV + 12 * NSA_HD].reshape(bs, ts, 2, NSA_KV, NSA_HD)
        win_s.append(jnp.concatenate([state_win_kv[l], new_win], axis=1)[:, -keep_s:])
        conv_p.append(ext_p[:, HALO + conv_rows - (CONV_W - 1):HALO + conv_rows])
        conv_s.append(ext_s[:, HALO + ts - (CONV_W - 1):HALO + ts])
        gm_s.append(gv_s.reshape(bs, SLAB, W_MIX)[:, :ts])

    y_p = _rmsnorm(xp, final_norm, slab=1, valid=1, out_dtype=F32)
    y_s = _rmsnorm(xs, final_norm, slab=1, valid=1, out_dtype=F32)
    return (y_p.reshape(bp, t, d), y_s.reshape(bs, SLAB, d)[:, :ts],
            jnp.stack(ret_p), jnp.stack(ret_s), jnp.stack(kv_p), jnp.stack(kv_s), jnp.stack(win_p), jnp.stack(win_s),
            jnp.stack(conv_p), jnp.stack(conv_s), jnp.stack(gm_s))
```

```python
import functools
import math

import numpy as np
import jax
import jax.numpy as jnp
from jax import lax
from jax.experimental import pallas as pl
from jax.experimental.pallas import tpu as pltpu

F32 = jnp.float32
BF16 = jnp.bfloat16

D_MODEL = 4096
DEPTH = 2
PAGE_SIZE = 128
W_MIX = D_MODEL // 4
N_BRANCH = 4
RET_HEADS = 4
RET_DK = W_MIX // RET_HEADS
RET_DV = W_MIX // RET_HEADS
RET_CHUNK = 128
ROPE_BASE = 10000.0
GM_CHUNK = 128
GM_GROUPS = 4
NSA_HEADS = 8
NSA_KV = 2
NSA_HD = W_MIX // NSA_HEADS
NSA_GROUP = NSA_HEADS // NSA_KV
CMP_LEN = 32
CMP_STRIDE = 16
SEL_LEN = 64
SEL_SHIFT = 6
SEL_TOPK = 16
WINDOW = 512
CONV_W = 31
D_FF = 4 * D_MODEL
EPS = 1e-6
LN_EPS = 1e-5

LANE = 128
SUBLANE = 8
SLAB = 8
VMEM_BIG = 56 * 1024 * 1024

A_RQ, A_RK, A_RV, A_RG = 0, 1024, 2048, 3072
A_GU, A_GV = 4096, 5120
A_NQ = 6144
B_CA, B_CB, B_GZ = 7168, 8192, 9216
A_NKV = 25600
C_NG = 27136
N_Z = 27648
W_BLK = 512
_O_NKV, _O_NG, _O_CA, _O_END = 7168, 8704, 8728, 27160

NEG = -1e30


def _round_up(a, b):
    return -(-a // b) * b


def _tile(m, pref):
    best = None
    for t in range(SUBLANE, min(m, pref) + 1, SUBLANE):
        if m % t == 0:
            best = t
    assert best is not None, (m, pref)
    return best


def _act_dtype(rows):
    return BF16 if rows % (2 * SUBLANE) == 0 else F32


def _params(sem, vmem=None):
    return pltpu.CompilerParams(dimension_semantics=sem, vmem_limit_bytes=vmem)


def _gelu(x):
    return 0.5 * x * (1.0 + jnp.tanh(0.7978845608028654 * (x + 0.044715 * (x * x * x))))


def _sigmoid(x):
    return 1.0 / (1.0 + jnp.exp(-x))


def _layernorm(x, w, b):
    mu = jnp.mean(x, axis=-1, keepdims=True)
    xc = x - mu
    var = jnp.mean(xc * xc, axis=-1, keepdims=True)
    return xc * lax.rsqrt(var + LN_EPS) * w + b


def _softmax_lanes(s, mask):
    sm = jnp.where(mask, s, NEG)
    m = jnp.max(sm, axis=-1, keepdims=True)
    m = jnp.where(m > 0.5 * NEG, m, 0.0)
    e = jnp.where(mask, jnp.exp(sm - m), 0.0)
    den = jnp.sum(e, axis=-1, keepdims=True)
    return e / jnp.where(den > 0.0, den, 1.0)


def _softmax_parts(s, mask):
    sm = jnp.where(mask, s, NEG)
    m = jnp.max(sm, axis=-1, keepdims=True)
    m = jnp.where(m > 0.5 * NEG, m, 0.0)
    e = jnp.exp(sm - m)
    den = jnp.sum(e, axis=-1, keepdims=True)
    return e, 1.0 / jnp.where(den > 0.0, den, 1.0)


def _dot(a, b):
    return jnp.dot(a, b, preferred_element_type=F32)


def _dot_nt(a, b):
    return lax.dot_general(a, b, (((1,), (1,)), ((), ())), preferred_element_type=F32)


def _dot_tn(a, b):
    return lax.dot_general(a, b, (((0,), (0,)), ((), ())), preferred_element_type=F32)


def _dot_split3(a, b_bf16):
    a1 = a.astype(BF16)
    r1 = a - a1.astype(F32)
    a2 = r1.astype(BF16)
    a3 = (r1 - a2.astype(F32)).astype(BF16)
    return _dot(a1, b_bf16) + _dot(a2, b_bf16) + _dot(a3, b_bf16)


def _rmsnorm_kernel(x_ref, w_ref, o_ref, *, slab, valid):
    x = x_ref[...]
    y = x * lax.rsqrt(jnp.mean(x * x, axis=-1, keepdims=True) + EPS) * w_ref[...]
    if valid < slab:
        row = lax.broadcasted_iota(jnp.int32, y.shape, 0)
        y = jnp.where((row & (slab - 1)) < valid, y, 0.0)
    o_ref[...] = y.astype(o_ref.dtype)


def _rmsnorm(x, w, *, slab, valid, out_dtype):
    m, d = x.shape
    tr = _tile(m, 256)
    assert tr % slab == 0 or valid == slab
    return pl.pallas_call(
        functools.partial(_rmsnorm_kernel, slab=slab, valid=valid),
        grid=(m // tr,),
        in_specs=[pl.BlockSpec((tr, d), lambda i: (i, 0)), pl.BlockSpec((1, d), lambda i: (0, 0))],
        out_specs=pl.BlockSpec((tr, d), lambda i: (i, 0)),
        out_shape=jax.ShapeDtypeStruct((m, d), out_dtype),
        compiler_params=_params(("parallel",)),
        name="rmsnorm",
    )(x, w.reshape(1, d))


def _cast_kernel(x_ref, o_ref):
    o_ref[...] = x_ref[...].astype(o_ref.dtype)


def _cast_bf16(w, *, rows):
    depth, r, c = w.shape
    return pl.pallas_call(
        _cast_kernel,
        grid=(depth, r // rows),
        in_specs=[pl.BlockSpec((None, rows, c), lambda l, i: (l, i, 0))],
        out_specs=pl.BlockSpec((None, rows, c), lambda l, i: (l, i, 0)),
        out_shape=jax.ShapeDtypeStruct(w.shape, BF16),
        compiler_params=_params(("parallel", "parallel"), VMEM_BIG),
        name="cast_weight",
    )(w)


def _w_in_source_row(j):
    nq_blocks = _O_NKV // W_BLK
    gz_end = nq_blocks + (_O_END - _O_CA) // W_BLK
    nkv_end = gz_end + (_O_NG - _O_NKV) // W_BLK
    src = jnp.where(j < nq_blocks, j * W_BLK,
                    jnp.where(j < gz_end, _O_CA + (j - nq_blocks) * W_BLK,
                              jnp.where(j < nkv_end, _O_NKV + (j - gz_end) * W_BLK, _O_NG)))
    return pl.multiple_of(src, SUBLANE)


def _dense_kernel(*refs, act, has_res, kgrid, head, w_transposed):
    refs = list(refs)
    xp_ref = refs.pop(0)
    xs_ref = refs.pop(0) if head else None
    w_ref = refs.pop(0)
    rp_ref = refs.pop(0) if has_res else None
    rs_ref = refs.pop(0) if has_res and head else None
    if not head:
        refs.pop(0)
    op_ref = refs.pop(0)
    if head:
        os_ref, wo_ref = refs
        w = jnp.transpose(w_ref[0]) if w_transposed else w_ref[...]
        wb = w.astype(BF16)
        wo_ref[...] = wb
    else:
        wb = w_ref[...]

    def finish(acc, r_ref):
        if act == "relu2":
            acc = jnp.square(jnp.maximum(acc, 0.0))
        if r_ref is not None:
            acc = r_ref[...] + acc
        return acc

    if kgrid:
        @pl.when(pl.program_id(2) == 0)
        def _():
            op_ref[...] = rp_ref[...]
            if head:
                os_ref[...] = rs_ref[...]

        op_ref[...] += _dot(xp_ref[...], wb)
        if head:
            os_ref[...] += _dot(xs_ref[...], wb)
    else:
        op_ref[...] = finish(_dot(xp_ref[...], wb), rp_ref).astype(op_ref.dtype)
        if head:
            os_ref[...] = finish(_dot(xs_ref[...], wb), rs_ref).astype(os_ref.dtype)


def _dense(xp, xs, w, *, layer, n, tn, tn_head, tm=1024, tk=None, act=None, out_dtype=F32, resp=None, ress=None,
           w_rows=None, name="dense"):
    mp, k = xp.shape
    ms = xs.shape[0]
    tm = _tile(mp, tm)
    kgrid = tk is not None
    has_res = resp is not None
    assert n % tn == 0 and n % tn_head == 0 and (not kgrid or (has_res and act is None and k % tk == 0 and not w_rows))
    kb = tk if kgrid else k
    nk = k // kb
    wrap = (lambda f: f) if kgrid else (lambda f: (lambda i, j: f(i, j, 0)))
    sem = ("arbitrary",) * (3 if kgrid else 2)
    body = functools.partial(_dense_kernel, act=act, has_res=has_res, kgrid=kgrid, w_transposed=bool(w_rows))

    if w_rows:
        w_spec = pl.BlockSpec((pl.Element(1), pl.Element(tn_head), pl.Element(k)),
                              wrap(lambda i, j, kk: (layer, w_rows(j), 0)))
    else:
        w_spec = pl.BlockSpec((None, kb, tn_head), wrap(lambda i, j, kk: (layer, kk, j)))
    once = dict(pipeline_mode=pl.Buffered(1)) if not kgrid else {}
    in_specs = [pl.BlockSpec((tm, kb), wrap(lambda i, j, kk: (0, kk)), **once),
                pl.BlockSpec((ms, kb), wrap(lambda i, j, kk: (0, kk)), **once), w_spec]
    args = [xp, xs, w]
    if has_res:
        in_specs += [pl.BlockSpec((tm, tn_head), wrap(lambda i, j, kk: (0, j))),
                     pl.BlockSpec((ms, tn_head), wrap(lambda i, j, kk: (0, j)))]
        args += [resp, ress]
    op, os_, wb = pl.pallas_call(
        functools.partial(body, head=True),
        grid=(1, n // tn_head) + ((nk,) if kgrid else ()),
        in_specs=in_specs,
        out_specs=[pl.BlockSpec((tm, tn_head), wrap(lambda i, j, kk: (0, j))),
                   pl.BlockSpec((ms, tn_head), wrap(lambda i, j, kk: (0, j))),
                   pl.BlockSpec((kb, tn_head), wrap(lambda i, j, kk: (kk, j)))],
        out_shape=[jax.ShapeDtypeStruct((mp, n), out_dtype), jax.ShapeDtypeStruct((ms, n), out_dtype),
                   jax.ShapeDtypeStruct((k, n), BF16)],
        compiler_params=_params(sem, VMEM_BIG),
        name=name + "_head",
    )(*args)
    if mp == tm:
        return op, os_

    in_specs = [pl.BlockSpec((tm, kb), wrap(lambda i, j, kk: (i + 1, kk))),
                pl.BlockSpec((kb, tn), wrap(lambda i, j, kk: (kk, j)))]
    args = [xp, wb]
    if has_res:
        in_specs.append(pl.BlockSpec((tm, tn), wrap(lambda i, j, kk: (i + 1, j))))
        args.append(resp)
    in_specs.append(pl.BlockSpec(memory_space=pl.ANY))
    args.append(op)
    op = pl.pallas_call(
        functools.partial(body, head=False),
        grid=(mp // tm - 1, n // tn) + ((nk,) if kgrid else ()),
        in_specs=in_specs,
        out_specs=pl.BlockSpec((tm, tn), wrap(lambda i, j, kk: (i + 1, j))),
        out_shape=jax.ShapeDtypeStruct((mp, n), out_dtype),
        input_output_aliases={len(args) - 1: 0},
        compiler_params=_params(sem, VMEM_BIG),
        name=name + "_tail",
    )(*args)
    return op, os_


def _ret_kernel(q_ref, k_ref, v_ref, g_ref, cos_ref, sin_ref, dm_ref, qd_ref, kd_ref, cd_ref, gw_ref, gb_ref,
                r0_ref, o_ref, ro_ref, r_sc, *, rows):
    ci = pl.program_id(1)

    @pl.when(ci == 0)
    def _():
        r_sc[...] = r0_ref[0]

    cos = cos_ref[...]
    sin = sin_ref[...]
    half = RET_DK // 2

    def rot(x):
        x1, x2 = x[:, :half], x[:, half:]
        return jnp.concatenate([x1 * cos - x2 * sin, x2 * cos + x1 * sin], axis=-1)

    def pad(x):
        if rows == RET_CHUNK:
            return x
        return jnp.concatenate([x, jnp.zeros((RET_CHUNK - rows, x.shape[1]), x.dtype)], axis=0)

    for h in range(RET_HEADS):
        cols = slice(h * RET_DK, (h + 1) * RET_DK)
        q = pad(rot(q_ref[:, cols]))
        k = pad(rot(k_ref[:, cols]) * (RET_DK ** -0.5))
        vb = pad(v_ref[:, cols]).astype(BF16)
        r = r_sc[h]
        s = _dot_nt(q.astype(BF16), k.astype(BF16)) * dm_ref[h]
        o = _dot(s.astype(BF16), vb) + _dot((q * qd_ref[h]).astype(BF16), r.astype(BF16))
        r_new = cd_ref[h] * r + _dot_tn((k * kd_ref[h]).astype(BF16), vb)
        r_sc[h] = r_new
        ro_ref[0, h] = r_new
        o = o[:rows]
        mu = jnp.mean(o, axis=-1, keepdims=True)
        oc = o - mu
        var = jnp.mean(oc * oc, axis=-1, keepdims=True)
        y = oc * lax.rsqrt(var + LN_EPS) * gw_ref[:, cols] + gb_ref[:, cols]
        g = g_ref[:, cols]
        o_ref[:, cols] = (g * _sigmoid(g) * y).astype(o_ref.dtype)


def _ret_tables(c_eff):
    log_g = np.log1p(-np.exp2(-5.0 - np.arange(RET_HEADS, dtype=np.float64)))
    i = np.arange(RET_CHUNK, dtype=np.float64)
    live = i < c_eff
    diff = i[:, None] - i[None, :]
    dmask = np.where(diff >= 0, np.exp(np.maximum(diff, 0.0)[None] * log_g[:, None, None]), 0.0)
    dmask = dmask * (live[:, None] & live[None, :])[None]
    q_dec = np.exp((i[None, :] + 1.0) * log_g[:, None]) * live[None]
    k_dec = np.exp((c_eff - 1.0 - i)[None, :] * log_g[:, None]) * live[None]
    c_dec = np.exp(c_eff * log_g)
    bc = lambda a: np.broadcast_to(a[:, :, None], (RET_HEADS, RET_CHUNK, RET_DK))
    return (jnp.asarray(dmask, F32), jnp.asarray(bc(q_dec), F32), jnp.asarray(bc(k_dec), F32),
            jnp.asarray(np.broadcast_to(c_dec[:, None, None], (RET_HEADS, 1, RET_DK)), F32))


def _rope_tables(positions, rows):
    half = RET_DK // 2
    inv = ROPE_BASE ** (-np.arange(half, dtype=np.float64) / half)
    ang = np.asarray(positions, np.float64)[:, None] * inv[None, :]
    cos = np.zeros((rows, half)); sin = np.zeros((rows, half))
    cos[:len(positions)] = np.cos(ang); sin[:len(positions)] = np.sin(ang)
    return jnp.asarray(cos, F32), jnp.asarray(sin, F32)


def _retention(z, r0, gn_w, gn_b, *, nb, rows, n_chunks, c_eff, cos, sin):
    m = z.shape[0]
    dmask, q_dec, k_dec, c_dec = _ret_tables(c_eff)
    zspec = lambda off: pl.BlockSpec((rows, W_MIX), lambda b, c: (b * n_chunks + c, off // W_MIX))
    tab = pl.BlockSpec((rows, RET_DK // 2), lambda b, c: (c, 0))
    whole = lambda a: pl.BlockSpec(a.shape, lambda b, c: (0,) * a.ndim)
    state = pl.BlockSpec((1, RET_HEADS, RET_DK, RET_DV), lambda b, c: (b, 0, 0, 0))
    gw, gb = gn_w.reshape(1, W_MIX), gn_b.reshape(1, W_MIX)
    return pl.pallas_call(
        functools.partial(_ret_kernel, rows=rows),
        grid=(nb, n_chunks),
        in_specs=[zspec(A_RQ), zspec(A_RK), zspec(A_RV), zspec(A_RG), tab, tab,
                  whole(dmask), whole(q_dec), whole(k_dec), whole(c_dec), whole(gw), whole(gb), state],
        out_specs=[pl.BlockSpec((rows, W_MIX), lambda b, c: (b * n_chunks + c, 0)), state],
        out_shape=[jax.ShapeDtypeStruct((m, W_MIX), _act_dtype(rows)),
                   jax.ShapeDtypeStruct((nb, RET_HEADS, RET_DK, RET_DV), F32)],
        scratch_shapes=[pltpu.VMEM((RET_HEADS, RET_DK, RET_DV), F32)],
        compiler_params=_params(("parallel", "arbitrary")),
        name="retention",
    )(z, z, z, z, cos, sin, dmask, q_dec, k_dec, c_dec, gw, gb, r0)


def _gm_kernel(u_ref, v_ref, lw_ref, lb_ref, ws_ref, bst_ref, o_ref, *maybe_gv_ref, rows):
    u = _gelu(u_ref[...])
    v = _layernorm(_gelu(v_ref[...]), lw_ref[...], lb_ref[...])
    for gv_ref in maybe_gv_ref:
        gv_ref[...] = v
    if rows < GM_CHUNK:
        v = jnp.concatenate([v, jnp.zeros((GM_CHUNK - rows, v.shape[1]), F32)], axis=0)
    ri = lax.broadcasted_iota(jnp.int32, (GM_CHUNK, GM_CHUNK), 0)
    cj = lax.broadcasted_iota(jnp.int32, (GM_CHUNK, GM_CHUNK), 1)
    gw = W_MIX // GM_GROUPS
    for g in range(GM_GROUPS):
        wm = jnp.where(cj <= ri, ws_ref[g], 0.0).astype(BF16)
        s = _dot(wm, v[:, g * gw:(g + 1) * gw].astype(BF16)) + bst_ref[:, g:g + 1]
        o_ref[:, g * gw:(g + 1) * gw] = (u[:, g * gw:(g + 1) * gw] * s[:rows]).astype(o_ref.dtype)


def _gmlp(z, ln_w, ln_b, ws, bs, *, rows, keep_v):
    m = z.shape[0]
    row = lambda: pl.BlockSpec((1, W_MIX), lambda i: (0, 0))
    n_out = 2 if keep_v else 1
    return pl.pallas_call(
        functools.partial(_gm_kernel, rows=rows),
        grid=(m // rows,),
        in_specs=[pl.BlockSpec((rows, W_MIX), lambda i: (i, A_GU // W_MIX)),
                  pl.BlockSpec((rows, W_MIX), lambda i: (i, A_GV // W_MIX)),
                  row(), row(),
                  pl.BlockSpec((GM_GROUPS, GM_CHUNK, GM_CHUNK), lambda i: (0, 0, 0)),
                  pl.BlockSpec((GM_CHUNK, GM_GROUPS), lambda i: (0, 0))],
        out_specs=[pl.BlockSpec((rows, W_MIX), lambda i: (i, 0)), pl.BlockSpec((rows, W_MIX), lambda i: (i, 0))][:n_out],
        out_shape=[jax.ShapeDtypeStruct((m, W_MIX), _act_dtype(rows)), jax.ShapeDtypeStruct((m, W_MIX), F32)][:n_out],
        compiler_params=_params(("parallel",)),
        name="gmlp",
    )(z, z, ln_w.reshape(1, W_MIX), ln_b.reshape(1, W_MIX), ws, bs.T)


HALO = 32
CONV_RB = 64


def _conv_kernel(a_ref, b_ref, buf_ref, cw_ref, cb_ref, lw_ref, lb_ref, o_ref, ext_ref, ext_sc, y_sc, win_sc, *,
                 rows):
    ti = pl.program_id(1)

    @pl.when(ti == 0)
    def _():
        ext_sc[0:HALO, :] = buf_ref[0]

    @pl.when(ti > 0)
    def _():
        ext_sc[0:HALO, :] = ext_sc[rows:rows + HALO, :]

    ext_sc[HALO:HALO + rows, :] = a_ref[...] * _sigmoid(b_ref[...])
    rb = min(CONV_RB, rows)
    first = HALO - (CONV_W - 1)
    for cc in range(W_MIX // LANE):
        lanes = slice(cc * LANE, (cc + 1) * LANE)
        for r0 in range(0, rows, rb):
            acc = jnp.broadcast_to(cb_ref[:, lanes], (rb, LANE))
            for s in range(SUBLANE):
                taps = [w for w in range(CONV_W) if (first + w) % SUBLANE == s]
                if taps:
                    start = first + taps[0] + r0
                    span = rb + SUBLANE * (len(taps) - 1)
                    win_sc[0:span, :] = ext_sc[start:start + span, lanes]
                    for k, w in enumerate(taps):
                        acc = acc + win_sc[SUBLANE * k:SUBLANE * k + rb, :] * cw_ref[w:w + 1, lanes]
            y_sc[r0:r0 + rb, lanes] = acc
    y = _layernorm(y_sc[...], lw_ref[...], lb_ref[...])
    o_ref[...] = (y * _sigmoid(y)).astype(o_ref.dtype)
    ext_ref[0] = ext_sc[...]


def _conv_module(z, buf, cw, cb, ln_w, ln_b, *, nb, rows, n_tiles):
    m = z.shape[0]
    row = lambda: pl.BlockSpec((1, W_MIX), lambda b, t: (0, 0))
    cwp = jnp.concatenate([cw, jnp.zeros((HALO - CONV_W, W_MIX), F32)], axis=0)
    return pl.pallas_call(
        functools.partial(_conv_kernel, rows=rows),
        grid=(nb, n_tiles),
        in_specs=[pl.BlockSpec((rows, W_MIX), lambda b, t: (b * n_tiles + t, B_CA // W_MIX)),
                  pl.BlockSpec((rows, W_MIX), lambda b, t: (b * n_tiles + t, B_CB // W_MIX)),
                  pl.BlockSpec((1, HALO, W_MIX), lambda b, t: (b, 0, 0)),
                  pl.BlockSpec((HALO, W_MIX), lambda b, t: (0, 0)),
                  row(), row(), row()],
        out_specs=[pl.BlockSpec((rows, W_MIX), lambda b, t: (b * n_tiles + t, 0)),
                   pl.BlockSpec((1, HALO + rows, W_MIX), lambda b, t: (b, 0, 0))],
        out_shape=[jax.ShapeDtypeStruct((m, W_MIX), _act_dtype(rows)),
                   jax.ShapeDtypeStruct((nb, HALO + rows, W_MIX), F32)],
        scratch_shapes=[pltpu.VMEM((HALO + rows, W_MIX), F32), pltpu.VMEM((rows, W_MIX), F32),
                        pltpu.VMEM((min(CONV_RB, rows) + HALO - SUBLANE, LANE), F32)],
        compiler_params=_params(("parallel", "arbitrary")),
        name="conv_module",
    )(z, z, buf, cwp, cb.reshape(1, W_MIX), ln_w.reshape(1, W_MIX), ln_b.reshape(1, W_MIX))


def _overlap_t(n_c_pad, n_s_pad, n_c, n_s):
    cs = np.arange(n_c_pad)[:, None] * CMP_STRIDE
    ss = np.arange(n_s_pad)[None, :] * SEL_LEN
    ov = np.clip(np.minimum(ss + SEL_LEN, cs + CMP_LEN) - np.maximum(ss, cs), 0, None).astype(np.float64)
    ov = ov * (np.arange(n_c_pad)[:, None] < n_c) * (np.arange(n_s_pad)[None, :] < n_s)
    return jnp.asarray(ov, BF16)


def _cmp_p_kernel(x_ref, pe_ref, w1_ref, w2_ref, o_ref, xs_sc, *, t, ncp):
    xs_sc[0:t, :] = x_ref[...]
    xs_sc[t:, :] = jnp.zeros((xs_sc.shape[0] - t, NSA_HD), F32)
    acc = jnp.zeros((ncp, NSA_HD), F32)
    for l in range(CMP_LEN):
        rows = xs_sc[pl.ds(l, ncp, stride=CMP_STRIDE), :] + pe_ref[0, l:l + 1, :]
        acc = acc + _dot(rows.astype(BF16), w1_ref[0, l * NSA_HD:(l + 1) * NSA_HD, :])
    o_ref[0, 0] = _dot(_gelu(acc).astype(BF16), w2_ref[0])


def _compress_prompt(z, pe, w1, w2, *, nb, t):
    ncp = _round_up(t // CMP_STRIDE, LANE)
    pad_rows = _round_up(CMP_STRIDE * (ncp - 1) + CMP_LEN, SUBLANE)
    return pl.pallas_call(
        functools.partial(_cmp_p_kernel, t=t, ncp=ncp),
        grid=(nb, 4),
        in_specs=[pl.BlockSpec((t, NSA_HD), lambda b, j: (b, A_NKV // NSA_HD + j)),
                  pl.BlockSpec((1, CMP_LEN, NSA_HD), lambda b, j: (j // 2, 0, 0)),
                  pl.BlockSpec((1, CMP_LEN * NSA_HD, NSA_HD), lambda b, j: (j // 2, 0, 0)),
                  pl.BlockSpec((1, NSA_HD, NSA_HD), lambda b, j: (j // 2, 0, 0))],
        out_specs=pl.BlockSpec((1, 1, ncp, NSA_HD), lambda b, j: (b, j, 0, 0)),
        out_shape=jax.ShapeDtypeStruct((nb, 4, ncp, NSA_HD), F32),
        scratch_shapes=[pltpu.VMEM((max(pad_rows, t + SUBLANE), NSA_HD), F32)],
        compiler_params=_params(("parallel", "parallel")),
        name="nsa_compress_prompt",
    )(z, pe, w1, w2)


def _nsa_p_kernel(q_ref, kc_ref, vc_ref, ks_ref, vs_ref, kw_ref, vw_ref, ng_ref, ov_ref, ex_ref, o_ref, os_sc,
                  *, t, tq, n_c, n_s, span, key_limits):
    qi = pl.program_id(2)
    kv = pl.program_id(1)
    scale = NSA_HD ** -0.5
    g_n = NSA_GROUP
    q = q_ref[...]
    qh = [q[:, g * NSA_HD:(g + 1) * NSA_HD].astype(BF16) for g in range(g_n)]
    q4 = jnp.concatenate(qh, axis=0)
    qpos = qi * tq + lax.broadcasted_iota(jnp.int32, (tq, 1), 0)
    qpos4 = jnp.concatenate([qpos] * g_n, axis=0)

    kc = kc_ref[0, 0].astype(BF16)
    ncp = kc.shape[0]
    s_c = _dot_nt(q4, kc) * scale
    cidx = lax.broadcasted_iota(jnp.int32, (1, ncp), 1)
    mask_c = (cidx * CMP_STRIDE + CMP_LEN - 1 <= qpos4) & (cidx < n_c)
    p_c = _softmax_lanes(s_c, mask_c)
    o_c = _dot(p_c.astype(BF16), vc_ref[0, 0].astype(BF16))

    p_sum = p_c[0:tq]
    for g in range(1, g_n):
        p_sum = p_sum + p_c[g * tq:(g + 1) * tq]
    imp = _dot_split3(p_sum, ov_ref[...])
    nsp = imp.shape[1]
    blk = lax.broadcasted_iota(jnp.int32, (tq, nsp), 1)
    cur = jnp.right_shift(qpos, SEL_SHIFT)
    forced = (blk == 0) | (blk == cur) | (blk == cur - 1)
    valid = (blk * SEL_LEN <= qpos) & (blk < n_s)
    score = jnp.where(valid, jnp.where(forced, jnp.inf, imp), -jnp.inf)
    live = _round_up(n_s, SUBLANE)
    score_t = jnp.transpose(score)[0:live]
    blk_t = lax.broadcasted_iota(jnp.int32, (live, tq), 0)
    rank_t = jnp.zeros((live, tq), F32)
    for j in range(n_s):
        sj = score_t[j:j + 1, :]
        ahead = (sj > score_t) | ((sj == score_t) & (blk_t > j))
        rank_t = rank_t + jnp.where(ahead, 1.0, 0.0)
    sel_t = jnp.where((score_t > -jnp.inf) & (rank_t < float(min(SEL_TOPK, n_s))), 1.0, 0.0)
    sel_tb = jnp.concatenate([sel_t, jnp.zeros((nsp - live, tq), F32)], axis=0).astype(BF16)

    for lo, hi, n_keys in key_limits:
        @pl.when((qi >= lo) & (qi < hi))
        def _(n_keys=n_keys):
            kidx = lax.broadcasted_iota(jnp.int32, (1, n_keys), 1)
            mask_s = (_dot_tn(sel_tb, ex_ref[:, 0:n_keys]) > 0.5) & (kidx <= qpos)
            kb = ks_ref[0:n_keys, :].astype(BF16)
            vb = vs_ref[0:n_keys, :].astype(BF16)
            for g in range(g_n):
                e, inv = _softmax_parts(_dot_nt(qh[g], kb) * scale, mask_s)
                os_sc[g * tq:(g + 1) * tq, :] = _dot(e.astype(BF16), vb) * inv

    o_s = os_sc[...]

    w0 = pl.multiple_of(jnp.clip(qi * tq - WINDOW, 0, t - span), LANE)
    kw = kw_ref[pl.ds(w0, span), :].astype(BF16)
    vw = vw_ref[pl.ds(w0, span), :].astype(BF16)
    kpos = w0 + lax.broadcasted_iota(jnp.int32, (1, span), 1)
    dist = qpos4 - kpos
    mask_w = (dist >= 0) & (dist <= WINDOW)
    e_w, inv_w = _softmax_parts(_dot_nt(q4, kw) * scale, mask_w)
    o_w = _dot(e_w.astype(BF16), vw) * inv_w

    gates = _sigmoid(ng_ref[...])
    for g in range(g_n):
        rows = slice(g * tq, (g + 1) * tq)
        col = (kv * g_n + g) * 3
        lane = lax.broadcasted_iota(jnp.int32, gates.shape, 1)
        pick = lambda j: jnp.sum(jnp.where(lane == col + j, gates, 0.0), axis=-1, keepdims=True)
        o = pick(0) * o_c[rows] + pick(1) * o_s[rows] + pick(2) * o_w[rows]
        o_ref[:, g * NSA_HD:(g + 1) * NSA_HD] = o.astype(o_ref.dtype)


KEY_PREFIX_VARIANTS = 4


def _nsa_prompt(z, zc, cmp_kv, *, nb, t):
    m = z.shape[0]
    tq = 128
    nq = t // tq
    ends = sorted({-(-v * nq // KEY_PREFIX_VARIANTS) for v in range(1, KEY_PREFIX_VARIANTS + 1)})
    key_limits = tuple((lo, hi, hi * tq) for lo, hi in zip([0] + ends[:-1], ends))
    n_c = (t - CMP_LEN) // CMP_STRIDE + 1
    n_s = -(-t // SEL_LEN)
    ncp = cmp_kv.shape[2]
    nsp = LANE
    assert n_s <= nsp
    span = min(WINDOW + tq, t)
    ov = _overlap_t(ncp, nsp, n_c, n_s)
    ex = jnp.asarray((np.arange(t)[None, :] // SEL_LEN) == np.arange(nsp)[:, None], BF16)
    kvw = NSA_GROUP * NSA_HD
    nk = A_NKV // NSA_HD
    full = lambda j0: pl.BlockSpec((t, NSA_HD), lambda b, kv, qi: (b, nk + j0 + kv))
    return pl.pallas_call(
        functools.partial(_nsa_p_kernel, t=t, tq=tq, n_c=n_c, n_s=n_s, span=span, key_limits=key_limits),
        grid=(nb, NSA_KV, nq),
        in_specs=[pl.BlockSpec((tq, kvw), lambda b, kv, qi: (b * nq + qi, A_NQ // kvw + kv)),
                  pl.BlockSpec((1, 1, ncp, NSA_HD), lambda b, kv, qi: (b, kv, 0, 0)),
                  pl.BlockSpec((1, 1, ncp, NSA_HD), lambda b, kv, qi: (b, 2 + kv, 0, 0)),
                  full(4), full(6), full(8), full(10),
                  pl.BlockSpec((tq, LANE), lambda b, kv, qi: (b * nq + qi, C_NG // LANE)),
                  pl.BlockSpec((ncp, nsp), lambda b, kv, qi: (0, 0)),
                  pl.BlockSpec((nsp, t), lambda b, kv, qi: (0, 0))],
        out_specs=pl.BlockSpec((tq, kvw), lambda b, kv, qi: (b * nq + qi, kv)),
        out_shape=jax.ShapeDtypeStruct((m, W_MIX), BF16),
        scratch_shapes=[pltpu.VMEM((NSA_GROUP * tq, NSA_HD), F32)],
        compiler_params=_params(("parallel", "parallel", "arbitrary"), VMEM_BIG),
        name="nsa_prompt",
    )(z, cmp_kv, cmp_kv, z, z, z, z, zc, ov, ex)


PAGES_PER_STEP = 16
PAGE_ROWS = PAGE_SIZE * 4 * NSA_KV
GROUPS_PER_PAGE = PAGE_SIZE // CMP_STRIDE


def _s_cmp_kernel(pt_ref, *refs):
    del pt_ref
    pages = refs[:PAGES_PER_STEP]
    w_ref, o_ref = refs[PAGES_PER_STEP], refs[PAGES_PER_STEP + 1]
    gp = PAGES_PER_STEP * GROUPS_PER_PAGE
    for which in range(2):
        acc = jnp.zeros((NSA_KV * gp, 2 * NSA_HD), F32)
        for l in range(CMP_STRIDE):
            pieces = [pages[p][pl.ds(l * 4 * NSA_KV + which * NSA_KV + kv, GROUPS_PER_PAGE, stride=CMP_STRIDE * 4 * NSA_KV), :]
                      for kv in range(NSA_KV) for p in range(PAGES_PER_STEP)]
            acc = acc + _dot(jnp.concatenate(pieces, axis=0).astype(BF16), w_ref[which, l])
        for kv in range(NSA_KV):
            o_ref[0, which, kv] = acc[kv * gp:(kv + 1) * gp]


def _page_specs(layer, n):
    def spec(p):
        return pl.BlockSpec((None, None, PAGE_ROWS, NSA_HD),
                            lambda b, s, pt: (layer, pt[b, s * PAGES_PER_STEP + p], 0, 0))
    return [spec(p) for p in range(n)]


def _compress_sample_partial(cache4, page_table, w1ab, *, layer, nb, n_pages):
    steps = n_pages // PAGES_PER_STEP
    gp = PAGES_PER_STEP * GROUPS_PER_PAGE
    ng = n_pages * GROUPS_PER_PAGE
    return pl.pallas_call(
        _s_cmp_kernel,
        grid_spec=pltpu.PrefetchScalarGridSpec(
            num_scalar_prefetch=1,
            grid=(nb, steps),
            in_specs=_page_specs(layer, PAGES_PER_STEP)
            + [pl.BlockSpec((2, CMP_STRIDE, NSA_HD, 2 * NSA_HD), lambda b, s, pt: (0, 0, 0, 0))],
            out_specs=pl.BlockSpec((1, 2, NSA_KV, gp, 2 * NSA_HD), lambda b, s, pt: (b, 0, 0, s, 0)),
        ),
        out_shape=jax.ShapeDtypeStruct((nb, 2, NSA_KV, ng, 2 * NSA_HD), F32),
        compiler_params=_params(("parallel", "arbitrary"), VMEM_BIG),
        name="nsa_compress_sample",
    )(page_table, *([cache4] * PAGES_PER_STEP), w1ab)


def _s_sel_kernel(uv_ref, pef_ref, w1_ref, w2_ref, q_ref, ng_ref, kwn_ref, vwn_ref, kwp_ref, vwp_ref, ov_ref,
                  selt_ref, op_ref, *, q0, valid_rows, n_c, n_s):
    kv = pl.program_id(1)
    scale = NSA_HD ** -0.5
    g_n = NSA_GROUP
    rows = SLAB
    q = q_ref[...]
    q4 = jnp.concatenate([q[:, g * NSA_HD:(g + 1) * NSA_HD] for g in range(g_n)], axis=0).astype(BF16)
    tpos = lax.broadcasted_iota(jnp.int32, (rows, 1), 0)
    qpos = q0 + tpos
    qpos4 = jnp.concatenate([qpos] * g_n, axis=0)

    def compressed(which):
        uv = uv_ref[0, which, 0]
        ng = uv.shape[0]
        u = uv[:, :NSA_HD]
        v_next = pltpu.roll(uv[:, NSA_HD:], ng - 1, 0)
        const = _dot(pef_ref[which], w1_ref[which])[0:1]
        return _dot(_gelu(u + v_next + const).astype(BF16), w2_ref[which])

    kc = compressed(0)
    vc = compressed(1)
    ncp = kc.shape[0]
    s_c = _dot_nt(q4, kc.astype(BF16)) * scale
    cidx = lax.broadcasted_iota(jnp.int32, (1, ncp), 1)
    mask_c = (cidx * CMP_STRIDE + CMP_LEN - 1 <= qpos4) & (cidx < n_c)
    p_c = _softmax_lanes(s_c, mask_c)
    o_c = _dot(p_c.astype(BF16), vc.astype(BF16))

    p_sum = p_c[0:rows]
    for g in range(1, g_n):
        p_sum = p_sum + p_c[g * rows:(g + 1) * rows]
    imp = _dot_split3(p_sum, ov_ref[...])
    nsp = imp.shape[1]
    blk = lax.broadcasted_iota(jnp.int32, (rows, nsp), 1)
    cur = jnp.right_shift(qpos, SEL_SHIFT)
    forced = (blk == 0) | (blk == cur) | (blk == cur - 1)
    valid = (blk * SEL_LEN <= qpos) & (blk < n_s)
    score = jnp.where(valid, jnp.where(forced, jnp.inf, imp), -jnp.inf)
    score_pad = jnp.concatenate([score, jnp.zeros((LANE - rows, nsp), F32)], axis=0)
    score_t = jnp.transpose(score_pad)
    bi = lax.broadcasted_iota(jnp.int32, (nsp, nsp), 0)
    bj = lax.broadcasted_iota(jnp.int32, (nsp, nsp), 1)
    lane = lax.broadcasted_iota(jnp.int32, (nsp, LANE), 1)
    sel_t = jnp.zeros((nsp, LANE), F32)
    k_take = float(min(SEL_TOPK, n_s))
    for tt in range(valid_rows):
        s_i = score_t[:, tt:tt + 1]
        s_j = score[tt:tt + 1, :]
        ahead = (s_j > s_i) | ((s_j == s_i) & (bj < bi))
        rank_i = jnp.sum(jnp.where(ahead, 1.0, 0.0), axis=-1, keepdims=True)
        ok_i = (rank_i < k_take) & (s_i > -jnp.inf)
        sel_t = jnp.where(((lane & (rows - 1)) == tt) & (lane < g_n * rows) & ok_i, 1.0, sel_t)
    selt_ref[0, 0] = sel_t

    kw = jnp.concatenate([kwp_ref[...], kwn_ref[...]], axis=0).astype(BF16)
    vw = jnp.concatenate([vwp_ref[...], vwn_ref[...]], axis=0).astype(BF16)
    n_before = kwp_ref.shape[0]
    widx = lax.broadcasted_iota(jnp.int32, (1, n_before + rows), 1)
    kpos = jnp.where(widx < n_before, q0 - n_before + widx, q0 + widx - n_before)
    dist = qpos4 - kpos
    mask_w = (dist >= 0) & (dist <= WINDOW) & (kpos >= 0)
    s_w = _dot_nt(q4, kw) * scale
    p_w = _softmax_lanes(s_w, mask_w)
    o_w = _dot(p_w.astype(BF16), vw)

    gates = _sigmoid(ng_ref[...])
    glane = lax.broadcasted_iota(jnp.int32, gates.shape, 1)
    for g in range(g_n):
        r = slice(g * rows, (g + 1) * rows)
        col = (kv * g_n + g) * 3
        pick = lambda j: jnp.sum(jnp.where(glane == col + j, gates, 0.0), axis=-1, keepdims=True)
        op_ref[0, 0, r, :] = pick(0) * o_c[r] + pick(2) * o_w[r]


def _s_attn_kernel(pt_ref, *refs, n_steps, valid_rows):
    del pt_ref
    pages = refs[:PAGES_PER_STEP]
    q_ref, kvn_ref, ng_ref, selt_ref, op_ref, o_ref, m_sc, l_sc, acc_sc = refs[PAGES_PER_STEP:]
    s_id = pl.program_id(1)
    scale = NSA_HD ** -0.5
    g_n = NSA_GROUP
    rows = SLAB
    blocks_per_step = PAGES_PER_STEP * PAGE_SIZE // SEL_LEN

    @pl.when(s_id == 0)
    def _():
        m_sc[...] = jnp.full(m_sc.shape, NEG, F32)
        l_sc[...] = jnp.zeros(l_sc.shape, F32)
        acc_sc[...] = jnp.zeros(acc_sc.shape, F32)

    q = q_ref[...]

    def q_rows(kv):
        q4 = jnp.concatenate([q[:, (kv * g_n + g) * NSA_HD:(kv * g_n + g + 1) * NSA_HD] for g in range(g_n)], axis=0)
        return jnp.concatenate([q4, jnp.zeros((LANE - g_n * rows, NSA_HD), F32)], axis=0).astype(BF16)

    def update(kv, s_t, mask_t, v):
        m_old = m_sc[kv]
        m_new = jnp.maximum(m_old, jnp.max(jnp.where(mask_t, s_t, NEG), axis=0, keepdims=True))
        alpha = jnp.exp(m_old - m_new)
        p_t = jnp.where(mask_t, jnp.exp(jnp.minimum(s_t - m_new, 0.0)), 0.0)
        l_sc[kv] = alpha * l_sc[kv] + jnp.sum(p_t, axis=0, keepdims=True)
        acc_sc[kv] = alpha * acc_sc[kv] + _dot_tn(v, p_t.astype(BF16))
        m_sc[kv] = m_new

    stride = 4 * NSA_KV
    for kv in range(NSA_KV):
        qk = q_rows(kv)
        k = jnp.concatenate([pg[pl.ds(2 * NSA_KV + kv, PAGE_SIZE, stride=stride), :] for pg in pages], axis=0)
        v = jnp.concatenate([pg[pl.ds(3 * NSA_KV + kv, PAGE_SIZE, stride=stride), :] for pg in pages], axis=0)
        s_t = _dot_nt(k.astype(BF16), qk) * scale
        start = pl.multiple_of(s_id * blocks_per_step, blocks_per_step)
        chunk = selt_ref[0, kv, pl.ds(start, blocks_per_step), :]
        mask_t = jnp.concatenate([jnp.broadcast_to(chunk[c:c + 1, :], (SEL_LEN, LANE))
                                  for c in range(blocks_per_step)], axis=0) > 0.5
        update(kv, s_t, mask_t, v.astype(BF16))

    @pl.when(s_id == n_steps - 1)
    def _():
        n_past_blocks = n_steps * blocks_per_step
        kvn = jnp.concatenate([kvn_ref[...], jnp.zeros((LANE - rows, 4 * NSA_HD), F32)], axis=0)
        gates = _sigmoid(ng_ref[...])
        glane = lax.broadcasted_iota(jnp.int32, gates.shape, 1)
        for kv in range(NSA_KV):
            k_new = kvn[:, kv * NSA_HD:(kv + 1) * NSA_HD].astype(BF16)
            v_new = kvn[:, (NSA_KV + kv) * NSA_HD:(NSA_KV + kv + 1) * NSA_HD].astype(BF16)
            s_t = _dot_nt(k_new, q_rows(kv)) * scale
            key = lax.broadcasted_iota(jnp.int32, (LANE, LANE), 0)
            tok = lax.broadcasted_iota(jnp.int32, (LANE, LANE), 1) & (rows - 1)
            blk_ok = selt_ref[0, kv, n_past_blocks:n_past_blocks + 1, :] > 0.5
            mask_t = (key <= tok) & (key < valid_rows) & blk_ok
            update(kv, s_t, mask_t, v_new)
            l = l_sc[kv]
            o_t = acc_sc[kv] / jnp.where(l > 0.0, l, 1.0)
            o_sel = jnp.transpose(o_t)
            for g in range(g_n):
                r = slice(g * rows, (g + 1) * rows)
                col = (kv * g_n + g) * 3 + 1
                gate = jnp.sum(jnp.where(glane == col, gates, 0.0), axis=-1, keepdims=True)
                o = op_ref[0, kv, r, :] + gate * o_sel[r]
                o_ref[:, (kv * g_n + g) * NSA_HD:(kv * g_n + g + 1) * NSA_HD] = o.astype(o_ref.dtype)


def _nsa_sample(zs, zc, cache4, page_table, win_past, pe, w1, w2, *, layer, nb, valid_rows, q0):
    n_pages = page_table.shape[1]
    assert n_pages % PAGES_PER_STEP == 0 and q0 == n_pages * PAGE_SIZE
    s_len = q0 + valid_rows
    n_c = (s_len - CMP_LEN) // CMP_STRIDE + 1
    n_s = -(-s_len // SEL_LEN)
    ng = n_pages * GROUPS_PER_PAGE
    assert n_c <= ng - 1
    nsp = _round_up(n_s, LANE)
    steps = n_pages // PAGES_PER_STEP
    w1b = w1.astype(BF16)
    w1r = w1b.reshape(2, CMP_LEN, NSA_HD, NSA_HD)
    w1ab = jnp.concatenate([w1r[:, :CMP_STRIDE], w1r[:, CMP_STRIDE:]], axis=-1)
    uv = _compress_sample_partial(cache4, page_table, w1ab, layer=layer, nb=nb, n_pages=n_pages)
    pef = jnp.broadcast_to(pe.reshape(2, 1, CMP_LEN * NSA_HD), (2, SUBLANE, CMP_LEN * NSA_HD)).astype(BF16)
    ov = _overlap_t(ng, nsp, n_c, n_s)
    kvw = NSA_GROUP * NSA_HD
    nk = A_NKV // NSA_HD
    n_before = win_past.shape[4]
    selt, o_part = pl.pallas_call(
        functools.partial(_s_sel_kernel, q0=q0, valid_rows=valid_rows, n_c=n_c, n_s=n_s),
        grid=(nb, NSA_KV),
        in_specs=[pl.BlockSpec((1, 2, 1, ng, 2 * NSA_HD), lambda b, kv: (b, 0, kv, 0, 0)),
                  pl.BlockSpec((2, SUBLANE, CMP_LEN * NSA_HD), lambda b, kv: (0, 0, 0)),
                  pl.BlockSpec((2, CMP_LEN * NSA_HD, NSA_HD), lambda b, kv: (0, 0, 0)),
                  pl.BlockSpec((2, NSA_HD, NSA_HD), lambda b, kv: (0, 0, 0)),
                  pl.BlockSpec((SLAB, kvw), lambda b, kv: (b, A_NQ // kvw + kv)),
                  pl.BlockSpec((SLAB, LANE), lambda b, kv: (b, C_NG // LANE)),
                  pl.BlockSpec((SLAB, NSA_HD), lambda b, kv: (b, nk + 8 + kv)),
                  pl.BlockSpec((SLAB, NSA_HD), lambda b, kv: (b, nk + 10 + kv)),
                  pl.BlockSpec((None, None, None, None, n_before, NSA_HD), lambda b, kv: (layer, b, 0, kv, 0, 0)),
                  pl.BlockSpec((None, None, None, None, n_before, NSA_HD), lambda b, kv: (layer, b, 1, kv, 0, 0)),
                  pl.BlockSpec((ng, nsp), lambda b, kv: (0, 0))],
        out_specs=[pl.BlockSpec((1, 1, nsp, LANE), lambda b, kv: (b, kv, 0, 0)),
                   pl.BlockSpec((1, 1, NSA_GROUP * SLAB, NSA_HD), lambda b, kv: (b, kv, 0, 0))],
        out_shape=[jax.ShapeDtypeStruct((nb, NSA_KV, nsp, LANE), F32),
                   jax.ShapeDtypeStruct((nb, NSA_KV, NSA_GROUP * SLAB, NSA_HD), F32)],
        compiler_params=_params(("parallel", "parallel"), VMEM_BIG),
        name="nsa_select_sample",
    )(uv, pef, w1b, w2.astype(BF16), zs, zc, zs, zs, win_past, win_past, ov)
    return pl.pallas_call(
        functools.partial(_s_attn_kernel, n_steps=steps, valid_rows=valid_rows),
        grid_spec=pltpu.PrefetchScalarGridSpec(
            num_scalar_prefetch=1,
            grid=(nb, steps),
            in_specs=_page_specs(layer, PAGES_PER_STEP)
            + [pl.BlockSpec((SLAB, NSA_HEADS * NSA_HD), lambda b, s, pt: (b, A_NQ // (NSA_HEADS * NSA_HD))),
               pl.BlockSpec((SLAB, 4 * NSA_HD), lambda b, s, pt: (b, (A_NKV + 4 * NSA_HD) // (4 * NSA_HD))),
               pl.BlockSpec((SLAB, LANE), lambda b, s, pt: (b, C_NG // LANE)),
               pl.BlockSpec((1, NSA_KV, nsp, LANE), lambda b, s, pt: (b, 0, 0, 0)),
               pl.BlockSpec((1, NSA_KV, NSA_GROUP * SLAB, NSA_HD), lambda b, s, pt: (b, 0, 0, 0))],
            out_specs=pl.BlockSpec((SLAB, W_MIX), lambda b, s, pt: (b, 0)),
            scratch_shapes=[pltpu.VMEM((NSA_KV, 1, LANE), F32), pltpu.VMEM((NSA_KV, 1, LANE), F32),
                            pltpu.VMEM((NSA_KV, NSA_HD, LANE), F32)],
        ),
        out_shape=jax.ShapeDtypeStruct((nb * SLAB, W_MIX), _act_dtype(SLAB)),
        compiler_params=_params(("parallel", "arbitrary"), VMEM_BIG),
        name="nsa_attend_sample",
    )(page_table, *([cache4] * PAGES_PER_STEP), zs, zs, zc, selt, o_part)


KV_ROW_COMPS = 4 * NSA_KV


def _kv_rows_kernel(z_ref, *rest):
    o_ref = rest[-1]
    z = z_ref[...]
    tt = z.shape[0]
    for c in range(KV_ROW_COMPS):
        o_ref[pl.ds(c, tt, stride=KV_ROW_COMPS), :] = z[:, c * NSA_HD:(c + 1) * NSA_HD]


def _kv_rows(z, prev, *, layer):
    m = z.shape[0]
    tt = _tile(m, 256)
    width = KV_ROW_COMPS * NSA_HD
    in_specs = [pl.BlockSpec((tt, width), lambda i: (i, A_NKV // width))]
    args = [z]
    alias = {}
    if prev is not None:
        in_specs.append(pl.BlockSpec(memory_space=pl.ANY))
        args.append(prev)
        alias = {1: 0}
    return pl.pallas_call(
        _kv_rows_kernel,
        grid=(m // tt,),
        in_specs=in_specs,
        out_specs=pl.BlockSpec((tt * KV_ROW_COMPS, NSA_HD), lambda i: (layer * (m // tt) + i, 0)),
        out_shape=jax.ShapeDtypeStruct((DEPTH * m * KV_ROW_COMPS, NSA_HD), F32),
        input_output_aliases=alias,
        compiler_params=_params(("parallel",)),
        name="kv_rows",
    )(*args)


def _merge_kernel(a_ref, b_ref, c_ref, d_ref, w_ref, g0_ref, g1_ref, g2_ref, g3_ref, o_ref):
    acc = None
    for n, (x_ref, g_ref) in enumerate(((a_ref, g0_ref), (b_ref, g1_ref), (c_ref, g2_ref), (d_ref, g3_ref))):
        term = _sigmoid(g_ref[...]) * _dot(x_ref[...].astype(BF16), w_ref[n])
        acc = term if acc is None else acc + term
    o_ref[...] = acc.astype(o_ref.dtype)


def _merge(branches, w_branch, zb, *, layer):
    m = zb.shape[0]
    tm = _tile(m, 1024)
    tn = 512
    br = pl.BlockSpec((tm, W_MIX), lambda i, j: (i, 0))
    gz = lambda n: pl.BlockSpec((tm, tn), lambda i, j: (i, (B_GZ + n * D_MODEL) // tn + j))
    return pl.pallas_call(
        _merge_kernel,
        grid=(m // tm, D_MODEL // tn),
        in_specs=[br, br, br, br, pl.BlockSpec((None, N_BRANCH, W_MIX, tn), lambda i, j: (layer, 0, 0, j)),
                  gz(0), gz(1), gz(2), gz(3)],
        out_specs=pl.BlockSpec((tm, tn), lambda i, j: (i, j)),
        out_shape=jax.ShapeDtypeStruct((m, D_MODEL), BF16),
        compiler_params=_params(("parallel", "parallel"), VMEM_BIG),
        name="branch_merge",
    )(*branches, w_branch, zb, zb, zb, zb)


def kernel(x_prompt, x_sample, cache_nsa_kv, state_ret, state_win_kv, state_conv, page_table, norm1, w_in, ret_gn_w,
           ret_gn_b, gm_ln_w, gm_ln_b, gm_ws, gm_bs, nsa_pe, nsa_w1, nsa_w2, conv_w, conv_b, conv_ln_w, conv_ln_b,
           w_branch, w_out, norm2, w_up, w_down, final_norm):
    bp, t, d = x_prompt.shape
    bs, ts, _ = x_sample.shape
    assert d == D_MODEL and t % RET_CHUNK == 0 and ts <= SLAB and norm1.shape[0] == DEPTH
    n_pages = page_table.shape[1]
    past_len = n_pages * PAGE_SIZE
    conv_rows = 128

    assert w_in.shape[2] == _O_END and _O_NG + W_BLK <= _O_END
    w_in_t = jnp.swapaxes(w_in, 1, 2)
    w_branch_b = _cast_bf16(w_branch.reshape(DEPTH, N_BRANCH * W_MIX, d), rows=512).reshape(w_branch.shape)
    nsa_w1_b = nsa_w1.astype(BF16)
    nsa_w2_b = nsa_w2.astype(BF16)

    xp = x_prompt.reshape(bp * t, d)
    xs = jnp.pad(x_sample, ((0, 0), (0, SLAB - ts), (0, 0))).reshape(bs * SLAB, d)
    cache4 = cache_nsa_kv.reshape(DEPTH, cache_nsa_kv.shape[1], PAGE_ROWS, NSA_HD)
    win_t = jnp.transpose(state_win_kv, (0, 1, 3, 4, 2, 5))
    conv_pad = jnp.pad(state_conv, ((0, 0), (0, 0), (HALO - (CONV_W - 1), 0), (0, 0)))
    ret0_p = jnp.zeros((bp, RET_HEADS, RET_DK, RET_DV), F32)
    conv0_p = jnp.zeros((bp, HALO, W_MIX), F32)
    cos_p, sin_p = _rope_tables(np.arange(t), t)
    cos_s, sin_s = _rope_tables(past_len + np.arange(ts), SLAB)
    c_eff_s = math.gcd(ts, RET_CHUNK)
    assert c_eff_s == ts

    keep_p = min(WINDOW, t)
    keep_s = min(WINDOW, past_len + ts)
    ret_p, ret_s, kv_s, win_p, win_s, conv_p, conv_s, gm_s = ([] for _ in range(8))
    kv_p_rows = None
    for l in range(DEPTH):
        hp = _rmsnorm(xp, norm1[l], slab=1, valid=1, out_dtype=BF16)
        hs = _rmsnorm(xs, norm1[l], slab=SLAB, valid=ts, out_dtype=BF16)
        zap, zas = _dense(hp, hs, w_in_t, layer=l, n=N_Z, tn=1024, tn_head=W_BLK, w_rows=_w_in_source_row,
                          name="in_proj")
        zbp, zbs, zcp, zcs = zap, zas, zap, zas

        o_ret_p, r_p = _retention(zap, ret0_p, ret_gn_w[l], ret_gn_b[l], nb=bp, rows=RET_CHUNK,
                                  n_chunks=t // RET_CHUNK, c_eff=RET_CHUNK, cos=cos_p, sin=sin_p)
        (o_gm_p,) = _gmlp(zap, gm_ln_w[l], gm_ln_b[l], gm_ws[l], gm_bs[l], rows=GM_CHUNK, keep_v=False)
        cmp_kv = _compress_prompt(zap, nsa_pe[l], nsa_w1_b[l], nsa_w2_b[l], nb=bp, t=t)
        o_nsa_p = _nsa_prompt(zap, zcp, cmp_kv, nb=bp, t=t)
        o_conv_p, ext_p = _conv_module(zbp, conv0_p, conv_w[l], conv_b[l], conv_ln_w[l], conv_ln_b[l], nb=bp,
                                       rows=conv_rows, n_tiles=t // conv_rows)
        merged_p = _merge((o_ret_p, o_gm_p, o_nsa_p, o_conv_p), w_branch_b, zbp, layer=l)

        o_ret_s, r_s = _retention(zas, state_ret[l], ret_gn_w[l], ret_gn_b[l], nb=bs, rows=SLAB, n_chunks=1,
                                  c_eff=c_eff_s, cos=cos_s, sin=sin_s)
        o_gm_s, gv_s = _gmlp(zas, gm_ln_w[l], gm_ln_b[l], gm_ws[l], gm_bs[l], rows=SLAB, keep_v=True)
        o_nsa_s = _nsa_sample(zas, zcs, cache4, page_table, win_t, nsa_pe[l], nsa_w1[l], nsa_w2[l], layer=l, nb=bs,
                              valid_rows=ts, q0=past_len)
        o_conv_s, ext_s = _conv_module(zbs, conv_pad[l], conv_w[l], conv_b[l], conv_ln_w[l], conv_ln_b[l], nb=bs,
                                       rows=SLAB, n_tiles=1)
        merged_s = _merge((o_ret_s, o_gm_s, o_nsa_s, o_conv_s), w_branch_b, zbs, layer=l)

        xp, xs = _dense(merged_p, merged_s, w_out, layer=l, n=D_MODEL, tn=512, tn_head=512, resp=xp, ress=xs,
                        name="out_proj")
        h2p = _rmsnorm(xp, norm2[l], slab=1, valid=1, out_dtype=BF16)
        h2s = _rmsnorm(xs, norm2[l], slab=1, valid=1, out_dtype=BF16)
        up, us = _dense(h2p, h2s, w_up, layer=l, n=D_FF, tn=1024, tn_head=512, act="relu2", out_dtype=BF16,
                        name="mlp_up")
        xp, xs = _dense(up, us, w_down, layer=l, n=D_MODEL, tn=1024, tn_head=512, tk=2048, resp=xp, ress=xs,
                        name="mlp_down")

        ret_p.append(r_p)
        ret_s.append(r_s)
        z3p = zap.reshape(bp, t, N_Z)
        kv_p_rows = _kv_rows(zap, kv_p_rows, layer=l)
        win_p.append(z3p[:, t - keep_p:, A_NKV + 8 * NSA_HD:A_NKV + 12 * NSA_HD].reshape(bp, keep_p, 2, NSA_KV, NSA_HD))
        z3s = zas.reshape(bs, SLAB, N_Z)[:, :ts]
        kv_s.append(z3s[:, :, A_NKV:A_NKV + 8 * NSA_HD].reshape(bs, ts, 4, NSA_KV, NSA_HD))
        new_win = z3s[:, :, A_NKV + 8 * NSA_HD:A_NKV + 12 * NSA_HD].reshape(bs, ts, 2, NSA_KV, NSA_HD)
        win_s.append(jnp.concatenate([state_win_kv[l], new_win], axis=1)[:, -keep_s:])
        conv_p.append(ext_p[:, HALO + conv_rows - (CONV_W - 1):HALO + conv_rows])
        conv_s.append(ext_s[:, HALO + ts - (CONV_W - 1):HALO + ts])
        gm_s.append(gv_s.reshape(bs, SLAB, W_MIX)[:, :ts])

    y_p = _rmsnorm(xp, final_norm, slab=1, valid=1, out_dtype=F32)
    y_s = _rmsnorm(xs, final_norm, slab=1, valid=1, out_dtype=F32)
    return (y_p.reshape(bp, t, d), y_s.reshape(bs, SLAB, d)[:, :ts],
            jnp.stack(ret_p), jnp.stack(ret_s), kv_p_rows.reshape(DEPTH, bp, t, 4, NSA_KV, NSA_HD), jnp.stack(kv_s),
            jnp.stack(win_p), jnp.stack(win_s),
            jnp.stack(conv_p), jnp.stack(conv_s), jnp.stack(gm_s))
```

```python
import functools
import math

import numpy as np
import jax
import jax.numpy as jnp
from jax import lax
from jax.experimental import pallas as pl
from jax.experimental.pallas import tpu as pltpu

F32 = jnp.float32
BF16 = jnp.bfloat16

D_MODEL = 4096
DEPTH = 2
PAGE_SIZE = 128
W_MIX = D_MODEL // 4
N_BRANCH = 4
RET_HEADS = 4
RET_DK = W_MIX // RET_HEADS
RET_DV = W_MIX // RET_HEADS
RET_CHUNK = 128
ROPE_BASE = 10000.0
GM_CHUNK = 128
GM_GROUPS = 4
NSA_HEADS = 8
NSA_KV = 2
NSA_HD = W_MIX // NSA_HEADS
NSA_GROUP = NSA_HEADS // NSA_KV
CMP_LEN = 32
CMP_STRIDE = 16
SEL_LEN = 64
SEL_SHIFT = 6
SEL_TOPK = 16
WINDOW = 512
CONV_W = 31
D_FF = 4 * D_MODEL
EPS = 1e-6
LN_EPS = 1e-5

LANE = 128
SUBLANE = 8
SLAB = 8
VMEM_BIG = 56 * 1024 * 1024

A_RQ, A_RK, A_RV, A_RG = 0, 1024, 2048, 3072
A_GU, A_GV = 4096, 5120
A_NQ = 6144
B_CA, B_CB, B_GZ = 7168, 8192, 9216
A_NKV = 25600
C_NG = 27136
N_Z = 27648
W_BLK = 512
_O_NKV, _O_NG, _O_CA, _O_END = 7168, 8704, 8728, 27160

NEG = -1e30


def _round_up(a, b):
    return -(-a // b) * b


def _tile(m, pref):
    best = None
    for t in range(SUBLANE, min(m, pref) + 1, SUBLANE):
        if m % t == 0:
            best = t
    assert best is not None, (m, pref)
    return best


def _act_dtype(rows):
    return BF16 if rows % (2 * SUBLANE) == 0 else F32


def _params(sem, vmem=None):
    return pltpu.CompilerParams(dimension_semantics=sem, vmem_limit_bytes=vmem)


def _gelu(x):
    return 0.5 * x * (1.0 + jnp.tanh(0.7978845608028654 * (x + 0.044715 * (x * x * x))))


def _sigmoid(x):
    return 1.0 / (1.0 + jnp.exp(-x))


def _layernorm(x, w, b):
    mu = jnp.mean(x, axis=-1, keepdims=True)
    xc = x - mu
    var = jnp.mean(xc * xc, axis=-1, keepdims=True)
    return xc * lax.rsqrt(var + LN_EPS) * w + b


def _softmax_lanes(s, mask):
    sm = jnp.where(mask, s, NEG)
    m = jnp.max(sm, axis=-1, keepdims=True)
    m = jnp.where(m > 0.5 * NEG, m, 0.0)
    e = jnp.where(mask, jnp.exp(sm - m), 0.0)
    den = jnp.sum(e, axis=-1, keepdims=True)
    return e / jnp.where(den > 0.0, den, 1.0)


def _softmax_parts(s, mask):
    sm = jnp.where(mask, s, NEG)
    m = jnp.max(sm, axis=-1, keepdims=True)
    m = jnp.where(m > 0.5 * NEG, m, 0.0)
    e = jnp.exp(sm - m)
    den = jnp.sum(e, axis=-1, keepdims=True)
    return e, 1.0 / jnp.where(den > 0.0, den, 1.0)


def _dot(a, b):
    return jnp.dot(a, b, preferred_element_type=F32)


def _dot_nt(a, b):
    return lax.dot_general(a, b, (((1,), (1,)), ((), ())), preferred_element_type=F32)


def _dot_tn(a, b):
    return lax.dot_general(a, b, (((0,), (0,)), ((), ())), preferred_element_type=F32)


def _dot_split3(a, b_bf16):
    a1 = a.astype(BF16)
    r1 = a - a1.astype(F32)
    a2 = r1.astype(BF16)
    a3 = (r1 - a2.astype(F32)).astype(BF16)
    return _dot(a1, b_bf16) + _dot(a2, b_bf16) + _dot(a3, b_bf16)


def _rmsnorm_kernel(x_ref, w_ref, o_ref, *, slab, valid):
    x = x_ref[...]
    y = x * lax.rsqrt(jnp.mean(x * x, axis=-1, keepdims=True) + EPS) * w_ref[...]
    if valid < slab:
        row = lax.broadcasted_iota(jnp.int32, y.shape, 0)
        y = jnp.where((row & (slab - 1)) < valid, y, 0.0)
    o_ref[...] = y.astype(o_ref.dtype)


def _rmsnorm(x, w, *, slab, valid, out_dtype):
    m, d = x.shape
    tr = _tile(m, 256)
    assert tr % slab == 0 or valid == slab
    return pl.pallas_call(
        functools.partial(_rmsnorm_kernel, slab=slab, valid=valid),
        grid=(m // tr,),
        in_specs=[pl.BlockSpec((tr, d), lambda i: (i, 0)), pl.BlockSpec((1, d), lambda i: (0, 0))],
        out_specs=pl.BlockSpec((tr, d), lambda i: (i, 0)),
        out_shape=jax.ShapeDtypeStruct((m, d), out_dtype),
        compiler_params=_params(("parallel",)),
        name="rmsnorm",
    )(x, w.reshape(1, d))


def _cast_kernel(x_ref, o_ref):
    o_ref[...] = x_ref[...].astype(o_ref.dtype)


def _cast_bf16(w, *, rows):
    depth, r, c = w.shape
    return pl.pallas_call(
        _cast_kernel,
        grid=(depth, r // rows),
        in_specs=[pl.BlockSpec((None, rows, c), lambda l, i: (l, i, 0))],
        out_specs=pl.BlockSpec((None, rows, c), lambda l, i: (l, i, 0)),
        out_shape=jax.ShapeDtypeStruct(w.shape, BF16),
        compiler_params=_params(("parallel", "parallel"), VMEM_BIG),
        name="cast_weight",
    )(w)


def _w_in_source_row(j):
    nq_blocks = _O_NKV // W_BLK
    gz_end = nq_blocks + (_O_END - _O_CA) // W_BLK
    nkv_end = gz_end + (_O_NG - _O_NKV) // W_BLK
    src = jnp.where(j < nq_blocks, j * W_BLK,
                    jnp.where(j < gz_end, _O_CA + (j - nq_blocks) * W_BLK,
                              jnp.where(j < nkv_end, _O_NKV + (j - gz_end) * W_BLK, _O_NG)))
    return pl.multiple_of(src, SUBLANE)


def _dense_kernel(*refs, act, has_res, kgrid, head, w_transposed):
    refs = list(refs)
    xp_ref = refs.pop(0)
    xs_ref = refs.pop(0) if head else None
    w_ref = refs.pop(0)
    rp_ref = refs.pop(0) if has_res else None
    rs_ref = refs.pop(0) if has_res and head else None
    if not head:
        refs.pop(0)
    op_ref = refs.pop(0)
    if head:
        os_ref, wo_ref = refs
        w = jnp.transpose(w_ref[0]) if w_transposed else w_ref[...]
        wb = w.astype(BF16)
        wo_ref[...] = wb
    else:
        wb = w_ref[...]

    def finish(acc, r_ref):
        if act == "relu2":
            acc = jnp.square(jnp.maximum(acc, 0.0))
        if r_ref is not None:
            acc = r_ref[...] + acc
        return acc

    if kgrid:
        @pl.when(pl.program_id(2) == 0)
        def _():
            op_ref[...] = rp_ref[...]
            if head:
                os_ref[...] = rs_ref[...]

        op_ref[...] += _dot(xp_ref[...], wb)
        if head:
            os_ref[...] += _dot(xs_ref[...], wb)
    else:
        op_ref[...] = finish(_dot(xp_ref[...], wb), rp_ref).astype(op_ref.dtype)
        if head:
            os_ref[...] = finish(_dot(xs_ref[...], wb), rs_ref).astype(os_ref.dtype)


def _dense(xp, xs, w, *, layer, n, tn, tn_head, tm=1024, tk=None, tk_head=None, act=None, out_dtype=F32, resp=None,
           ress=None, w_rows=None, name="dense"):
    mp, k = xp.shape
    ms = xs.shape[0]
    tm = _tile(mp, tm)
    kgrid = tk is not None
    has_res = resp is not None
    assert n % tn == 0 and n % tn_head == 0 and (not kgrid or (has_res and act is None and k % tk == 0 and not w_rows))
    kb = tk if kgrid else k
    nk = k // kb
    kh = tk_head if kgrid and tk_head else kb
    assert k % kh == 0
    wrap = (lambda f: f) if kgrid else (lambda f: (lambda i, j: f(i, j, 0)))
    sem = ("arbitrary",) * (3 if kgrid else 2)
    body = functools.partial(_dense_kernel, act=act, has_res=has_res, kgrid=kgrid, w_transposed=bool(w_rows))

    if w_rows:
        w_spec = pl.BlockSpec((pl.Element(1), pl.Element(tn_head), pl.Element(k)),
                              wrap(lambda i, j, kk: (layer, w_rows(j), 0)))
    else:
        w_spec = pl.BlockSpec((None, kh, tn_head), wrap(lambda i, j, kk: (layer, kk, j)))
    once = dict(pipeline_mode=pl.Buffered(1)) if not kgrid else {}
    in_specs = [pl.BlockSpec((tm, kh), wrap(lambda i, j, kk: (0, kk)), **once),
                pl.BlockSpec((ms, kh), wrap(lambda i, j, kk: (0, kk)), **once), w_spec]
    args = [xp, xs, w]
    if has_res:
        in_specs += [pl.BlockSpec((tm, tn_head), wrap(lambda i, j, kk: (0, j))),
                     pl.BlockSpec((ms, tn_head), wrap(lambda i, j, kk: (0, j)))]
        args += [resp, ress]
    op, os_, wb = pl.pallas_call(
        functools.partial(body, head=True),
        grid=(1, n // tn_head) + ((k // kh,) if kgrid else ()),
        in_specs=in_specs,
        out_specs=[pl.BlockSpec((tm, tn_head), wrap(lambda i, j, kk: (0, j))),
                   pl.BlockSpec((ms, tn_head), wrap(lambda i, j, kk: (0, j))),
                   pl.BlockSpec((kh, tn_head), wrap(lambda i, j, kk: (kk, j)))],
        out_shape=[jax.ShapeDtypeStruct((mp, n), out_dtype), jax.ShapeDtypeStruct((ms, n), out_dtype),
                   jax.ShapeDtypeStruct((k, n), BF16)],
        compiler_params=_params(sem, VMEM_BIG),
        name=name + "_head",
    )(*args)
    if mp == tm:
        return op, os_

    in_specs = [pl.BlockSpec((tm, kb), wrap(lambda i, j, kk: (i + 1, kk))),
                pl.BlockSpec((kb, tn), wrap(lambda i, j, kk: (kk, j)))]
    args = [xp, wb]
    if has_res:
        in_specs.append(pl.BlockSpec((tm, tn), wrap(lambda i, j, kk: (i + 1, j))))
        args.append(resp)
    in_specs.append(pl.BlockSpec(memory_space=pl.ANY))
    args.append(op)
    op = pl.pallas_call(
        functools.partial(body, head=False),
        grid=(mp // tm - 1, n // tn) + ((nk,) if kgrid else ()),
        in_specs=in_specs,
        out_specs=pl.BlockSpec((tm, tn), wrap(lambda i, j, kk: (i + 1, j))),
        out_shape=jax.ShapeDtypeStruct((mp, n), out_dtype),
        input_output_aliases={len(args) - 1: 0},
        compiler_params=_params(sem, VMEM_BIG),
        name=name + "_tail",
    )(*args)
    return op, os_


def _ret_kernel(q_ref, k_ref, v_ref, g_ref, cos_ref, sin_ref, dm_ref, qd_ref, kd_ref, cd_ref, gw_ref, gb_ref,
                r0_ref, o_ref, ro_ref, r_sc, *, rows):
    ci = pl.program_id(1)

    @pl.when(ci == 0)
    def _():
        r_sc[...] = r0_ref[0]

    cos = cos_ref[...]
    sin = sin_ref[...]
    half = RET_DK // 2

    def rot(x):
        x1, x2 = x[:, :half], x[:, half:]
        return jnp.concatenate([x1 * cos - x2 * sin, x2 * cos + x1 * sin], axis=-1)

    def pad(x):
        if rows == RET_CHUNK:
            return x
        return jnp.concatenate([x, jnp.zeros((RET_CHUNK - rows, x.shape[1]), x.dtype)], axis=0)

    for h in range(RET_HEADS):
        cols = slice(h * RET_DK, (h + 1) * RET_DK)
        q = pad(rot(q_ref[:, cols]))
        k = pad(rot(k_ref[:, cols]) * (RET_DK ** -0.5))
        vb = pad(v_ref[:, cols]).astype(BF16)
        r = r_sc[h]
        s = _dot_nt(q.astype(BF16), k.astype(BF16)) * dm_ref[h]
        o = _dot(s.astype(BF16), vb) + _dot((q * qd_ref[h]).astype(BF16), r.astype(BF16))
        r_new = cd_ref[h] * r + _dot_tn((k * kd_ref[h]).astype(BF16), vb)
        r_sc[h] = r_new
        ro_ref[0, h] = r_new
        o = o[:rows]
        mu = jnp.mean(o, axis=-1, keepdims=True)
        oc = o - mu
        var = jnp.mean(oc * oc, axis=-1, keepdims=True)
        y = oc * lax.rsqrt(var + LN_EPS) * gw_ref[:, cols] + gb_ref[:, cols]
        g = g_ref[:, cols]
        o_ref[:, cols] = (g * _sigmoid(g) * y).astype(o_ref.dtype)


def _ret_tables(c_eff):
    log_g = np.log1p(-np.exp2(-5.0 - np.arange(RET_HEADS, dtype=np.float64)))
    i = np.arange(RET_CHUNK, dtype=np.float64)
    live = i < c_eff
    diff = i[:, None] - i[None, :]
    dmask = np.where(diff >= 0, np.exp(np.maximum(diff, 0.0)[None] * log_g[:, None, None]), 0.0)
    dmask = dmask * (live[:, None] & live[None, :])[None]
    q_dec = np.exp((i[None, :] + 1.0) * log_g[:, None]) * live[None]
    k_dec = np.exp((c_eff - 1.0 - i)[None, :] * log_g[:, None]) * live[None]
    c_dec = np.exp(c_eff * log_g)
    bc = lambda a: np.broadcast_to(a[:, :, None], (RET_HEADS, RET_CHUNK, RET_DK))
    return (jnp.asarray(dmask, F32), jnp.asarray(bc(q_dec), F32), jnp.asarray(bc(k_dec), F32),
            jnp.asarray(np.broadcast_to(c_dec[:, None, None], (RET_HEADS, 1, RET_DK)), F32))


def _rope_tables(positions, rows):
    half = RET_DK // 2
    inv = ROPE_BASE ** (-np.arange(half, dtype=np.float64) / half)
    ang = np.asarray(positions, np.float64)[:, None] * inv[None, :]
    cos = np.zeros((rows, half)); sin = np.zeros((rows, half))
    cos[:len(positions)] = np.cos(ang); sin[:len(positions)] = np.sin(ang)
    return jnp.asarray(cos, F32), jnp.asarray(sin, F32)


def _retention(z, r0, gn_w, gn_b, *, nb, rows, n_chunks, c_eff, cos, sin):
    m = z.shape[0]
    dmask, q_dec, k_dec, c_dec = _ret_tables(c_eff)
    zspec = lambda off: pl.BlockSpec((rows, W_MIX), lambda b, c: (b * n_chunks + c, off // W_MIX))
    tab = pl.BlockSpec((rows, RET_DK // 2), lambda b, c: (c, 0))
    whole = lambda a: pl.BlockSpec(a.shape, lambda b, c: (0,) * a.ndim)
    state = pl.BlockSpec((1, RET_HEADS, RET_DK, RET_DV), lambda b, c: (b, 0, 0, 0))
    gw, gb = gn_w.reshape(1, W_MIX), gn_b.reshape(1, W_MIX)
    return pl.pallas_call(
        functools.partial(_ret_kernel, rows=rows),
        grid=(nb, n_chunks),
        in_specs=[zspec(A_RQ), zspec(A_RK), zspec(A_RV), zspec(A_RG), tab, tab,
                  whole(dmask), whole(q_dec), whole(k_dec), whole(c_dec), whole(gw), whole(gb), state],
        out_specs=[pl.BlockSpec((rows, W_MIX), lambda b, c: (b * n_chunks + c, 0)), state],
        out_shape=[jax.ShapeDtypeStruct((m, W_MIX), _act_dtype(rows)),
                   jax.ShapeDtypeStruct((nb, RET_HEADS, RET_DK, RET_DV), F32)],
        scratch_shapes=[pltpu.VMEM((RET_HEADS, RET_DK, RET_DV), F32)],
        compiler_params=_params(("parallel", "arbitrary")),
        name="retention",
    )(z, z, z, z, cos, sin, dmask, q_dec, k_dec, c_dec, gw, gb, r0)


def _gm_kernel(u_ref, v_ref, lw_ref, lb_ref, ws_ref, bst_ref, o_ref, *maybe_gv_ref, rows):
    u = _gelu(u_ref[...])
    v = _layernorm(_gelu(v_ref[...]), lw_ref[...], lb_ref[...])
    for gv_ref in maybe_gv_ref:
        gv_ref[...] = v
    if rows < GM_CHUNK:
        v = jnp.concatenate([v, jnp.zeros((GM_CHUNK - rows, v.shape[1]), F32)], axis=0)
    ri = lax.broadcasted_iota(jnp.int32, (GM_CHUNK, GM_CHUNK), 0)
    cj = lax.broadcasted_iota(jnp.int32, (GM_CHUNK, GM_CHUNK), 1)
    gw = W_MIX // GM_GROUPS
    for g in range(GM_GROUPS):
        wm = jnp.where(cj <= ri, ws_ref[g], 0.0).astype(BF16)
        s = _dot(wm, v[:, g * gw:(g + 1) * gw].astype(BF16)) + bst_ref[:, g:g + 1]
        o_ref[:, g * gw:(g + 1) * gw] = (u[:, g * gw:(g + 1) * gw] * s[:rows]).astype(o_ref.dtype)


def _gmlp(z, ln_w, ln_b, ws, bs, *, rows, keep_v):
    m = z.shape[0]
    row = lambda: pl.BlockSpec((1, W_MIX), lambda i: (0, 0))
    n_out = 2 if keep_v else 1
    return pl.pallas_call(
        functools.partial(_gm_kernel, rows=rows),
        grid=(m // rows,),
        in_specs=[pl.BlockSpec((rows, W_MIX), lambda i: (i, A_GU // W_MIX)),
                  pl.BlockSpec((rows, W_MIX), lambda i: (i, A_GV // W_MIX)),
                  row(), row(),
                  pl.BlockSpec((GM_GROUPS, GM_CHUNK, GM_CHUNK), lambda i: (0, 0, 0)),
                  pl.BlockSpec((GM_CHUNK, GM_GROUPS), lambda i: (0, 0))],
        out_specs=[pl.BlockSpec((rows, W_MIX), lambda i: (i, 0)), pl.BlockSpec((rows, W_MIX), lambda i: (i, 0))][:n_out],
        out_shape=[jax.ShapeDtypeStruct((m, W_MIX), _act_dtype(rows)), jax.ShapeDtypeStruct((m, W_MIX), F32)][:n_out],
        compiler_params=_params(("parallel",)),
        name="gmlp",
    )(z, z, ln_w.reshape(1, W_MIX), ln_b.reshape(1, W_MIX), ws, bs.T)


HALO = 32
CONV_RB = 64


def _conv_kernel(a_ref, b_ref, buf_ref, cw_ref, cb_ref, lw_ref, lb_ref, o_ref, ext_ref, ext_sc, y_sc, win_sc, *,
                 rows):
    ti = pl.program_id(1)

    @pl.when(ti == 0)
    def _():
        ext_sc[0:HALO, :] = buf_ref[0]

    @pl.when(ti > 0)
    def _():
        ext_sc[0:HALO, :] = ext_sc[rows:rows + HALO, :]

    ext_sc[HALO:HALO + rows, :] = a_ref[...] * _sigmoid(b_ref[...])
    rb = min(CONV_RB, rows)
    first = HALO - (CONV_W - 1)
    for cc in range(W_MIX // LANE):
        lanes = slice(cc * LANE, (cc + 1) * LANE)
        for r0 in range(0, rows, rb):
            acc = jnp.broadcast_to(cb_ref[:, lanes], (rb, LANE))
            for s in range(SUBLANE):
                taps = [w for w in range(CONV_W) if (first + w) % SUBLANE == s]
                if taps:
                    start = first + taps[0] + r0
                    span = rb + SUBLANE * (len(taps) - 1)
                    win_sc[0:span, :] = ext_sc[start:start + span, lanes]
                    for k, w in enumerate(taps):
                        acc = acc + win_sc[SUBLANE * k:SUBLANE * k + rb, :] * cw_ref[w:w + 1, lanes]
            y_sc[r0:r0 + rb, lanes] = acc
    y = _layernorm(y_sc[...], lw_ref[...], lb_ref[...])
    o_ref[...] = (y * _sigmoid(y)).astype(o_ref.dtype)
    ext_ref[0] = ext_sc[...]


def _conv_module(z, buf, cw, cb, ln_w, ln_b, *, nb, rows, n_tiles):
    m = z.shape[0]
    row = lambda: pl.BlockSpec((1, W_MIX), lambda b, t: (0, 0))
    cwp = jnp.concatenate([cw, jnp.zeros((HALO - CONV_W, W_MIX), F32)], axis=0)
    return pl.pallas_call(
        functools.partial(_conv_kernel, rows=rows),
        grid=(nb, n_tiles),
        in_specs=[pl.BlockSpec((rows, W_MIX), lambda b, t: (b * n_tiles + t, B_CA // W_MIX)),
                  pl.BlockSpec((rows, W_MIX), lambda b, t: (b * n_tiles + t, B_CB // W_MIX)),
                  pl.BlockSpec((1, HALO, W_MIX), lambda b, t: (b, 0, 0)),
                  pl.BlockSpec((HALO, W_MIX), lambda b, t: (0, 0)),
                  row(), row(), row()],
        out_specs=[pl.BlockSpec((rows, W_MIX), lambda b, t: (b * n_tiles + t, 0)),
                   pl.BlockSpec((1, HALO + rows, W_MIX), lambda b, t: (b, 0, 0))],
        out_shape=[jax.ShapeDtypeStruct((m, W_MIX), _act_dtype(rows)),
                   jax.ShapeDtypeStruct((nb, HALO + rows, W_MIX), F32)],
        scratch_shapes=[pltpu.VMEM((HALO + rows, W_MIX), F32), pltpu.VMEM((rows, W_MIX), F32),
                        pltpu.VMEM((min(CONV_RB, rows) + HALO - SUBLANE, LANE), F32)],
        compiler_params=_params(("parallel", "arbitrary")),
        name="conv_module",
    )(z, z, buf, cwp, cb.reshape(1, W_MIX), ln_w.reshape(1, W_MIX), ln_b.reshape(1, W_MIX))


def _overlap_t(n_c_pad, n_s_pad, n_c, n_s):
    cs = np.arange(n_c_pad)[:, None] * CMP_STRIDE
    ss = np.arange(n_s_pad)[None, :] * SEL_LEN
    ov = np.clip(np.minimum(ss + SEL_LEN, cs + CMP_LEN) - np.maximum(ss, cs), 0, None).astype(np.float64)
    ov = ov * (np.arange(n_c_pad)[:, None] < n_c) * (np.arange(n_s_pad)[None, :] < n_s)
    return jnp.asarray(ov, BF16)


def _cmp_p_kernel(x_ref, pe_ref, w1_ref, w2_ref, o_ref, xs_sc, *, t, ncp):
    xs_sc[0:t, :] = x_ref[...]
    xs_sc[t:, :] = jnp.zeros((xs_sc.shape[0] - t, NSA_HD), F32)
    acc = jnp.zeros((ncp, NSA_HD), F32)
    for l in range(CMP_LEN):
        rows = xs_sc[pl.ds(l, ncp, stride=CMP_STRIDE), :] + pe_ref[0, l:l + 1, :]
        acc = acc + _dot(rows.astype(BF16), w1_ref[0, l * NSA_HD:(l + 1) * NSA_HD, :])
    o_ref[0, 0] = _dot(_gelu(acc).astype(BF16), w2_ref[0])


def _compress_prompt(z, pe, w1, w2, *, nb, t):
    ncp = _round_up(t // CMP_STRIDE, LANE)
    pad_rows = _round_up(CMP_STRIDE * (ncp - 1) + CMP_LEN, SUBLANE)
    return pl.pallas_call(
        functools.partial(_cmp_p_kernel, t=t, ncp=ncp),
        grid=(nb, 4),
        in_specs=[pl.BlockSpec((t, NSA_HD), lambda b, j: (b, A_NKV // NSA_HD + j)),
                  pl.BlockSpec((1, CMP_LEN, NSA_HD), lambda b, j: (j // 2, 0, 0)),
                  pl.BlockSpec((1, CMP_LEN * NSA_HD, NSA_HD), lambda b, j: (j // 2, 0, 0)),
                  pl.BlockSpec((1, NSA_HD, NSA_HD), lambda b, j: (j // 2, 0, 0))],
        out_specs=pl.BlockSpec((1, 1, ncp, NSA_HD), lambda b, j: (b, j, 0, 0)),
        out_shape=jax.ShapeDtypeStruct((nb, 4, ncp, NSA_HD), F32),
        scratch_shapes=[pltpu.VMEM((max(pad_rows, t + SUBLANE), NSA_HD), F32)],
        compiler_params=_params(("parallel", "parallel")),
        name="nsa_compress_prompt",
    )(z, pe, w1, w2)


def _nsa_p_kernel(q_ref, kc_ref, vc_ref, ks_ref, vs_ref, kw_ref, vw_ref, ng_ref, ov_ref, ex_ref, o_ref, os_sc,
                  *, t, tq, n_c, n_s, span, key_limits):
    qi = pl.program_id(2)
    kv = pl.program_id(1)
    scale = NSA_HD ** -0.5
    g_n = NSA_GROUP
    q = q_ref[...]
    qh = [q[:, g * NSA_HD:(g + 1) * NSA_HD].astype(BF16) for g in range(g_n)]
    q4 = jnp.concatenate(qh, axis=0)
    qpos = qi * tq + lax.broadcasted_iota(jnp.int32, (tq, 1), 0)
    qpos4 = jnp.concatenate([qpos] * g_n, axis=0)

    kc = kc_ref[0, 0].astype(BF16)
    ncp = kc.shape[0]
    s_c = _dot_nt(q4, kc) * scale
    cidx = lax.broadcasted_iota(jnp.int32, (1, ncp), 1)
    mask_c = (cidx * CMP_STRIDE + CMP_LEN - 1 <= qpos4) & (cidx < n_c)
    p_c = _softmax_lanes(s_c, mask_c)
    o_c = _dot(p_c.astype(BF16), vc_ref[0, 0].astype(BF16))

    p_sum = p_c[0:tq]
    for g in range(1, g_n):
        p_sum = p_sum + p_c[g * tq:(g + 1) * tq]
    imp = _dot_split3(p_sum, ov_ref[...])
    nsp = imp.shape[1]
    blk = lax.broadcasted_iota(jnp.int32, (tq, nsp), 1)
    cur = jnp.right_shift(qpos, SEL_SHIFT)
    forced = (blk == 0) | (blk == cur) | (blk == cur - 1)
    valid = (blk * SEL_LEN <= qpos) & (blk < n_s)
    score = jnp.where(valid, jnp.where(forced, jnp.inf, imp), -jnp.inf)
    live = _round_up(n_s, SUBLANE)
    score_t = jnp.transpose(score)[0:live]
    blk_t = lax.broadcasted_iota(jnp.int32, (live, tq), 0)
    rank_t = jnp.zeros((live, tq), F32)
    for j in range(n_s):
        sj = score_t[j:j + 1, :]
        ahead = (sj > score_t) | ((sj == score_t) & (blk_t > j))
        rank_t = rank_t + jnp.where(ahead, 1.0, 0.0)
    sel_t = jnp.where((score_t > -jnp.inf) & (rank_t < float(min(SEL_TOPK, n_s))), 1.0, 0.0)
    sel_tb = jnp.concatenate([sel_t, jnp.zeros((nsp - live, tq), F32)], axis=0).astype(BF16)

    for lo, hi, n_keys in key_limits:
        @pl.when((qi >= lo) & (qi < hi))
        def _(n_keys=n_keys):
            kidx = lax.broadcasted_iota(jnp.int32, (1, n_keys), 1)
            mask_s = (_dot_tn(sel_tb, ex_ref[:, 0:n_keys]) > 0.5) & (kidx <= qpos)
            kb = ks_ref[0:n_keys, :].astype(BF16)
            vb = vs_ref[0:n_keys, :].astype(BF16)
            for g in range(g_n):
                e, inv = _softmax_parts(_dot_nt(qh[g], kb) * scale, mask_s)
                os_sc[g * tq:(g + 1) * tq, :] = _dot(e.astype(BF16), vb) * inv

    o_s = os_sc[...]

    w0 = pl.multiple_of(jnp.clip(qi * tq - WINDOW, 0, t - span), LANE)
    kw = kw_ref[pl.ds(w0, span), :].astype(BF16)
    vw = vw_ref[pl.ds(w0, span), :].astype(BF16)
    kpos = w0 + lax.broadcasted_iota(jnp.int32, (1, span), 1)
    dist = qpos4 - kpos
    mask_w = (dist >= 0) & (dist <= WINDOW)
    e_w, inv_w = _softmax_parts(_dot_nt(q4, kw) * scale, mask_w)
    o_w = _dot(e_w.astype(BF16), vw) * inv_w

    gates = _sigmoid(ng_ref[...])
    for g in range(g_n):
        rows = slice(g * tq, (g + 1) * tq)
        col = (kv * g_n + g) * 3
        lane = lax.broadcasted_iota(jnp.int32, gates.shape, 1)
        pick = lambda j: jnp.sum(jnp.where(lane == col + j, gates, 0.0), axis=-1, keepdims=True)
        o = pick(0) * o_c[rows] + pick(1) * o_s[rows] + pick(2) * o_w[rows]
        o_ref[:, g * NSA_HD:(g + 1) * NSA_HD] = o.astype(o_ref.dtype)


KEY_PREFIX_VARIANTS = 8


def _nsa_prompt(z, zc, cmp_kv, *, nb, t):
    m = z.shape[0]
    tq = 128
    nq = t // tq
    ends = sorted({-(-v * nq // KEY_PREFIX_VARIANTS) for v in range(1, KEY_PREFIX_VARIANTS + 1)})
    key_limits = tuple((lo, hi, hi * tq) for lo, hi in zip([0] + ends[:-1], ends))
    n_c = (t - CMP_LEN) // CMP_STRIDE + 1
    n_s = -(-t // SEL_LEN)
    ncp = cmp_kv.shape[2]
    nsp = LANE
    assert n_s <= nsp
    span = min(WINDOW + tq, t)
    ov = _overlap_t(ncp, nsp, n_c, n_s)
    ex = jnp.asarray((np.arange(t)[None, :] // SEL_LEN) == np.arange(nsp)[:, None], BF16)
    kvw = NSA_GROUP * NSA_HD
    nk = A_NKV // NSA_HD
    full = lambda j0: pl.BlockSpec((t, NSA_HD), lambda b, kv, qi: (b, nk + j0 + kv))
    return pl.pallas_call(
        functools.partial(_nsa_p_kernel, t=t, tq=tq, n_c=n_c, n_s=n_s, span=span, key_limits=key_limits),
        grid=(nb, NSA_KV, nq),
        in_specs=[pl.BlockSpec((tq, kvw), lambda b, kv, qi: (b * nq + qi, A_NQ // kvw + kv)),
                  pl.BlockSpec((1, 1, ncp, NSA_HD), lambda b, kv, qi: (b, kv, 0, 0)),
                  pl.BlockSpec((1, 1, ncp, NSA_HD), lambda b, kv, qi: (b, 2 + kv, 0, 0)),
                  full(4), full(6), full(8), full(10),
                  pl.BlockSpec((tq, LANE), lambda b, kv, qi: (b * nq + qi, C_NG // LANE)),
                  pl.BlockSpec((ncp, nsp), lambda b, kv, qi: (0, 0)),
                  pl.BlockSpec((nsp, t), lambda b, kv, qi: (0, 0))],
        out_specs=pl.BlockSpec((tq, kvw), lambda b, kv, qi: (b * nq + qi, kv)),
        out_shape=jax.ShapeDtypeStruct((m, W_MIX), BF16),
        scratch_shapes=[pltpu.VMEM((NSA_GROUP * tq, NSA_HD), F32)],
        compiler_params=_params(("parallel", "parallel", "arbitrary"), VMEM_BIG),
        name="nsa_prompt",
    )(z, cmp_kv, cmp_kv, z, z, z, z, zc, ov, ex)


PAGES_PER_STEP = 16
PAGE_ROWS = PAGE_SIZE * 4 * NSA_KV
GROUPS_PER_PAGE = PAGE_SIZE // CMP_STRIDE


def _s_cmp_kernel(pt_ref, *refs):
    del pt_ref
    pages = refs[:PAGES_PER_STEP]
    w_ref, o_ref = refs[PAGES_PER_STEP], refs[PAGES_PER_STEP + 1]
    gp = PAGES_PER_STEP * GROUPS_PER_PAGE
    for which in range(2):
        acc = jnp.zeros((NSA_KV * gp, 2 * NSA_HD), F32)
        for l in range(CMP_STRIDE):
            pieces = [pages[p][pl.ds(l * 4 * NSA_KV + which * NSA_KV + kv, GROUPS_PER_PAGE, stride=CMP_STRIDE * 4 * NSA_KV), :]
                      for kv in range(NSA_KV) for p in range(PAGES_PER_STEP)]
            acc = acc + _dot(jnp.concatenate(pieces, axis=0).astype(BF16), w_ref[which, l])
        for kv in range(NSA_KV):
            o_ref[0, which, kv] = acc[kv * gp:(kv + 1) * gp]


def _page_specs(layer, n):
    def spec(p):
        return pl.BlockSpec((None, None, PAGE_ROWS, NSA_HD),
                            lambda b, s, pt: (layer, pt[b, s * PAGES_PER_STEP + p], 0, 0))
    return [spec(p) for p in range(n)]


def _compress_sample_partial(cache4, page_table, w1ab, *, layer, nb, n_pages):
    steps = n_pages // PAGES_PER_STEP
    gp = PAGES_PER_STEP * GROUPS_PER_PAGE
    ng = n_pages * GROUPS_PER_PAGE
    return pl.pallas_call(
        _s_cmp_kernel,
        grid_spec=pltpu.PrefetchScalarGridSpec(
            num_scalar_prefetch=1,
            grid=(nb, steps),
            in_specs=_page_specs(layer, PAGES_PER_STEP)
            + [pl.BlockSpec((2, CMP_STRIDE, NSA_HD, 2 * NSA_HD), lambda b, s, pt: (0, 0, 0, 0))],
            out_specs=pl.BlockSpec((1, 2, NSA_KV, gp, 2 * NSA_HD), lambda b, s, pt: (b, 0, 0, s, 0)),
        ),
        out_shape=jax.ShapeDtypeStruct((nb, 2, NSA_KV, ng, 2 * NSA_HD), F32),
        compiler_params=_params(("parallel", "arbitrary"), VMEM_BIG),
        name="nsa_compress_sample",
    )(page_table, *([cache4] * PAGES_PER_STEP), w1ab)


def _s_sel_kernel(uv_ref, pef_ref, w1_ref, w2_ref, q_ref, ng_ref, kwn_ref, vwn_ref, kwp_ref, vwp_ref, ov_ref,
                  selt_ref, op_ref, *, q0, valid_rows, n_c, n_s):
    kv = pl.program_id(1)
    scale = NSA_HD ** -0.5
    g_n = NSA_GROUP
    rows = SLAB
    q = q_ref[...]
    q4 = jnp.concatenate([q[:, g * NSA_HD:(g + 1) * NSA_HD] for g in range(g_n)], axis=0).astype(BF16)
    tpos = lax.broadcasted_iota(jnp.int32, (rows, 1), 0)
    qpos = q0 + tpos
    qpos4 = jnp.concatenate([qpos] * g_n, axis=0)

    def compressed(which):
        uv = uv_ref[0, which, 0]
        ng = uv.shape[0]
        u = uv[:, :NSA_HD]
        v_next = pltpu.roll(uv[:, NSA_HD:], ng - 1, 0)
        const = _dot(pef_ref[which], w1_ref[which])[0:1]
        return _dot(_gelu(u + v_next + const).astype(BF16), w2_ref[which])

    kc = compressed(0)
    vc = compressed(1)
    ncp = kc.shape[0]
    s_c = _dot_nt(q4, kc.astype(BF16)) * scale
    cidx = lax.broadcasted_iota(jnp.int32, (1, ncp), 1)
    mask_c = (cidx * CMP_STRIDE + CMP_LEN - 1 <= qpos4) & (cidx < n_c)
    p_c = _softmax_lanes(s_c, mask_c)
    o_c = _dot(p_c.astype(BF16), vc.astype(BF16))

    p_sum = p_c[0:rows]
    for g in range(1, g_n):
        p_sum = p_sum + p_c[g * rows:(g + 1) * rows]
    imp = _dot_split3(p_sum, ov_ref[...])
    nsp = imp.shape[1]
    blk = lax.broadcasted_iota(jnp.int32, (rows, nsp), 1)
    cur = jnp.right_shift(qpos, SEL_SHIFT)
    forced = (blk == 0) | (blk == cur) | (blk == cur - 1)
    valid = (blk * SEL_LEN <= qpos) & (blk < n_s)
    score = jnp.where(valid, jnp.where(forced, jnp.inf, imp), -jnp.inf)
    score_pad = jnp.concatenate([score, jnp.zeros((LANE - rows, nsp), F32)], axis=0)
    score_t = jnp.transpose(score_pad)
    bi = lax.broadcasted_iota(jnp.int32, (nsp, nsp), 0)
    bj = lax.broadcasted_iota(jnp.int32, (nsp, nsp), 1)
    lane = lax.broadcasted_iota(jnp.int32, (nsp, LANE), 1)
    sel_t = jnp.zeros((nsp, LANE), F32)
    k_take = float(min(SEL_TOPK, n_s))
    for tt in range(valid_rows):
        s_i = score_t[:, tt:tt + 1]
        s_j = score[tt:tt + 1, :]
        ahead = (s_j > s_i) | ((s_j == s_i) & (bj < bi))
        rank_i = jnp.sum(jnp.where(ahead, 1.0, 0.0), axis=-1, keepdims=True)
        ok_i = (rank_i < k_take) & (s_i > -jnp.inf)
        sel_t = jnp.where(((lane & (rows - 1)) == tt) & (lane < g_n * rows) & ok_i, 1.0, sel_t)
    selt_ref[0, 0] = sel_t

    kw = jnp.concatenate([kwp_ref[...], kwn_ref[...]], axis=0).astype(BF16)
    vw = jnp.concatenate([vwp_ref[...], vwn_ref[...]], axis=0).astype(BF16)
    n_before = kwp_ref.shape[0]
    widx = lax.broadcasted_iota(jnp.int32, (1, n_before + rows), 1)
    kpos = jnp.where(widx < n_before, q0 - n_before + widx, q0 + widx - n_before)
    dist = qpos4 - kpos
    mask_w = (dist >= 0) & (dist <= WINDOW) & (kpos >= 0)
    s_w = _dot_nt(q4, kw) * scale
    p_w = _softmax_lanes(s_w, mask_w)
    o_w = _dot(p_w.astype(BF16), vw)

    gates = _sigmoid(ng_ref[...])
    glane = lax.broadcasted_iota(jnp.int32, gates.shape, 1)
    for g in range(g_n):
        r = slice(g * rows, (g + 1) * rows)
        col = (kv * g_n + g) * 3
        pick = lambda j: jnp.sum(jnp.where(glane == col + j, gates, 0.0), axis=-1, keepdims=True)
        op_ref[0, 0, r, :] = pick(0) * o_c[r] + pick(2) * o_w[r]


def _s_attn_kernel(pt_ref, *refs, n_steps, valid_rows):
    del pt_ref
    pages = refs[:PAGES_PER_STEP]
    q_ref, kvn_ref, ng_ref, selt_ref, op_ref, o_ref, m_sc, l_sc, acc_sc = refs[PAGES_PER_STEP:]
    s_id = pl.program_id(1)
    scale = NSA_HD ** -0.5
    g_n = NSA_GROUP
    rows = SLAB
    blocks_per_step = PAGES_PER_STEP * PAGE_SIZE // SEL_LEN

    @pl.when(s_id == 0)
    def _():
        m_sc[...] = jnp.full(m_sc.shape, NEG, F32)
        l_sc[...] = jnp.zeros(l_sc.shape, F32)
        acc_sc[...] = jnp.zeros(acc_sc.shape, F32)

    q = q_ref[...]

    def q_rows(kv):
        q4 = jnp.concatenate([q[:, (kv * g_n + g) * NSA_HD:(kv * g_n + g + 1) * NSA_HD] for g in range(g_n)], axis=0)
        return jnp.concatenate([q4, jnp.zeros((LANE - g_n * rows, NSA_HD), F32)], axis=0).astype(BF16)

    def update(kv, s_t, mask_t, v):
        m_old = m_sc[kv]
        m_new = jnp.maximum(m_old, jnp.max(jnp.where(mask_t, s_t, NEG), axis=0, keepdims=True))
        alpha = jnp.exp(m_old - m_new)
        p_t = jnp.where(mask_t, jnp.exp(jnp.minimum(s_t - m_new, 0.0)), 0.0)
        l_sc[kv] = alpha * l_sc[kv] + jnp.sum(p_t, axis=0, keepdims=True)
        acc_sc[kv] = alpha * acc_sc[kv] + _dot_tn(v, p_t.astype(BF16))
        m_sc[kv] = m_new

    stride = 4 * NSA_KV
    for kv in range(NSA_KV):
        qk = q_rows(kv)
        k = jnp.concatenate([pg[pl.ds(2 * NSA_KV + kv, PAGE_SIZE, stride=stride), :] for pg in pages], axis=0)
        v = jnp.concatenate([pg[pl.ds(3 * NSA_KV + kv, PAGE_SIZE, stride=stride), :] for pg in pages], axis=0)
        s_t = _dot_nt(k.astype(BF16), qk) * scale
        start = pl.multiple_of(s_id * blocks_per_step, blocks_per_step)
        chunk = selt_ref[0, kv, pl.ds(start, blocks_per_step), :]
        mask_t = jnp.concatenate([jnp.broadcast_to(chunk[c:c + 1, :], (SEL_LEN, LANE))
                                  for c in range(blocks_per_step)], axis=0) > 0.5
        update(kv, s_t, mask_t, v.astype(BF16))

    @pl.when(s_id == n_steps - 1)
    def _():
        n_past_blocks = n_steps * blocks_per_step
        kvn = jnp.concatenate([kvn_ref[...], jnp.zeros((LANE - rows, 4 * NSA_HD), F32)], axis=0)
        gates = _sigmoid(ng_ref[...])
        glane = lax.broadcasted_iota(jnp.int32, gates.shape, 1)
        for kv in range(NSA_KV):
            k_new = kvn[:, kv * NSA_HD:(kv + 1) * NSA_HD].astype(BF16)
            v_new = kvn[:, (NSA_KV + kv) * NSA_HD:(NSA_KV + kv + 1) * NSA_HD].astype(BF16)
            s_t = _dot_nt(k_new, q_rows(kv)) * scale
            key = lax.broadcasted_iota(jnp.int32, (LANE, LANE), 0)
            tok = lax.broadcasted_iota(jnp.int32, (LANE, LANE), 1) & (rows - 1)
            blk_ok = selt_ref[0, kv, n_past_blocks:n_past_blocks + 1, :] > 0.5
            mask_t = (key <= tok) & (key < valid_rows) & blk_ok
            update(kv, s_t, mask_t, v_new)
            l = l_sc[kv]
            o_t = acc_sc[kv] / jnp.where(l > 0.0, l, 1.0)
            o_sel = jnp.transpose(o_t)
            for g in range(g_n):
                r = slice(g * rows, (g + 1) * rows)
                col = (kv * g_n + g) * 3 + 1
                gate = jnp.sum(jnp.where(glane == col, gates, 0.0), axis=-1, keepdims=True)
                o = op_ref[0, kv, r, :] + gate * o_sel[r]
                o_ref[:, (kv * g_n + g) * NSA_HD:(kv * g_n + g + 1) * NSA_HD] = o.astype(o_ref.dtype)


def _nsa_sample(zs, zc, cache4, page_table, win_past, pe, w1, w2, *, layer, nb, valid_rows, q0):
    n_pages = page_table.shape[1]
    assert n_pages % PAGES_PER_STEP == 0 and q0 == n_pages * PAGE_SIZE
    s_len = q0 + valid_rows
    n_c = (s_len - CMP_LEN) // CMP_STRIDE + 1
    n_s = -(-s_len // SEL_LEN)
    ng = n_pages * GROUPS_PER_PAGE
    assert n_c <= ng - 1
    nsp = _round_up(n_s, LANE)
    steps = n_pages // PAGES_PER_STEP
    w1b = w1.astype(BF16)
    w1r = w1b.reshape(2, CMP_LEN, NSA_HD, NSA_HD)
    w1ab = jnp.concatenate([w1r[:, :CMP_STRIDE], w1r[:, CMP_STRIDE:]], axis=-1)
    uv = _compress_sample_partial(cache4, page_table, w1ab, layer=layer, nb=nb, n_pages=n_pages)
    pef = jnp.broadcast_to(pe.reshape(2, 1, CMP_LEN * NSA_HD), (2, SUBLANE, CMP_LEN * NSA_HD)).astype(BF16)
    ov = _overlap_t(ng, nsp, n_c, n_s)
    kvw = NSA_GROUP * NSA_HD
    nk = A_NKV // NSA_HD
    n_before = win_past.shape[4]
    selt, o_part = pl.pallas_call(
        functools.partial(_s_sel_kernel, q0=q0, valid_rows=valid_rows, n_c=n_c, n_s=n_s),
        grid=(nb, NSA_KV),
        in_specs=[pl.BlockSpec((1, 2, 1, ng, 2 * NSA_HD), lambda b, kv: (b, 0, kv, 0, 0)),
                  pl.BlockSpec((2, SUBLANE, CMP_LEN * NSA_HD), lambda b, kv: (0, 0, 0)),
                  pl.BlockSpec((2, CMP_LEN * NSA_HD, NSA_HD), lambda b, kv: (0, 0, 0)),
                  pl.BlockSpec((2, NSA_HD, NSA_HD), lambda b, kv: (0, 0, 0)),
                  pl.BlockSpec((SLAB, kvw), lambda b, kv: (b, A_NQ // kvw + kv)),
                  pl.BlockSpec((SLAB, LANE), lambda b, kv: (b, C_NG // LANE)),
                  pl.BlockSpec((SLAB, NSA_HD), lambda b, kv: (b, nk + 8 + kv)),
                  pl.BlockSpec((SLAB, NSA_HD), lambda b, kv: (b, nk + 10 + kv)),
                  pl.BlockSpec((None, None, None, None, n_before, NSA_HD), lambda b, kv: (layer, b, 0, kv, 0, 0)),
                  pl.BlockSpec((None, None, None, None, n_before, NSA_HD), lambda b, kv: (layer, b, 1, kv, 0, 0)),
                  pl.BlockSpec((ng, nsp), lambda b, kv: (0, 0))],
        out_specs=[pl.BlockSpec((1, 1, nsp, LANE), lambda b, kv: (b, kv, 0, 0)),
                   pl.BlockSpec((1, 1, NSA_GROUP * SLAB, NSA_HD), lambda b, kv: (b, kv, 0, 0))],
        out_shape=[jax.ShapeDtypeStruct((nb, NSA_KV, nsp, LANE), F32),
                   jax.ShapeDtypeStruct((nb, NSA_KV, NSA_GROUP * SLAB, NSA_HD), F32)],
        compiler_params=_params(("parallel", "parallel"), VMEM_BIG),
        name="nsa_select_sample",
    )(uv, pef, w1b, w2.astype(BF16), zs, zc, zs, zs, win_past, win_past, ov)
    return pl.pallas_call(
        functools.partial(_s_attn_kernel, n_steps=steps, valid_rows=valid_rows),
        grid_spec=pltpu.PrefetchScalarGridSpec(
            num_scalar_prefetch=1,
            grid=(nb, steps),
            in_specs=_page_specs(layer, PAGES_PER_STEP)
            + [pl.BlockSpec((SLAB, NSA_HEADS * NSA_HD), lambda b, s, pt: (b, A_NQ // (NSA_HEADS * NSA_HD))),
               pl.BlockSpec((SLAB, 4 * NSA_HD), lambda b, s, pt: (b, (A_NKV + 4 * NSA_HD) // (4 * NSA_HD))),
               pl.BlockSpec((SLAB, LANE), lambda b, s, pt: (b, C_NG // LANE)),
               pl.BlockSpec((1, NSA_KV, nsp, LANE), lambda b, s, pt: (b, 0, 0, 0)),
               pl.BlockSpec((1, NSA_KV, NSA_GROUP * SLAB, NSA_HD), lambda b, s, pt: (b, 0, 0, 0))],
            out_specs=pl.BlockSpec((SLAB, W_MIX), lambda b, s, pt: (b, 0)),
            scratch_shapes=[pltpu.VMEM((NSA_KV, 1, LANE), F32), pltpu.VMEM((NSA_KV, 1, LANE), F32),
                            pltpu.VMEM((NSA_KV, NSA_HD, LANE), F32)],
        ),
        out_shape=jax.ShapeDtypeStruct((nb * SLAB, W_MIX), _act_dtype(SLAB)),
        compiler_params=_params(("parallel", "arbitrary"), VMEM_BIG),
        name="nsa_attend_sample",
    )(page_table, *([cache4] * PAGES_PER_STEP), zs, zs, zc, selt, o_part)


KV_ROW_COMPS = 4 * NSA_KV


def _kv_rows_kernel(z_ref, *rest):
    o_ref = rest[-1]
    z = z_ref[...]
    tt = z.shape[0]
    for c in range(KV_ROW_COMPS):
        o_ref[pl.ds(c, tt, stride=KV_ROW_COMPS), :] = z[:, c * NSA_HD:(c + 1) * NSA_HD]


def _kv_rows(z, prev, *, layer):
    m = z.shape[0]
    tt = _tile(m, 256)
    width = KV_ROW_COMPS * NSA_HD
    in_specs = [pl.BlockSpec((tt, width), lambda i: (i, A_NKV // width))]
    args = [z]
    alias = {}
    if prev is not None:
        in_specs.append(pl.BlockSpec(memory_space=pl.ANY))
        args.append(prev)
        alias = {1: 0}
    return pl.pallas_call(
        _kv_rows_kernel,
        grid=(m // tt,),
        in_specs=in_specs,
        out_specs=pl.BlockSpec((tt * KV_ROW_COMPS, NSA_HD), lambda i: (layer * (m // tt) + i, 0)),
        out_shape=jax.ShapeDtypeStruct((DEPTH * m * KV_ROW_COMPS, NSA_HD), F32),
        input_output_aliases=alias,
        compiler_params=_params(("parallel",)),
        name="kv_rows",
    )(*args)


def _merge_kernel(a_ref, b_ref, c_ref, d_ref, w_ref, g0_ref, g1_ref, g2_ref, g3_ref, o_ref):
    acc = None
    for n, (x_ref, g_ref) in enumerate(((a_ref, g0_ref), (b_ref, g1_ref), (c_ref, g2_ref), (d_ref, g3_ref))):
        term = _sigmoid(g_ref[...]) * _dot(x_ref[...].astype(BF16), w_ref[n])
        acc = term if acc is None else acc + term
    o_ref[...] = acc.astype(o_ref.dtype)


def _merge(branches, w_branch, zb, *, layer):
    m = zb.shape[0]
    tm = _tile(m, 1024)
    tn = 512
    br = pl.BlockSpec((tm, W_MIX), lambda i, j: (i, 0))
    gz = lambda n: pl.BlockSpec((tm, tn), lambda i, j: (i, (B_GZ + n * D_MODEL) // tn + j))
    return pl.pallas_call(
        _merge_kernel,
        grid=(m // tm, D_MODEL // tn),
        in_specs=[br, br, br, br, pl.BlockSpec((None, N_BRANCH, W_MIX, tn), lambda i, j: (layer, 0, 0, j)),
                  gz(0), gz(1), gz(2), gz(3)],
        out_specs=pl.BlockSpec((tm, tn), lambda i, j: (i, j)),
        out_shape=jax.ShapeDtypeStruct((m, D_MODEL), BF16),
        compiler_params=_params(("parallel", "parallel"), VMEM_BIG),
        name="branch_merge",
    )(*branches, w_branch, zb, zb, zb, zb)


def kernel(x_prompt, x_sample, cache_nsa_kv, state_ret, state_win_kv, state_conv, page_table, norm1, w_in, ret_gn_w,
           ret_gn_b, gm_ln_w, gm_ln_b, gm_ws, gm_bs, nsa_pe, nsa_w1, nsa_w2, conv_w, conv_b, conv_ln_w, conv_ln_b,
           w_branch, w_out, norm2, w_up, w_down, final_norm):
    bp, t, d = x_prompt.shape
    bs, ts, _ = x_sample.shape
    assert d == D_MODEL and t % RET_CHUNK == 0 and ts <= SLAB and norm1.shape[0] == DEPTH
    n_pages = page_table.shape[1]
    past_len = n_pages * PAGE_SIZE
    conv_rows = 128

    assert w_in.shape[2] == _O_END and _O_NG + W_BLK <= _O_END
    w_in_t = jnp.swapaxes(w_in, 1, 2)
    w_branch_b = _cast_bf16(w_branch.reshape(DEPTH, N_BRANCH * W_MIX, d), rows=512).reshape(w_branch.shape)
    nsa_w1_b = nsa_w1.astype(BF16)
    nsa_w2_b = nsa_w2.astype(BF16)

    xp = x_prompt.reshape(bp * t, d)
    xs = jnp.pad(x_sample, ((0, 0), (0, SLAB - ts), (0, 0))).reshape(bs * SLAB, d)
    cache4 = cache_nsa_kv.reshape(DEPTH, cache_nsa_kv.shape[1], PAGE_ROWS, NSA_HD)
    win_t = jnp.transpose(state_win_kv, (0, 1, 3, 4, 2, 5))
    conv_pad = jnp.pad(state_conv, ((0, 0), (0, 0), (HALO - (CONV_W - 1), 0), (0, 0)))
    ret0_p = jnp.zeros((bp, RET_HEADS, RET_DK, RET_DV), F32)
    conv0_p = jnp.zeros((bp, HALO, W_MIX), F32)
    cos_p, sin_p = _rope_tables(np.arange(t), t)
    cos_s, sin_s = _rope_tables(past_len + np.arange(ts), SLAB)
    c_eff_s = math.gcd(ts, RET_CHUNK)
    assert c_eff_s == ts

    keep_p = min(WINDOW, t)
    keep_s = min(WINDOW, past_len + ts)
    ret_p, ret_s, kv_s, win_p, win_s, conv_p, conv_s, gm_s = ([] for _ in range(8))
    kv_p_rows = None
    for l in range(DEPTH):
        hp = _rmsnorm(xp, norm1[l], slab=1, valid=1, out_dtype=BF16)
        hs = _rmsnorm(xs, norm1[l], slab=SLAB, valid=ts, out_dtype=BF16)
        zap, zas = _dense(hp, hs, w_in_t, layer=l, n=N_Z, tn=1024, tn_head=W_BLK, w_rows=_w_in_source_row,
                          name="in_proj")
        zbp, zbs, zcp, zcs = zap, zas, zap, zas

        o_ret_p, r_p = _retention(zap, ret0_p, ret_gn_w[l], ret_gn_b[l], nb=bp, rows=RET_CHUNK,
                                  n_chunks=t // RET_CHUNK, c_eff=RET_CHUNK, cos=cos_p, sin=sin_p)
        (o_gm_p,) = _gmlp(zap, gm_ln_w[l], gm_ln_b[l], gm_ws[l], gm_bs[l], rows=GM_CHUNK, keep_v=False)
        cmp_kv = _compress_prompt(zap, nsa_pe[l], nsa_w1_b[l], nsa_w2_b[l], nb=bp, t=t)
        o_nsa_p = _nsa_prompt(zap, zcp, cmp_kv, nb=bp, t=t)
        o_conv_p, ext_p = _conv_module(zbp, conv0_p, conv_w[l], conv_b[l], conv_ln_w[l], conv_ln_b[l], nb=bp,
                                       rows=conv_rows, n_tiles=t // conv_rows)
        merged_p = _merge((o_ret_p, o_gm_p, o_nsa_p, o_conv_p), w_branch_b, zbp, layer=l)

        o_ret_s, r_s = _retention(zas, state_ret[l], ret_gn_w[l], ret_gn_b[l], nb=bs, rows=SLAB, n_chunks=1,
                                  c_eff=c_eff_s, cos=cos_s, sin=sin_s)
        o_gm_s, gv_s = _gmlp(zas, gm_ln_w[l], gm_ln_b[l], gm_ws[l], gm_bs[l], rows=SLAB, keep_v=True)
        o_nsa_s = _nsa_sample(zas, zcs, cache4, page_table, win_t, nsa_pe[l], nsa_w1[l], nsa_w2[l], layer=l, nb=bs,
                              valid_rows=ts, q0=past_len)
        o_conv_s, ext_s = _conv_module(zbs, conv_pad[l], conv_w[l], conv_b[l], conv_ln_w[l], conv_ln_b[l], nb=bs,
                                       rows=SLAB, n_tiles=1)
        merged_s = _merge((o_ret_s, o_gm_s, o_nsa_s, o_conv_s), w_branch_b, zbs, layer=l)

        xp, xs = _dense(merged_p, merged_s, w_out, layer=l, n=D_MODEL, tn=512, tn_head=512, resp=xp, ress=xs,
                        name="out_proj")
        h2p = _rmsnorm(xp, norm2[l], slab=1, valid=1, out_dtype=BF16)
        h2s = _rmsnorm(xs, norm2[l], slab=1, valid=1, out_dtype=BF16)
        up, us = _dense(h2p, h2s, w_up, layer=l, n=D_FF, tn=1024, tn_head=512, act="relu2", out_dtype=BF16,
                        name="mlp_up")
        xp, xs = _dense(up, us, w_down, layer=l, n=D_MODEL, tn=1024, tn_head=1024, tk=2048, tk_head=1024, resp=xp, ress=xs,
                        name="mlp_down")

        ret_p.append(r_p)
        ret_s.append(r_s)
        z3p = zap.reshape(bp, t, N_Z)
        kv_p_rows = _kv_rows(zap, kv_p_rows, layer=l)
        win_p.append(z3p[:, t - keep_p:, A_NKV + 8 * NSA_HD:A_NKV + 12 * NSA_HD].reshape(bp, keep_p, 2, NSA_KV, NSA_HD))
        z3s = zas.reshape(bs, SLAB, N_Z)[:, :ts]
        kv_s.append(z3s[:, :, A_NKV:A_NKV + 8 * NSA_HD].reshape(bs, ts, 4, NSA_KV, NSA_HD))
        new_win = z3s[:, :, A_NKV + 8 * NSA_HD:A_NKV + 12 * NSA_HD].reshape(bs, ts, 2, NSA_KV, NSA_HD)
        win_s.append(jnp.concatenate([state_win_kv[l], new_win], axis=1)[:, -keep_s:])
        conv_p.append(ext_p[:, HALO + conv_rows - (CONV_W - 1):HALO + conv_rows])
        conv_s.append(ext_s[:, HALO + ts - (CONV_W - 1):HALO + ts])
        gm_s.append(gv_s.reshape(bs, SLAB, W_MIX)[:, :ts])

    y_p = _rmsnorm(xp, final_norm, slab=1, valid=1, out_dtype=F32)
    y_s = _rmsnorm(xs, final_norm, slab=1, valid=1, out_dtype=F32)
    return (y_p.reshape(bp, t, d), y_s.reshape(bs, SLAB, d)[:, :ts],
            jnp.stack(ret_p), jnp.stack(ret_s), kv_p_rows.reshape(DEPTH, bp, t, 4, NSA_KV, NSA_HD), jnp.stack(kv_s),
            jnp.stack(win_p), jnp.stack(win_s),
            jnp.stack(conv_p), jnp.stack(conv_s), jnp.stack(gm_s))
```
